```python
import math
import jax, jax.numpy as jnp
from jax import lax
import numpy as np

D_MODEL = 1024
BATCH = 2
SEQ = 16384
DEPTH = 4
DEC_BATCH = 2
DEC_SEQ = 8192
PAST_LEN = 128

PLE_DIM = 256
N_BRANCH = 3
BRANCH_WIDTH = 512
EPS = 1e-6
CHUNK = 64

S5_WIDTH = BRANCH_WIDTH
S5_GROUP_CH = 16
S5_GROUPS = S5_WIDTH // S5_GROUP_CH
S5_STATE = 64

GLA_HEADS = 4
GLA_DK = 64
GLA_DV = BRANCH_WIDTH // GLA_HEADS
GLA_RANK = 16
GLA_TAU = 16.0

GDN_HEADS = 4
GDN_DK = 128
GDN_DV = BRANCH_WIDTH // GDN_HEADS
CONV_W = 5

N_EXPERTS = 16
EXPERT_FF = 2048
EC_FACTOR = 2

IN_SPLITS = (S5_WIDTH,
             GLA_HEADS * GLA_DK, GLA_HEADS * GLA_DK, GLA_HEADS * GLA_DV, GLA_HEADS * GLA_DV, 2 * GLA_RANK,
             GDN_HEADS * GDN_DK, GDN_HEADS * GDN_DK, GDN_HEADS * GDN_DV, GDN_HEADS * GDN_DV,
             2 * GDN_HEADS, 2 * GDN_HEADS,
             N_BRANCH * D_MODEL)
D_IN = sum(IN_SPLITS)

kernel_name = 'hybrid_bidir_s5_gla_gdn_ec_encoder'


def rmsnorm(x, g):
    xf = x.astype(jnp.float32)
    return xf * lax.rsqrt(jnp.mean(xf * xf, axis=-1, keepdims=True) + EPS) * g


def l2norm(x):
    return x * lax.rsqrt(jnp.sum(x * x, axis=-1, keepdims=True) + EPS)


def flip_seq(x):
    return jnp.flip(x, axis=1)


def gated_head_norm(o, g, z):
    b, l = o.shape[:2]
    return rmsnorm(o, g).reshape(b, l, -1) * jax.nn.silu(z)


def centred_conv(x, w):
    l = x.shape[1]
    pad = CONV_W // 2
    xp = jnp.pad(x, ((0, 0), (pad, pad), (0, 0)))
    return sum(xp[:, t:t + l] * w[:, t] for t in range(CONV_W))


def _complex_affine_combine(e1, e2):
    a1r, a1i, b1r, b1i = e1
    a2r, a2i, b2r, b2i = e2
    return (a2r * a1r - a2i * a1i,
            a2r * a1i + a2i * a1r,
            a2r * b1r - a2i * b1i + b2r,
            a2r * b1i + a2i * b1r + b2i)


def s5_bidirectional(u, b_re, b_im, c_re, c_im, d_skip, lam_re, lam_im, log_dt, w_glu):
    bsz, l, _ = u.shape
    f32 = jnp.float32
    b_re, b_im, c_re, c_im = (t.astype(f32) for t in (b_re, b_im, c_re, c_im))
    lam_re, lam_im, log_dt = lam_re.astype(f32), lam_im.astype(f32), log_dt.astype(f32)
    ug = u.reshape(bsz, l, S5_GROUPS, S5_GROUP_CH)
    bu_re = jnp.einsum('blgc,gnc->blgn', ug, b_re)
    bu_im = jnp.einsum('blgc,gnc->blgn', ug, b_im)
    h_re = jnp.zeros_like(bu_re)
    h_im = jnp.zeros_like(bu_im)
    for direction in range(2):
        dt = jnp.exp(log_dt[direction])[:, None]
        lr, li = lam_re[direction], lam_im[direction]
        mag = jnp.exp(lr * dt)
        ab_re, ab_im = mag * jnp.cos(li * dt), mag * jnp.sin(li * dt)
        den = lr * lr + li * li
        num_re = ab_re - 1.0
        coef_re = (num_re * lr + ab_im * li) / den
        coef_im = (ab_im * lr - num_re * li) / den
        x_re = coef_re * bu_re - coef_im * bu_im
        x_im = coef_re * bu_im + coef_im * bu_re
        a_re = jnp.broadcast_to(ab_re, x_re.shape)
        a_im = jnp.broadcast_to(ab_im, x_re.shape)
        _, _, s_re, s_im = lax.associative_scan(
            _complex_affine_combine, (a_re, a_im, x_re, x_im), axis=1, reverse=(direction == 1))
        h_re = h_re + s_re
        h_im = h_im + s_im
    y = jnp.einsum('gcn,blgn->blgc', c_re, h_re) - jnp.einsum('gcn,blgn->blgc', c_im, h_im)
    y = y.reshape(bsz, l, S5_WIDTH) + d_skip * u
    y = jax.nn.gelu(y)
    return y * jax.nn.sigmoid(y @ w_glu)


def gla_chunked(q, k, v, g):
    bsz, l, h, dk = q.shape
    dv = v.shape[-1]
    n = l // CHUNK
    q = q.reshape(bsz, n, CHUNK, h, dk)
    k = k.reshape(bsz, n, CHUNK, h, dk)
    v = v.reshape(bsz, n, CHUNK, h, dv)
    g_cum = jnp.cumsum(g.reshape(bsz, n, CHUNK, h, dk), axis=2)
    q_dec = q * jnp.exp(g_cum)
    k_inv = k * jnp.exp(-g_cum)
    idx = jnp.arange(CHUNK)
    incl = idx[:, None] >= idx[None, :]
    scores = jnp.where(incl, jnp.einsum('bnihd,bnjhd->bnhij', q_dec, k_inv), 0.0)
    o_intra = jnp.einsum('bnhij,bnjhv->bnihv', scores, v)
    k_tail = k * jnp.exp(g_cum[:, :, -1:] - g_cum)
    kv = jnp.einsum('bnchd,bnchv->bnhdv', k_tail, v)
    chunk_decay = jnp.exp(g_cum[:, :, -1])

    def step(state, inp):
        dec, kv_n = inp
        return state * dec[..., None] + kv_n, state

    s0 = jnp.zeros((bsz, h, dk, dv), kv.dtype)
    _, s_prev = lax.scan(step, s0, (jnp.moveaxis(chunk_decay, 1, 0), jnp.moveaxis(kv, 1, 0)))
    s_prev = jnp.moveaxis(s_prev, 0, 1)
    o_inter = jnp.einsum('bnchd,bnhdv->bnchv', q_dec, s_prev)
    return (o_intra + o_inter).reshape(bsz, l, h, dv)


def gdn_chunked(q, k, v, beta, g):
    bsz, l, h, dk = q.shape
    dv = v.shape[-1]
    n = l // CHUNK

    def blocks(x):
        return jnp.moveaxis(x.reshape((bsz, n, CHUNK, h) + x.shape[3:]), 3, 2)

    q, k, v, beta, g = blocks(q), blocks(k), blocks(v), blocks(beta), blocks(g)
    gam = jnp.cumsum(g, axis=-1)
    idx = jnp.arange(CHUNK)
    incl = idx[:, None] >= idx[None, :]
    strict = idx[:, None] > idx[None, :]
    decay = jnp.where(incl, jnp.exp(jnp.where(incl, gam[..., :, None] - gam[..., None, :], 0.0)), 0.0)
    k_beta = k * beta[..., None]
    lower = jnp.where(strict, jnp.einsum('bnhid,bnhjd->bnhij', k_beta, k) * decay, 0.0)
    eye = jnp.eye(CHUNK, dtype=lower.dtype)
    rhs = jnp.concatenate([v * beta[..., None], k_beta * jnp.exp(gam)[..., None]], axis=-1)
    sol = lax.linalg.triangular_solve(eye + lower, rhs, left_side=True, lower=True, unit_diagonal=True)
    u_c, w_c = sol[..., :dv], sol[..., dv:]
    attn = jnp.einsum('bnhid,bnhjd->bnhij', q, k) * decay
    q_dec = q * jnp.exp(gam)[..., None]
    k_tail = k * jnp.exp(gam[..., -1:] - gam)[..., None]
    chunk_decay = jnp.exp(gam[..., -1])

    def step(state, inp):
        u_n, w_n, qd_n, kt_n, a_n, cd_n = inp
        v_new = u_n - jnp.einsum('bhcd,bhdv->bhcv', w_n, state)
        o_n = jnp.einsum('bhcd,bhdv->bhcv', qd_n, state) + jnp.einsum('bhij,bhjv->bhiv', a_n, v_new)
        state = state * cd_n[..., None, None] + jnp.einsum('bhcd,bhcv->bhdv', kt_n, v_new)
        return state, o_n

    xs = tuple(jnp.moveaxis(t, 1, 0) for t in (u_c, w_c, q_dec, k_tail, attn, chunk_decay))
    s0 = jnp.zeros((bsz, h, dk, dv), u_c.dtype)
    _, o = lax.scan(step, s0, xs)
    return jnp.moveaxis(o, 0, 1).transpose(0, 1, 3, 2, 4).reshape(bsz, l, h, dv)


def expert_choice_ffn(hn, w_router, w_gate, w_up, w_down):
    bsz, l, d = hn.shape
    tokens = hn.reshape(-1, d)
    n_tok = tokens.shape[0]
    cap = max(1, EC_FACTOR * n_tok // N_EXPERTS)
    aff = jax.nn.softmax((tokens @ w_router).astype(jnp.float32), axis=-1)
    gate_vals, idx = lax.top_k(aff.T, cap)
    xe = tokens[idx]
    hid = jax.nn.silu(jnp.einsum('ecd,edf->ecf', xe, w_gate)) * jnp.einsum('ecd,edf->ecf', xe, w_up)
    ye = jnp.einsum('ecf,efd->ecd', hid, w_down) * gate_vals[..., None]
    out = jnp.zeros((n_tok, d), ye.dtype).at[idx.reshape(-1)].add(ye.reshape(-1, d))
    return out.reshape(bsz, l, d)


def encoder_layer(h, p_i, w, i):
    bsz, l, _ = h.shape
    hn = rmsnorm(h, w['norm_mix'][i])
    proj = hn @ w['w_in'][i]
    (u_s5, gla_q, gla_k, gla_v, gla_r, gla_lr, gdn_q, gdn_k, gdn_v, gdn_z, gdn_a, gdn_b,
     gate_logits) = jnp.split(proj, np.cumsum(IN_SPLITS)[:-1].tolist(), axis=-1)

    y_s5 = s5_bidirectional(u_s5, w['s5_B_re'][i], w['s5_B_im'][i], w['s5_C_re'][i], w['s5_C_im'][i],
                            w['s5_D'][i], w['s5_lam_re'][i], w['s5_lam_im'][i], w['s5_log_dt'][i],
                            w['s5_w_glu'][i])

    q = gla_q.reshape(bsz, l, GLA_HEADS, GLA_DK) * GLA_DK ** -0.5
    k = gla_k.reshape(bsz, l, GLA_HEADS, GLA_DK)
    v = gla_v.reshape(bsz, l, GLA_HEADS, GLA_DV)
    lr = gla_lr.reshape(bsz, l, 2, GLA_RANK)
    g_log = jax.nn.log_sigmoid(jnp.einsum('blsr,srd->blsd', lr, w['gla_w_gate'][i]) + w['gla_b_gate'][i]) / GLA_TAU
    g_fwd = g_log[:, :, 0].reshape(bsz, l, GLA_HEADS, GLA_DK)
    g_bwd = g_log[:, :, 1].reshape(bsz, l, GLA_HEADS, GLA_DK)
    o_gla = gla_chunked(q, k, v, g_fwd) + flip_seq(
        gla_chunked(flip_seq(q), flip_seq(k), flip_seq(v), flip_seq(g_bwd)))
    y_gla = gated_head_norm(o_gla, w['gla_norm'][i], gla_r)

    qkv = jax.nn.silu(centred_conv(jnp.concatenate([gdn_q, gdn_k, gdn_v], axis=-1), w['gdn_conv'][i]))
    cq, ck, cv = jnp.split(qkv, [GDN_HEADS * GDN_DK, 2 * GDN_HEADS * GDN_DK], axis=-1)
    cq = l2norm(cq.reshape(bsz, l, GDN_HEADS, GDN_DK)) * GDN_DK ** -0.5
    ck = l2norm(ck.reshape(bsz, l, GDN_HEADS, GDN_DK))
    cv = cv.reshape(bsz, l, GDN_HEADS, GDN_DV)
    a = gdn_a.reshape(bsz, l, 2, GDN_HEADS)
    decay_log = -jnp.exp(w['gdn_A_log'][i]) * jax.nn.softplus(a + w['gdn_dt_bias'][i])
    beta = jax.nn.sigmoid(gdn_b.reshape(bsz, l, 2, GDN_HEADS))
    o_gdn = gdn_chunked(cq, ck, cv, beta[:, :, 0], decay_log[:, :, 0]) + flip_seq(
        gdn_chunked(flip_seq(cq), flip_seq(ck), flip_seq(cv), flip_seq(beta[:, :, 1]), flip_seq(decay_log[:, :, 1])))
    y_gdn = gated_head_norm(o_gdn, w['gdn_norm'][i], gdn_z)

    branches = jnp.stack([y_s5, y_gla, y_gdn], axis=2)
    gates = jax.nn.sigmoid(gate_logits.reshape(bsz, l, N_BRANCH, D_MODEL))
    merged = jnp.einsum('blrd,blrd->bld', gates, jnp.einsum('blrc,rcd->blrd', branches, w['w_branch'][i]))
    h = h + merged @ w['w_out'][i]

    h = h + expert_choice_ffn(rmsnorm(h, w['norm_ffn'][i]), w['w_router'][i], w['w_exp_gate'][i],
                              w['w_exp_up'][i], w['w_exp_down'][i])

    ple_gate = jax.nn.sigmoid(rmsnorm(h, w['norm_ple'][i]) @ w['w_ple_gate'][i])
    return h + ple_gate * (p_i @ w['w_ple_proj'][i])


def encoder_trunk(x, p, w):
    h = x.astype(jnp.float32)
    for i in range(DEPTH):
        h = encoder_layer(h, p[i], w, i)
    return rmsnorm(h, w['norm_final']).astype(x.dtype)


def setup_inputs(seed: int = 0) -> dict:
    key = jax.random.key(seed)
    ks = iter(jax.random.split(key, 40))
    f32 = jnp.float32

    def nrm(shape, scale):
        return jax.random.normal(next(ks), shape, f32) * scale

    def gain(shape):
        return 1.0 + 0.01 * jax.random.normal(next(ks), shape, f32)

    def unif(shape, lo, hi):
        return jax.random.uniform(next(ks), shape, f32, lo, hi)

    n_idx = jnp.arange(S5_STATE, dtype=f32)
    dt_gdn = jnp.exp(unif((DEPTH, 2, GDN_HEADS), math.log(1e-3), math.log(1e-1)))
    return {
        'x_prompt': nrm((BATCH, SEQ, D_MODEL), 1.0),
        'x_sample': nrm((DEC_BATCH, DEC_SEQ, D_MODEL), 1.0),
        'p_prompt': nrm((DEPTH, BATCH, SEQ, PLE_DIM), 1.0),
        'p_sample': nrm((DEPTH, DEC_BATCH, DEC_SEQ, PLE_DIM), 1.0),
        'norm_mix': gain((DEPTH, D_MODEL)),
        'w_in': nrm((DEPTH, D_MODEL, D_IN), D_MODEL ** -0.5),
        's5_B_re': nrm((DEPTH, S5_GROUPS, S5_STATE, S5_GROUP_CH), (2.0 * S5_GROUP_CH) ** -0.5),
        's5_B_im': nrm((DEPTH, S5_GROUPS, S5_STATE, S5_GROUP_CH), (2.0 * S5_GROUP_CH) ** -0.5),
        's5_C_re': nrm((DEPTH, S5_GROUPS, S5_GROUP_CH, S5_STATE), 0.5),
        's5_C_im': nrm((DEPTH, S5_GROUPS, S5_GROUP_CH, S5_STATE), 0.5),
        's5_D': nrm((DEPTH, S5_WIDTH), 0.5),
        's5_lam_re': -0.5 + nrm((DEPTH, 2, S5_GROUPS, S5_STATE), 0.01),
        's5_lam_im': math.pi * n_idx + nrm((DEPTH, 2, S5_GROUPS, S5_STATE), 0.01),
        's5_log_dt': unif((DEPTH, 2, S5_GROUPS), math.log(1e-3), math.log(1e-1)),
        's5_w_glu': nrm((DEPTH, S5_WIDTH, S5_WIDTH), S5_WIDTH ** -0.5),
        'gla_w_gate': nrm((DEPTH, 2, GLA_RANK, GLA_HEADS * GLA_DK), GLA_RANK ** -0.5),
        'gla_b_gate': nrm((DEPTH, 2, GLA_HEADS * GLA_DK), 0.01),
        'gla_norm': gain((DEPTH, GLA_DV)),
        'gdn_conv': nrm((DEPTH, 2 * GDN_HEADS * GDN_DK + GDN_HEADS * GDN_DV, CONV_W), CONV_W ** -0.5),
        'gdn_A_log': jnp.log(unif((DEPTH, 2, GDN_HEADS), 1.0, 16.0)),
        'gdn_dt_bias': dt_gdn + jnp.log(-jnp.expm1(-dt_gdn)),
        'gdn_norm': gain((DEPTH, GDN_DV)),
        'w_branch': nrm((DEPTH, N_BRANCH, BRANCH_WIDTH, D_MODEL), BRANCH_WIDTH ** -0.5),
        'w_out': nrm((DEPTH, D_MODEL, D_MODEL), D_MODEL ** -0.5),
        'norm_ffn': gain((DEPTH, D_MODEL)),
        'w_router': nrm((DEPTH, D_MODEL, N_EXPERTS), D_MODEL ** -0.5),
        'w_exp_gate': nrm((DEPTH, N_EXPERTS, D_MODEL, EXPERT_FF), D_MODEL ** -0.5),
        'w_exp_up': nrm((DEPTH, N_EXPERTS, D_MODEL, EXPERT_FF), D_MODEL ** -0.5),
        'w_exp_down': nrm((DEPTH, N_EXPERTS, EXPERT_FF, D_MODEL), EXPERT_FF ** -0.5),
        'norm_ple': gain((DEPTH, D_MODEL)),
        'w_ple_gate': nrm((DEPTH, D_MODEL, D_MODEL), D_MODEL ** -0.5),
        'w_ple_proj': nrm((DEPTH, PLE_DIM, D_MODEL), PLE_DIM ** -0.5),
        'norm_final': gain((D_MODEL,)),
    }


def reference(x_prompt, x_sample, p_prompt, p_sample, norm_mix, w_in,
              s5_B_re, s5_B_im, s5_C_re, s5_C_im, s5_D, s5_lam_re, s5_lam_im, s5_log_dt, s5_w_glu,
              gla_w_gate, gla_b_gate, gla_norm,
              gdn_conv, gdn_A_log, gdn_dt_bias, gdn_norm,
              w_branch, w_out, norm_ffn, w_router, w_exp_gate, w_exp_up, w_exp_down,
              norm_ple, w_ple_gate, w_ple_proj, norm_final):
    w = dict(norm_mix=norm_mix, w_in=w_in,
             s5_B_re=s5_B_re, s5_B_im=s5_B_im, s5_C_re=s5_C_re, s5_C_im=s5_C_im, s5_D=s5_D,
             s5_lam_re=s5_lam_re, s5_lam_im=s5_lam_im, s5_log_dt=s5_log_dt, s5_w_glu=s5_w_glu,
             gla_w_gate=gla_w_gate, gla_b_gate=gla_b_gate, gla_norm=gla_norm,
             gdn_conv=gdn_conv, gdn_A_log=gdn_A_log, gdn_dt_bias=gdn_dt_bias, gdn_norm=gdn_norm,
             w_branch=w_branch, w_out=w_out, norm_ffn=norm_ffn, w_router=w_router,
             w_exp_gate=w_exp_gate, w_exp_up=w_exp_up, w_exp_down=w_exp_down,
             norm_ple=norm_ple, w_ple_gate=w_ple_gate, w_ple_proj=w_ple_proj, norm_final=norm_final)
    y_prompt = encoder_trunk(x_prompt, p_prompt, w)
    y_sample = encoder_trunk(x_sample, p_sample, w)
    return (y_prompt, y_sample)
```

```python
import functools
import math

import jax
import jax.numpy as jnp
import numpy as np
from jax import lax
from jax.experimental import pallas as pl
from jax.experimental.pallas import tpu as pltpu

f32 = jnp.float32
bf16 = jnp.bfloat16
i32 = jnp.int32
HIGHEST = lax.Precision.HIGHEST

D_MODEL = 1024
DEPTH = 4
PLE_DIM = 256
BW = 512
EPS = 1e-6
CHUNK = 64
S5_GROUPS, S5_GC, S5_STATE = 32, 16, 64
GLA_H, GLA_DK, GLA_DV, GLA_RANK, GLA_TAU = 4, 64, 128, 16, 16.0
GDN_H, GDN_DK, GDN_DV, CONV_W = 4, 128, 128, 5
N_EXPERTS, EXPERT_FF, EC_FACTOR = 16, 2048, 2

LANES = 128
SUBLANES = 8
VMEM_LIMIT_BYTES = 56 * 1024 * 1024

C_GATE, C_S5, C_GLA_V, C_GLA_R = 0, 3072, 3584, 4096
C_GDN_QKV, C_GDN_Z, C_GLA_Q, C_GLA_K, C_GLA_LR, C_GDN_AB = 4608, 6144, 6656, 6912, 7168, 7296
D_INP = 7424

S5_T = 8
S5_LB = BW // LANES
S5_SW = 8 * S5_STATE * 2

MOE_TT = 512
MOE_SB = 256


def _cparams(*sem):
    return pltpu.CompilerParams(dimension_semantics=sem, vmem_limit_bytes=VMEM_LIMIT_BYTES)


def _nt(a, b):
    return lax.dot_general(a, b, (((1,), (1,)), ((), ())), preferred_element_type=f32)


def _tn(a, b):
    return lax.dot_general(a, b, (((0,), (0,)), ((), ())), preferred_element_type=f32)


def _mm(a, b):
    return jnp.dot(a, b, preferred_element_type=f32)


def _mm_hi(a, b):
    return jnp.dot(a, b, preferred_element_type=f32, precision=HIGHEST)


def _ind(mask, dtype=f32):
    return jnp.where(mask, 1.0, 0.0).astype(dtype)


def _sigmoid(x):
    return 1.0 / (1.0 + jnp.exp(-x))


def _silu(x):
    return x * _sigmoid(x)


def _gelu_tanh(x):
    return 0.5 * x * (1.0 + jnp.tanh(math.sqrt(2.0 / math.pi) * (x + 0.044715 * (x * x * x))))


def _rms(x, g):
    return x * lax.rsqrt(jnp.mean(x * x, axis=-1, keepdims=True) + EPS) * g


def _inproj_body(x_ref, g_ref, w_ref, o_ref):
    xn = _rms(x_ref[...], g_ref[...]).astype(bf16)
    o_ref[...] = _mm(xn, w_ref[...])


def _inproj(x2d, gain, w_p, tm=256):
    m = x2d.shape[0]
    return pl.pallas_call(
        _inproj_body, name="inproj",
        out_shape=jax.ShapeDtypeStruct((m, D_INP), f32),
        grid=(m // tm,),
        in_specs=[pl.BlockSpec((tm, D_MODEL), lambda i: (i, 0)),
                  pl.BlockSpec((1, D_MODEL), lambda i: (0, 0)),
                  pl.BlockSpec((D_MODEL, D_INP), lambda i: (0, 0), pipeline_mode=pl.Buffered(1))],
        out_specs=pl.BlockSpec((tm, D_INP), lambda i: (i, 0)),
        compiler_params=_cparams("parallel"),
    )(x2d, gain, w_p)


def _reorder_w_in(w_in):
    o = np.cumsum((0, 512, 256, 256, 512, 512, 32, 512, 512, 512, 512, 8, 8, 3072))
    seg = lambda k: w_in[:, o[k]:o[k + 1]]
    zpad = lambda a: jnp.pad(a, ((0, 0), (0, LANES - a.shape[1])))
    parts = [seg(12), seg(0), seg(3), seg(4), seg(6), seg(7), seg(8), seg(9), seg(1), seg(2),
             zpad(seg(5)), zpad(jnp.concatenate([seg(10), seg(11)], axis=1))]
    return jnp.concatenate(parts, axis=1).astype(bf16)


def _s5_weights(b_re, b_im, c_re, c_im, lam_re, lam_im, log_dt):
    T = S5_T
    dt = jnp.exp(log_dt)[:, :, None]
    lr, li = lam_re, lam_im
    mag = jnp.exp(lr * dt)
    ab_re, ab_im = mag * jnp.cos(li * dt), mag * jnp.sin(li * dt)
    den = lr * lr + li * li
    num_re = ab_re - 1.0
    coef_re = (num_re * lr + ab_im * li) / den
    coef_im = (ab_im * lr - num_re * li) / den
    xb_re = coef_re[..., None] * b_re[None] - coef_im[..., None] * b_im[None]
    xb_im = coef_re[..., None] * b_im[None] + coef_im[..., None] * b_re[None]

    def powers(taus):
        tau = jnp.asarray(taus, lr.dtype)
        pm = jnp.exp((lr * dt)[..., None] * tau)
        ang = (li * dt)[..., None] * tau
        return pm * jnp.cos(ang), pm * jnp.sin(ang)

    p_re, p_im = powers(np.arange(T + 1))
    cp_re = c_re[None, :, :, :, None] * p_re[:, :, None] - c_im[None, :, :, :, None] * p_im[:, :, None]
    cp_im = c_re[None, :, :, :, None] * p_im[:, :, None] + c_im[None, :, :, :, None] * p_re[:, :, None]
    kern = (jnp.einsum('dgknt,dgnc->dgtkc', cp_re, xb_re, precision=HIGHEST)
            - jnp.einsum('dgknt,dgnc->dgtkc', cp_im, xb_im, precision=HIGHEST))
    s_idx = np.arange(T)[:, None]
    t_idx = np.arange(T)[None, :]
    lag_f = np.clip(t_idx - s_idx, 0, T)
    lag_b = np.clip(s_idx - t_idx, 0, T)
    m_f = jnp.asarray((t_idx >= s_idx), kern.dtype)[None, :, :, None, None]
    m_b = jnp.asarray((s_idx >= t_idx), kern.dtype)[None, :, :, None, None]
    a_g = kern[0][:, lag_f] * m_f + kern[1][:, lag_b] * m_b
    eye8 = jnp.eye(8, dtype=kern.dtype)
    a_g = a_g.reshape(S5_LB, 8, T, T, S5_GC, S5_GC)
    a_blk = jnp.einsum('jgstkc,gh->jsgcthk', a_g, eye8).reshape(S5_LB, T * LANES, T * LANES)

    def state_in(d, taus):
        e_re = p_re[d][:, :, taus][..., None] * xb_re[d][:, :, None, :] - p_im[d][:, :, taus][..., None] * xb_im[d][:, :, None, :]
        e_im = p_re[d][:, :, taus][..., None] * xb_im[d][:, :, None, :] + p_im[d][:, :, taus][..., None] * xb_re[d][:, :, None, :]
        e = jnp.stack([e_re, e_im], axis=0)
        e = e.reshape(2, S5_LB, 8, S5_STATE, T, S5_GC)
        m = jnp.einsum('pjgnsc,gh->jsgchpn', e, eye8)
        m = m.reshape(S5_LB, T, 8, S5_GC, 4, 2, 2, S5_STATE)
        m = jnp.transpose(m, (0, 1, 2, 3, 4, 6, 5, 7))
        return m.reshape(S5_LB, T * LANES, S5_SW)

    m_f_w = state_in(0, np.arange(T - 1, -1, -1))
    m_b_w = state_in(1, np.arange(T))

    def state_out(d, taus):
        r = cp_re[d][..., taus]
        im = -cp_im[d][..., taus]
        w = jnp.stack([r, im], axis=0).reshape(2, S5_LB, 8, S5_GC, S5_STATE, T)
        w = jnp.einsum('pjhknt,gh->jhpntgk', w, eye8)
        w = w.reshape(S5_LB, 4, 2, 2, S5_STATE, T, 8, S5_GC)
        w = jnp.transpose(w, (0, 1, 3, 2, 4, 5, 6, 7))
        return w.reshape(S5_LB, S5_SW, T * LANES)

    n_f_w = state_out(0, np.arange(1, T + 1))
    n_b_w = state_out(1, np.arange(T, 0, -1))

    q_re, q_im = powers(T * np.arange(8))

    def table(arr_re, arr_im, d, order):
        t = jnp.stack([arr_re[d][..., order], arr_im[d][..., order]], axis=0)
        t = t.reshape(2, 16, 2, S5_STATE, len(order))
        return jnp.transpose(t, (4, 1, 0, 2, 3)).reshape(len(order), 16 * 2 * LANES)

    asc = np.arange(8)
    dbl = np.array([1, 2, 4, 0, 0, 0, 0, 0])
    tabs = (table(q_re, q_im, 0, asc), table(q_re, q_im, 0, dbl),
            table(q_re, q_im, 1, asc[::-1]), table(q_re, q_im, 1, dbl))
    return (a_blk.astype(bf16), m_f_w.astype(bf16), m_b_w.astype(bf16), n_f_w.astype(bf16), n_b_w.astype(bf16),
            tuple(t.astype(f32) for t in tabs))


def _s5_load_chunks(u_ref):
    rows = u_ref.shape[0] // S5_T
    parts = [u_ref[pl.ds(s, rows, stride=S5_T), :] for s in range(S5_T)]
    return jnp.concatenate(parts, axis=1).astype(bf16)


def _s5_in_body(u_ref, mf_ref, mb_ref, xf_ref, xb_ref):
    lhs = _s5_load_chunks(u_ref)
    xf_ref[...] = _mm(lhs, mf_ref[...])
    xb_ref[...] = _mm(lhs, mb_ref[...])


def _s5_in(proj, m_f_w, m_b_w, tl):
    b, l, _ = proj.shape
    nc = l // S5_T
    out = jax.ShapeDtypeStruct((b, nc, S5_LB * S5_SW), f32)
    wspec = pl.BlockSpec((None, S5_T * LANES, S5_SW), lambda bi, j, t: (j, 0, 0))
    ospec = pl.BlockSpec((None, tl // S5_T, S5_SW), lambda bi, j, t: (bi, t, j))
    return pl.pallas_call(
        _s5_in_body, name="s5_in",
        out_shape=(out, out),
        grid=(b, S5_LB, l // tl),
        in_specs=[pl.BlockSpec((None, tl, LANES), lambda bi, j, t: (bi, t, C_S5 // LANES + j)), wspec, wspec],
        out_specs=(ospec, ospec),
        compiler_params=_cparams("parallel", "parallel", "parallel"),
    )(proj, m_f_w, m_b_w)


def _s5_scan_body(xf_ref, xb_ref, cf_ref, hf_ref, cb_ref, hb_ref, sf_ref, sb_ref):
    n8 = xf_ref.shape[0] // SUBLANES
    rows = lax.broadcasted_iota(i32, (SUBLANES, LANES), 0)
    re, im = slice(0, LANES), slice(LANES, 2 * LANES)

    def shifted(x, d, fwd):
        if fwd:
            return jnp.where(rows >= d, pltpu.roll(x, d, 0), 0.0)
        return jnp.where(rows < SUBLANES - d, pltpu.roll(x, SUBLANES - d, 0), 0.0)

    def local_scan(xr, xi, h_ref, fwd):
        er, ei = shifted(xr, 1, fwd), shifted(xi, 1, fwd)
        for k, d in enumerate((1, 2, 4)):
            ar, ai = h_ref[k:k + 1, re], h_ref[k:k + 1, im]
            sr, si = shifted(er, d, fwd), shifted(ei, d, fwd)
            er, ei = er + ar * sr - ai * si, ei + ar * si + ai * sr
        return er, ei

    def tile(x_ref, c_ref, h_ref, o_ref, i, sr, si, fwd):
        r0 = pl.multiple_of(i * SUBLANES, SUBLANES)
        xr, xi = x_ref[pl.ds(r0, SUBLANES), re], x_ref[pl.ds(r0, SUBLANES), im]
        er, ei = local_scan(xr, xi, h_ref, fwd)
        cr, ci = c_ref[:, re], c_ref[:, im]
        outr = er + cr * sr - ci * si
        outi = ei + cr * si + ci * sr
        o_ref[pl.ds(r0, SUBLANES), re] = outr
        o_ref[pl.ds(r0, SUBLANES), im] = outi
        e = SUBLANES - 1 if fwd else 0
        ar, ai = h_ref[0:1, re], h_ref[0:1, im]
        nr = ar * outr[e:e + 1] - ai * outi[e:e + 1] + xr[e:e + 1]
        ni = ar * outi[e:e + 1] + ai * outr[e:e + 1] + xi[e:e + 1]
        return nr, ni

    def step(i, carry):
        fr, fi, br, bi = carry
        fr, fi = tile(xf_ref, cf_ref, hf_ref, sf_ref, i, fr, fi, True)
        br, bi = tile(xb_ref, cb_ref, hb_ref, sb_ref, n8 - 1 - i, br, bi, False)
        return fr, fi, br, bi

    z = jnp.zeros((1, LANES), f32)
    lax.fori_loop(0, n8, step, (z, z, z, z))


def _s5_scan(xf, xb, tabs):
    b, nc, w = xf.shape
    ncb = w // (2 * LANES)
    xspec = pl.BlockSpec((None, nc, 2 * LANES), lambda bi, c: (bi, 0, c))
    tspec = pl.BlockSpec((SUBLANES, 2 * LANES), lambda bi, c: (0, c))
    out = jax.ShapeDtypeStruct((b, nc, w), f32)
    return pl.pallas_call(
        _s5_scan_body, name="s5_scan",
        out_shape=(out, out),
        grid=(b, ncb),
        in_specs=[xspec, xspec, tspec, tspec, tspec, tspec],
        out_specs=(xspec, xspec),
        compiler_params=_cparams("parallel", "parallel"),
    )(xf, xb, *tabs)


def _s5_out_body(u_ref, sf_ref, sb_ref, a_ref, nf_ref, nb_ref, y_ref):
    lhs = _s5_load_chunks(u_ref)
    y = (_mm(lhs, a_ref[...]) + _mm(sf_ref[...].astype(bf16), nf_ref[...])
         + _mm(sb_ref[...].astype(bf16), nb_ref[...]))
    rows = y.shape[0]
    for t in range(S5_T):
        y_ref[pl.ds(t, rows, stride=S5_T), :] = y[:, t * LANES:(t + 1) * LANES]


def _s5_out(proj, sf, sb, a_blk, n_f_w, n_b_w, tl):
    b, l, _ = proj.shape
    sspec = pl.BlockSpec((None, tl // S5_T, S5_SW), lambda bi, j, t: (bi, t, j))
    return pl.pallas_call(
        _s5_out_body, name="s5_out",
        out_shape=jax.ShapeDtypeStruct((b, l, BW), f32),
        grid=(b, S5_LB, l // tl),
        in_specs=[pl.BlockSpec((None, tl, LANES), lambda bi, j, t: (bi, t, C_S5 // LANES + j)), sspec, sspec,
                  pl.BlockSpec((None, S5_T * LANES, S5_T * LANES), lambda bi, j, t: (j, 0, 0)),
                  pl.BlockSpec((None, S5_SW, S5_T * LANES), lambda bi, j, t: (j, 0, 0)),
                  pl.BlockSpec((None, S5_SW, S5_T * LANES), lambda bi, j, t: (j, 0, 0))],
        out_specs=pl.BlockSpec((None, tl, LANES), lambda bi, j, t: (bi, t, j)),
        compiler_params=_cparams("parallel", "parallel", "parallel"),
    )(proj, sf, sb, a_blk, n_f_w, n_b_w)


def _s5_mixer(proj, s5w, tl=2048):
    a_blk, m_f_w, m_b_w, n_f_w, n_b_w, tabs = s5w
    tl = min(tl, proj.shape[1])
    xf, xb = _s5_in(proj, m_f_w, m_b_w, tl)
    sf, sb = _s5_scan(xf, xb, tabs)
    return _s5_out(proj, sf, sb, a_blk, n_f_w, n_b_w, tl)


def _gla_body(q_ref, k_ref, v_ref, lr_ref, wg_ref, bg_ref, o_ref, st_ref, *, fwd):
    nch = q_ref.shape[0] // CHUNK
    hk = GLA_H * GLA_DK
    hv = GLA_H * GLA_DV

    @pl.when(pl.program_id(1) == 0)
    def _():
        st_ref[...] = jnp.zeros_like(st_ref)

    r64 = lax.broadcasted_iota(i32, (CHUNK, CHUNK), 0)
    c64 = lax.broadcasted_iota(i32, (CHUNK, CHUNK), 1)
    tri = _ind((r64 >= c64) if fwd else (r64 <= c64))
    rr = lax.broadcasted_iota(i32, (CHUNK, hk), 0)
    cc = lax.broadcasted_iota(i32, (CHUNK, hk), 1) % CHUNK
    causal = (rr >= cc) if fwd else (rr <= cc)
    kr = lax.broadcasted_iota(i32, (hk, hk), 0) // CHUNK
    kc = lax.broadcasted_iota(i32, (hk, hk), 1) // GLA_DK
    kmask = kr == kc
    vr = lax.broadcasted_iota(i32, (hk, hv), 0) // CHUNK
    vc = lax.broadcasted_iota(i32, (hk, hv), 1) // GLA_DV
    vmask = vr == vc
    sr_ = lax.broadcasted_iota(i32, (hv, hk), 0) // GLA_DV
    sc_ = lax.broadcasted_iota(i32, (hv, hk), 1) // GLA_DK
    smask = sr_ == sc_
    scale = GLA_DK ** -0.5

    def chunk(ci, carry):
        c = ci if fwd else nch - 1 - ci
        r0 = pl.multiple_of(c * CHUNK, CHUNK)
        q = q_ref[pl.ds(r0, CHUNK), :]
        k = k_ref[pl.ds(r0, CHUNK), :]
        v = v_ref[pl.ds(r0, CHUNK), :].astype(bf16)
        gl = _mm_hi(lr_ref[pl.ds(r0, CHUNK), :], wg_ref[...]) + bg_ref[...]
        g = (jnp.minimum(gl, 0.0) - jnp.log(1.0 + jnp.exp(-jnp.abs(gl)))) * (1.0 / GLA_TAU)
        gc = _mm_hi(tri, g)
        gtot = gc[CHUNK - 1:CHUNK, :] if fwd else gc[0:1, :]
        qd = (q * scale * jnp.exp(gc)).astype(bf16)
        ki = (k * jnp.exp(-gc)).astype(bf16)
        kt = (k * jnp.exp(gtot - gc)).astype(bf16)
        kstack = jnp.where(kmask, jnp.concatenate([ki] * GLA_H, axis=0), jnp.zeros((), bf16))
        sc = jnp.where(causal, _nt(qd, kstack), 0.0).astype(bf16)
        vbd = jnp.where(vmask, jnp.concatenate([v] * GLA_H, axis=0), jnp.zeros((), bf16))
        st = st_ref[...]
        o_ref[pl.ds(r0, CHUNK), :] = _mm(sc, vbd) + _nt(qd, st.astype(bf16))
        st_ref[...] = st * jnp.exp(gtot) + jnp.where(smask, _tn(v, kt), 0.0)
        return carry

    lax.fori_loop(0, nch, chunk, 0)


def _gla(proj, wg, bg, fwd, blk=512):
    b, l, _ = proj.shape
    blk = min(blk, l)
    nb = l // blk
    hk, hv = GLA_H * GLA_DK, GLA_H * GLA_DV
    bidx = (lambda i: i) if fwd else (lambda i: nb - 1 - i)
    col = lambda width, off: pl.BlockSpec((None, blk, width), lambda bi, i: (bi, bidx(i), off // width))
    return pl.pallas_call(
        functools.partial(_gla_body, fwd=fwd), name="gla_fwd" if fwd else "gla_bwd",
        out_shape=jax.ShapeDtypeStruct((b, l, hv), f32),
        grid=(b, nb),
        in_specs=[col(hk, C_GLA_Q), col(hk, C_GLA_K), col(hv, C_GLA_V), col(LANES, C_GLA_LR),
                  pl.BlockSpec((LANES, hk), lambda bi, i: (0, 0)),
                  pl.BlockSpec((1, hk), lambda bi, i: (0, 0))],
        out_specs=pl.BlockSpec((None, blk, hv), lambda bi, i: (bi, bidx(i), 0)),
        scratch_shapes=[pltpu.VMEM((hv, hk), f32)],
        compiler_params=_cparams("parallel", "arbitrary"),
    )(proj, proj, proj, proj, wg, bg)


def _gdn_prep_body(x_ref, xp_ref, xn_ref, ab_ref, cw_ref, par_ref, qkv_ref, gb_ref):
    i, n = pl.program_id(1), pl.num_programs(1)
    blk = x_ref.shape[0]
    prev = jnp.where(i > 0, xp_ref[...], 0.0)
    nxt = jnp.where(i < n - 1, xn_ref[...], 0.0)
    ext = jnp.concatenate([prev, x_ref[...], nxt], axis=0)
    tot = blk + 2 * SUBLANES
    acc = None
    for t in range(CONV_W):
        sh = (CONV_W // 2 - t) % tot
        xs = ext if sh == 0 else pltpu.roll(ext, sh, 0)
        term = xs[SUBLANES:SUBLANES + blk, :] * cw_ref[t:t + 1, :]
        acc = term if acc is None else acc + term
    y = _silu(acc)
    nqk = 2 * GDN_H
    for h in range(3 * GDN_H):
        sl = slice(h * LANES, (h + 1) * LANES)
        yh = y[:, sl]
        if h < nqk:
            yh = yh * lax.rsqrt(jnp.sum(yh * yh, axis=-1, keepdims=True) + EPS)
            if h < GDN_H:
                yh = yh * GDN_DK ** -0.5
        qkv_ref[:, sl] = yh
    x = ab_ref[...]
    lane = lax.broadcasted_iota(i32, x.shape, 1)
    xa = x + par_ref[1:2, :]
    softplus = jnp.maximum(xa, 0.0) + jnp.log(1.0 + jnp.exp(-jnp.abs(xa)))
    gb_ref[...] = jnp.where(lane < nqk, par_ref[0:1, :] * softplus, _sigmoid(x))


def _gdn_prep(proj, conv_w, par, blk=256):
    b, l, _ = proj.shape
    blk = min(blk, l)
    nb = l // blk
    w = 3 * BW
    r8 = blk // SUBLANES
    last8 = l // SUBLANES - 1
    return pl.pallas_call(
        _gdn_prep_body, name="gdn_prep",
        out_shape=(jax.ShapeDtypeStruct((b, l, w), f32), jax.ShapeDtypeStruct((b, l, LANES), f32)),
        grid=(b, nb),
        in_specs=[pl.BlockSpec((None, blk, w), lambda bi, i: (bi, i, C_GDN_QKV // w)),
                  pl.BlockSpec((None, SUBLANES, w), lambda bi, i: (bi, jnp.maximum(i * r8 - 1, 0), C_GDN_QKV // w)),
                  pl.BlockSpec((None, SUBLANES, w), lambda bi, i: (bi, jnp.minimum((i + 1) * r8, last8), C_GDN_QKV // w)),
                  pl.BlockSpec((None, blk, LANES), lambda bi, i: (bi, i, C_GDN_AB // LANES)),
                  pl.BlockSpec((SUBLANES, w), lambda bi, i: (0, 0)),
                  pl.BlockSpec((SUBLANES, LANES), lambda bi, i: (0, 0))],
        out_specs=(pl.BlockSpec((None, blk, w), lambda bi, i: (bi, i, 0)),
                   pl.BlockSpec((None, blk, LANES), lambda bi, i: (bi, i, 0))),
        compiler_params=_cparams("parallel", "parallel"),
    )(proj, proj, proj, proj, conv_w, par)


def _split_bf16(a):
    hi = a.astype(bf16)
    return hi, (a - hi.astype(f32)).astype(bf16)


def _mm3(a, b):
    ah, al = _split_bf16(a)
    bh, bl = _split_bf16(b)
    return _mm(ah, bh) + (_mm(ah, bl) + _mm(al, bh))


def _unit_tri_inverse(lw, eye, bd16):
    ld = jnp.where(bd16, lw, 0.0)
    lo = lw - ld
    p = eye - ld
    l2 = _mm3(ld, ld)
    p = p + _mm3(p, l2)
    l4 = _mm3(l2, l2)
    p = p + _mm3(p, l4)
    l8 = _mm3(l4, l4)
    p = p + _mm3(p, l8)
    m = _mm3(p, lo)
    m2 = _mm3(m, m)
    q = eye - m
    q = q + _mm3(q, m2)
    return _mm3(q, p)


def _gdn_body(q_ref, k_ref, v_ref, gb_ref, o_ref, s_ref, *, fwd):
    nch = q_ref.shape[0] // CHUNK
    d = 0 if fwd else 1

    @pl.when(pl.program_id(1) == 0)
    def _():
        s_ref[...] = jnp.zeros_like(s_ref)

    r64 = lax.broadcasted_iota(i32, (CHUNK, CHUNK), 0)
    c64 = lax.broadcasted_iota(i32, (CHUNK, CHUNK), 1)
    incl = (r64 >= c64) if fwd else (r64 <= c64)
    strict = (r64 > c64) if fwd else (r64 < c64)
    tri = _ind(incl)
    eye = _ind(r64 == c64)
    bd16 = (r64 // 16) == (c64 // 16)

    def chunk(ci, carry):
        c = ci if fwd else nch - 1 - ci
        r0 = pl.multiple_of(c * CHUNK, CHUNK)
        gcols = gb_ref[pl.ds(r0, CHUNK), :]
        gam = _mm_hi(tri, gcols)
        gam_t = gam.T
        outs = []
        for h in range(GDN_H):
            lg = d * GDN_H + h
            lb = 2 * GDN_H + lg
            sl = slice(h * LANES, (h + 1) * LANES)
            gcol = gam[:, lg:lg + 1]
            grow = gam_t[lg:lg + 1, :]
            beta = gcols[:, lb:lb + 1]
            gtot = gcol[CHUNK - 1:CHUNK, :] if fwd else gcol[0:1, :]
            dec = jnp.where(incl, jnp.exp(jnp.where(incl, gcol - grow, 0.0)), 0.0)
            qh = q_ref[pl.ds(r0, CHUNK), sl]
            kh = k_ref[pl.ds(r0, CHUNK), sl]
            vh = v_ref[pl.ds(r0, CHUNK), sl]
            kb = kh * beta
            khb = kh.astype(bf16)
            qk = _nt(jnp.concatenate([qh, kb], axis=0).astype(bf16), khb)
            attn = (qk[:CHUNK] * dec).astype(bf16)
            lw = jnp.where(strict, qk[CHUNK:] * dec, 0.0)
            tinv = _unit_tri_inverse(lw, eye, bd16)
            eg = jnp.exp(gcol)
            rhs = jnp.concatenate([vh * beta, kb * eg], axis=1)
            sol = _mm3(tinv, rhs)
            u, w = sol[:, :GDN_DV], sol[:, GDN_DV:]
            s = s_ref[sl, :]
            ws_qs = _mm(jnp.concatenate([w, qh * eg], axis=0).astype(bf16), s.astype(bf16))
            vnew = (u - ws_qs[:CHUNK]).astype(bf16)
            outs.append(ws_qs[CHUNK:] + _mm(attn, vnew))
            kt = (kh * jnp.exp(gtot - gcol)).astype(bf16)
            s_ref[sl, :] = s * jnp.exp(gtot) + _tn(kt, vnew)
        o_ref[pl.ds(r0, CHUNK), :] = jnp.concatenate(outs, axis=1)
        return carry

    lax.fori_loop(0, nch, chunk, 0)


def _gdn(qkv, gb, fwd, blk=512):
    b, l, _ = qkv.shape
    blk = min(blk, l)
    nb = l // blk
    bidx = (lambda i: i) if fwd else (lambda i: nb - 1 - i)
    col = lambda j: pl.BlockSpec((None, blk, BW), lambda bi, i: (bi, bidx(i), j))
    return pl.pallas_call(
        functools.partial(_gdn_body, fwd=fwd), name="gdn_fwd" if fwd else "gdn_bwd",
        out_shape=jax.ShapeDtypeStruct((b, l, BW), f32),
        grid=(b, nb),
        in_specs=[col(0), col(1), col(2), pl.BlockSpec((None, blk, LANES), lambda bi, i: (bi, bidx(i), 0))],
        out_specs=pl.BlockSpec((None, blk, BW), lambda bi, i: (bi, bidx(i), 0)),
        scratch_shapes=[pltpu.VMEM((GDN_H * GDN_DK, GDN_DV), f32)],
        compiler_params=_cparams("parallel", "arbitrary"),
    )(qkv, qkv, qkv, gb)


def _head_norm_gate(o, gain, z):
    outs = []
    for h in range(BW // LANES):
        oh = o[:, h * LANES:(h + 1) * LANES]
        outs.append(oh * lax.rsqrt(jnp.mean(oh * oh, axis=-1, keepdims=True) + EPS) * gain)
    return jnp.concatenate(outs, axis=1) * _silu(z)


def _merge_body(h_ref, gate_ref, ch_ref, u_ref, glaf_ref, glab_ref, r_ref, gdnf_ref, gdnb_ref, z_ref,
                dsk_ref, wglu_ref, gng_ref, dng_ref, wbr_ref, wout_ref, o_ref):
    y0 = _gelu_tanh(ch_ref[...] + dsk_ref[...] * u_ref[...])
    y_s5 = y0 * _sigmoid(_mm(y0.astype(bf16), wglu_ref[...]))
    y_gla = _head_norm_gate(glaf_ref[...] + glab_ref[...], gng_ref[...], r_ref[...])
    y_gdn = _head_norm_gate(gdnf_ref[...] + gdnb_ref[...], dng_ref[...], z_ref[...])
    merged = None
    for r, y in enumerate((y_s5, y_gla, y_gdn)):
        gate = _sigmoid(gate_ref[:, r * D_MODEL:(r + 1) * D_MODEL])
        term = gate * _mm(y.astype(bf16), wbr_ref[r])
        merged = term if merged is None else merged + term
    o_ref[...] = h_ref[...] + _mm(merged.astype(bf16), wout_ref[...])


def _merge(h2d, proj2d, ch, gla_f, gla_b, gdn_f, gdn_b, dsk, wglu, gng, dng, wbr, wout, tm=256):
    m = h2d.shape[0]
    row = lambda width, off=0: pl.BlockSpec((tm, width), lambda i: (i, off // width))
    full = lambda shape: pl.BlockSpec(shape, lambda i: (0,) * len(shape))
    return pl.pallas_call(
        _merge_body, name="merge",
        out_shape=jax.ShapeDtypeStruct((m, D_MODEL), f32),
        grid=(m // tm,),
        in_specs=[row(D_MODEL), row(3 * D_MODEL, C_GATE), row(BW), row(BW, C_S5), row(BW), row(BW),
                  row(BW, C_GLA_R), row(BW), row(BW), row(BW, C_GDN_Z),
                  full((1, BW)), full((BW, BW)), full((1, LANES)), full((1, LANES)),
                  full((3, BW, D_MODEL)), full((D_MODEL, D_MODEL))],
        out_specs=row(D_MODEL),
        compiler_params=_cparams("parallel"),
    )(h2d, proj2d, ch, proj2d, gla_f, gla_b, proj2d, gdn_f, gdn_b, proj2d, dsk, wglu, gng, dng, wbr, wout)


def _router_body(h_ref, g_ref, wr_ref, hn_ref, aff_ref):
    hn = _rms(h_ref[...], g_ref[...])
    hn_ref[...] = hn.astype(bf16)
    logits = lax.dot_general(wr_ref[...], hn, (((1,), (1,)), ((), ())), preferred_element_type=f32,
                             precision=HIGHEST)
    e = jnp.exp(logits - jnp.max(logits, axis=0, keepdims=True))
    aff_ref[...] = e / jnp.sum(e, axis=0, keepdims=True)


def _router(h2d, gain, wr_t, tm=512):
    m = h2d.shape[0]
    return pl.pallas_call(
        _router_body, name="router",
        out_shape=(jax.ShapeDtypeStruct((m, D_MODEL), bf16), jax.ShapeDtypeStruct((N_EXPERTS, m), f32)),
        grid=(m // tm,),
        in_specs=[pl.BlockSpec((tm, D_MODEL), lambda i: (i, 0)),
                  pl.BlockSpec((1, D_MODEL), lambda i: (0, 0)),
                  pl.BlockSpec((N_EXPERTS, D_MODEL), lambda i: (0, 0))],
        out_specs=(pl.BlockSpec((tm, D_MODEL), lambda i: (i, 0)), pl.BlockSpec((N_EXPERTS, tm), lambda i: (0, i))),
        compiler_params=_cparams("parallel"),
    )(h2d, gain, wr_t)


def _threshold_body(aff_ref, thr_ref, *, cap):
    keys = pltpu.bitcast(aff_ref[...], i32)

    def count(mask):
        return jnp.sum(jnp.where(mask, 1.0, 0.0), axis=1, keepdims=True).astype(i32)

    def bit(bi, t):
        cand = t | (1 << (30 - bi))
        return jnp.where(count(keys >= cand) >= cap, cand, t)

    t = lax.fori_loop(0, 31, bit, jnp.zeros((N_EXPERTS, 1), i32))
    budget = cap - count(keys > t)
    lane = lax.broadcasted_iota(i32, (N_EXPERTS, LANES), 1)
    thr_ref[...] = jnp.where(lane == 0, t, jnp.where(lane == 1, budget, 0))


def _threshold(aff_t, cap):
    n = aff_t.shape[1]
    return pl.pallas_call(
        functools.partial(_threshold_body, cap=cap), name="topc_threshold",
        out_shape=jax.ShapeDtypeStruct((N_EXPERTS, LANES), i32),
        in_specs=[pl.BlockSpec((N_EXPERTS, n), lambda: (0, 0))],
        out_specs=pl.BlockSpec((N_EXPERTS, LANES), lambda: (0, 0)),
        compiler_params=pltpu.CompilerParams(vmem_limit_bytes=VMEM_LIMIT_BYTES),
    )(aff_t)


def _slots_body(aff_ref, thr_ref, slot_ref, wts_ref, cnt_ref, run_ref):
    @pl.when(pl.program_id(0) == 0)
    def _():
        run_ref[...] = jnp.zeros_like(run_ref)

    tt = aff_ref.shape[1]
    aff = aff_ref[...]
    keys = pltpu.bitcast(aff, i32)
    t = thr_ref[:, 0:1]
    budget = thr_ref[:, 1:2]
    upper = _ind(lax.broadcasted_iota(i32, (tt, tt), 0) <= lax.broadcasted_iota(i32, (tt, tt), 1), bf16)
    eq = keys == t
    sel_run = run_ref[:, 0:1]
    tie_run = run_ref[:, 1:2]
    cs_eq = _mm(_ind(eq, bf16), upper).astype(i32)
    tie_rank = tie_run + cs_eq - 1
    sel = (keys > t) | (eq & (tie_rank < budget))
    cs_sel = _mm(_ind(sel, bf16), upper).astype(i32)
    slot_ref[...] = jnp.where(sel, sel_run + cs_sel - 1, -1)
    wts_ref[...] = jnp.where(sel, aff, 0.0)
    n_sel = cs_sel[:, tt - 1:tt]
    n_eq = cs_eq[:, tt - 1:tt]
    cnt_ref[...] = jnp.broadcast_to(n_sel, cnt_ref.shape)
    lane = lax.broadcasted_iota(i32, run_ref.shape, 1)
    run_ref[...] = run_ref[...] + jnp.where(lane == 0, n_sel, jnp.where(lane == 1, n_eq, 0))


def _slots(aff_t, thr):
    n = aff_t.shape[1]
    nt = n // MOE_TT
    return pl.pallas_call(
        _slots_body, name="topc_slots",
        out_shape=(jax.ShapeDtypeStruct((N_EXPERTS, n), i32), jax.ShapeDtypeStruct((N_EXPERTS, n), f32),
                   jax.ShapeDtypeStruct((nt, N_EXPERTS, LANES), i32)),
        grid=(nt,),
        in_specs=[pl.BlockSpec((N_EXPERTS, MOE_TT), lambda i: (0, i)),
                  pl.BlockSpec((N_EXPERTS, LANES), lambda i: (0, 0))],
        out_specs=(pl.BlockSpec((N_EXPERTS, MOE_TT), lambda i: (0, i)),
                   pl.BlockSpec((N_EXPERTS, MOE_TT), lambda i: (0, i)),
                   pl.BlockSpec((None, N_EXPERTS, LANES), lambda i: (i, 0, 0))),
        scratch_shapes=[pltpu.VMEM((N_EXPERTS, LANES), i32)],
        compiler_params=_cparams("arbitrary"),
    )(aff_t, thr)


def _pair_tables(cnt, cap):
    e, nt = cnt.shape
    nj = cap // MOE_SB
    npair = nt + nj - 1
    off = jnp.concatenate([jnp.zeros((e, 1), i32), jnp.cumsum(cnt, axis=1, dtype=i32)], axis=1)
    j_lo = off[:, :-1] // MOE_SB
    j_hi = (off[:, 1:] - 1) // MOE_SB
    per_tile = jnp.where(cnt > 0, j_hi - j_lo + 1, 0)
    ps = jnp.concatenate([jnp.zeros((e, 1), i32), jnp.cumsum(per_tile, axis=1, dtype=i32)], axis=1)
    total = ps[:, -1:]
    p = jnp.arange(npair, dtype=i32)[None, :]
    pc = jnp.minimum(p, total - 1)
    tile = jax.vmap(lambda a, v: jnp.searchsorted(a, v, side='right'))(ps, pc).astype(i32) - 1
    tile = jnp.clip(tile, 0, nt - 1)
    blk = jnp.take_along_axis(j_lo, tile, axis=1) + (pc - jnp.take_along_axis(ps, tile, axis=1))
    valid = p < total
    prev_blk = jnp.concatenate([jnp.full((e, 1), -1, i32), blk[:, :-1]], axis=1)
    next_blk = jnp.concatenate([blk[:, 1:], jnp.full((e, 1), -1, i32)], axis=1)
    first = valid & (blk != prev_blk)
    last = valid & ((blk != next_blk) | (p == total - 1))
    flags = valid.astype(i32) + 2 * first.astype(i32) + 4 * last.astype(i32)
    ee = jnp.broadcast_to(jnp.arange(e, dtype=i32)[:, None], (e, npair))
    big = nt * (e + 1)
    key_pairs = jnp.where(valid, tile * (e + 1) + 1 + ee, big).reshape(-1)
    key_init = jnp.arange(nt, dtype=i32) * (e + 1)
    keys = jnp.concatenate([key_init, key_pairs])
    c_tile = jnp.concatenate([jnp.arange(nt, dtype=i32), tile.reshape(-1)])
    c_exp = jnp.concatenate([jnp.zeros((nt,), i32), ee.reshape(-1)])
    c_blk = jnp.concatenate([jnp.zeros((nt,), i32), blk.reshape(-1)])
    c_valid = jnp.concatenate([jnp.zeros((nt,), i32), valid.reshape(-1).astype(i32)])
    order = jnp.argsort(keys, stable=True)
    keys_s = keys[order]
    live = keys_s < big
    c_tile = jnp.where(live, c_tile[order], nt - 1)
    c_exp = jnp.where(live, c_exp[order], 0)
    c_blk = jnp.where(live, c_blk[order], 0)
    c_valid = jnp.where(live, c_valid[order], 0)
    is_init = live & (keys_s % (e + 1) == 0)
    nxt_tile = jnp.concatenate([c_tile[1:], jnp.full((1,), -1, i32)])
    nxt_live = jnp.concatenate([live[1:], jnp.zeros((1,), bool)])
    c_last = live & ((nxt_tile != c_tile) | ~nxt_live)
    c_flags = c_valid + 2 * is_init.astype(i32) + 4 * c_last.astype(i32)
    return (tile, blk, flags), (c_tile, c_exp, c_blk, c_flags)


def _experts_body(tile_ref, blk_ref, flag_ref, x_ref, slot_ref, wts_ref, wg_ref, wu_ref, wd_ref, y_ref,
                  xe_ref, ge_ref):
    e, p = pl.program_id(0), pl.program_id(1)
    fl = flag_ref[e, p]
    sb, tt = xe_ref.shape[0], x_ref.shape[0]

    @pl.when((fl & 2) != 0)
    def _():
        xe_ref[...] = jnp.zeros_like(xe_ref)
        ge_ref[...] = jnp.zeros_like(ge_ref)

    @pl.when((fl & 1) != 0)
    def _():
        rel = slot_ref[...] - blk_ref[e, p] * sb
        hit = lax.broadcasted_iota(i32, (sb, tt), 0) == rel
        xe_ref[...] += _mm(_ind(hit, bf16), x_ref[...])
        gsel = jnp.where(hit, wts_ref[...], 0.0)
        acc = gsel[:, 0:LANES]
        for c in range(1, tt // LANES):
            acc = acc + gsel[:, c * LANES:(c + 1) * LANES]
        ge_ref[...] += acc

    @pl.when((fl & 4) != 0)
    def _():
        x = xe_ref[...].astype(bf16)
        hid = (_silu(_mm(x, wg_ref[...])) * _mm(x, wu_ref[...])).astype(bf16)
        gate = jnp.sum(ge_ref[...], axis=1, keepdims=True)
        y_ref[...] = (_mm(hid, wd_ref[...]) * gate).astype(y_ref.dtype)


def _experts(hn, slot3, wts3, tabs, wg, wu, wd, cap):
    tile, blk, flags = tabs
    npair = tile.shape[1]
    grid_spec = pltpu.PrefetchScalarGridSpec(
        num_scalar_prefetch=3,
        grid=(N_EXPERTS, npair),
        in_specs=[pl.BlockSpec((MOE_TT, D_MODEL), lambda e, p, t, b, f: (t[e, p], 0)),
                  pl.BlockSpec((None, 1, MOE_TT), lambda e, p, t, b, f: (e, 0, t[e, p])),
                  pl.BlockSpec((None, 1, MOE_TT), lambda e, p, t, b, f: (e, 0, t[e, p])),
                  pl.BlockSpec((None, D_MODEL, EXPERT_FF), lambda e, p, t, b, f: (e, 0, 0)),
                  pl.BlockSpec((None, D_MODEL, EXPERT_FF), lambda e, p, t, b, f: (e, 0, 0)),
                  pl.BlockSpec((None, EXPERT_FF, D_MODEL), lambda e, p, t, b, f: (e, 0, 0))],
        out_specs=pl.BlockSpec((None, MOE_SB, D_MODEL), lambda e, p, t, b, f: (e, b[e, p], 0)),
        scratch_shapes=[pltpu.VMEM((MOE_SB, D_MODEL), f32), pltpu.VMEM((MOE_SB, LANES), f32)])
    return pl.pallas_call(
        _experts_body, name="experts",
        out_shape=jax.ShapeDtypeStruct((N_EXPERTS, cap, D_MODEL), bf16),
        grid_spec=grid_spec,
        compiler_params=_cparams("arbitrary", "arbitrary"),
    )(tile, blk, flags, hn, slot3, wts3, wg, wu, wd)


def _combine_body(tile_ref, exp_ref, blk_ref, flag_ref, h_ref, slot_ref, y_ref, p_ref, g_ref, wpg_ref, wpp_ref,
                  o_ref, acc_ref):
    s = pl.program_id(0)
    fl = flag_ref[s]
    tt, sb = acc_ref.shape[0], y_ref.shape[0]

    @pl.when((fl & 2) != 0)
    def _():
        acc_ref[...] = h_ref[...]

    @pl.when((fl & 1) != 0)
    def _():
        rel = slot_ref[...] - blk_ref[s] * sb
        hit = lax.broadcasted_iota(i32, (sb, tt), 0) == rel
        acc_ref[...] += _tn(_ind(hit, bf16), y_ref[...])

    @pl.when((fl & 4) != 0)
    def _():
        h2 = acc_ref[...]
        gate = _sigmoid(_mm(_rms(h2, g_ref[...]).astype(bf16), wpg_ref[...]))
        o_ref[...] = h2 + gate * _mm(p_ref[...].astype(bf16), wpp_ref[...])


def _combine(h2d, slot3, ye, p2d, tabs, g_ple, wpg, wpp):
    c_tile, c_exp, c_blk, c_flags = tabs
    m = h2d.shape[0]
    full = lambda shape: pl.BlockSpec(shape, lambda s, t, e, b, f: (0,) * len(shape))
    grid_spec = pltpu.PrefetchScalarGridSpec(
        num_scalar_prefetch=4,
        grid=(c_tile.shape[0],),
        in_specs=[pl.BlockSpec((MOE_TT, D_MODEL), lambda s, t, e, b, f: (t[s], 0)),
                  pl.BlockSpec((None, 1, MOE_TT), lambda s, t, e, b, f: (e[s], 0, t[s])),
                  pl.BlockSpec((None, MOE_SB, D_MODEL), lambda s, t, e, b, f: (e[s], b[s], 0)),
                  pl.BlockSpec((MOE_TT, PLE_DIM), lambda s, t, e, b, f: (t[s], 0)),
                  full((1, D_MODEL)), full((D_MODEL, D_MODEL)), full((PLE_DIM, D_MODEL))],
        out_specs=pl.BlockSpec((MOE_TT, D_MODEL), lambda s, t, e, b, f: (t[s], 0)),
        scratch_shapes=[pltpu.VMEM((MOE_TT, D_MODEL), f32)])
    return pl.pallas_call(
        _combine_body, name="combine_ple",
        out_shape=jax.ShapeDtypeStruct((m, D_MODEL), f32),
        grid_spec=grid_spec,
        compiler_params=_cparams("arbitrary"),
    )(c_tile, c_exp, c_blk, c_flags, h2d, slot3, ye, p2d, g_ple, wpg, wpp)


def _moe_ple(h2d, p2d, g_ffn, wr_t, wg, wu, wd, g_ple, wpg, wpp):
    n = h2d.shape[0]
    cap = max(1, EC_FACTOR * n // N_EXPERTS)
    hn, aff_t = _router(h2d, g_ffn, wr_t)
    thr = _threshold(aff_t, cap)
    slot, wts, cnt = _slots(aff_t, thr)
    etabs, ctabs = _pair_tables(jnp.transpose(cnt[:, :, 0]), cap)
    slot3 = slot.reshape(N_EXPERTS, 1, n)
    ye = _experts(hn, slot3, wts.reshape(N_EXPERTS, 1, n), etabs, wg, wu, wd, cap)
    return _combine(h2d, slot3, ye, p2d, ctabs, g_ple, wpg, wpp)


def _final_norm_body(x_ref, g_ref, o_ref):
    o_ref[...] = _rms(x_ref[...], g_ref[...])


def _final_norm(h2d, gain, tm=1024):
    m = h2d.shape[0]
    return pl.pallas_call(
        _final_norm_body, name="final_norm",
        out_shape=jax.ShapeDtypeStruct((m, D_MODEL), f32),
        grid=(m // tm,),
        in_specs=[pl.BlockSpec((tm, D_MODEL), lambda i: (i, 0)), pl.BlockSpec((1, D_MODEL), lambda i: (0, 0))],
        out_specs=pl.BlockSpec((tm, D_MODEL), lambda i: (i, 0)),
        compiler_params=_cparams("parallel"),
    )(h2d, gain)


def _layer_weights(w):
    lw = {}
    lw['g_mix'] = w['norm_mix'].reshape(1, D_MODEL)
    lw['w_in'] = _reorder_w_in(w['w_in'])
    lw['s5'] = _s5_weights(w['s5_B_re'], w['s5_B_im'], w['s5_C_re'], w['s5_C_im'],
                           w['s5_lam_re'], w['s5_lam_im'], w['s5_log_dt'])
    wgate = w['gla_w_gate']
    lw['gla_wg'] = tuple(jnp.zeros((LANES, GLA_H * GLA_DK), f32).at[d * GLA_RANK:(d + 1) * GLA_RANK].set(wgate[d])
                         for d in range(2))
    lw['gla_bg'] = tuple(w['gla_b_gate'][d].reshape(1, -1) for d in range(2))
    lw['conv_w'] = jnp.pad(jnp.transpose(w['gdn_conv']), ((0, SUBLANES - CONV_W), (0, 0)))
    neg_a = -jnp.exp(w['gdn_A_log']).reshape(-1)
    par = jnp.zeros((SUBLANES, LANES), f32).at[0, :2 * GDN_H].set(neg_a).at[1, :2 * GDN_H].set(
        w['gdn_dt_bias'].reshape(-1))
    lw['gdn_par'] = par
    lw['dsk'] = w['s5_D'].reshape(1, BW)
    lw['wglu'] = w['s5_w_glu'].astype(bf16)
    lw['gng'] = w['gla_norm'].reshape(1, LANES)
    lw['dng'] = w['gdn_norm'].reshape(1, LANES)
    lw['wbr'] = w['w_branch'].astype(bf16)
    lw['wout'] = w['w_out'].astype(bf16)
    lw['g_ffn'] = w['norm_ffn'].reshape(1, D_MODEL)
    lw['wr_t'] = jnp.transpose(w['w_router'])
    lw['wg'] = w['w_exp_gate'].astype(bf16)
    lw['wu'] = w['w_exp_up'].astype(bf16)
    lw['wd'] = w['w_exp_down'].astype(bf16)
    lw['g_ple'] = w['norm_ple'].reshape(1, D_MODEL)
    lw['wpg'] = w['w_ple_gate'].astype(bf16)
    lw['wpp'] = w['w_ple_proj'].astype(bf16)
    return lw


def _mixers(h, lw):
    b, l, _ = h.shape
    h2d = h.reshape(b * l, D_MODEL)
    proj2d = _inproj(h2d, lw['g_mix'], lw['w_in'])
    proj = proj2d.reshape(b, l, D_INP)
    ch = _s5_mixer(proj, lw['s5'])
    gla_f = _gla(proj, lw['gla_wg'][0], lw['gla_bg'][0], True)
    gla_b = _gla(proj, lw['gla_wg'][1], lw['gla_bg'][1], False)
    qkv, gb = _gdn_prep(proj, lw['conv_w'], lw['gdn_par'])
    gdn_f = _gdn(qkv, gb, True)
    gdn_b = _gdn(qkv, gb, False)
    flat = lambda a: a.reshape(b * l, a.shape[-1])
    return _merge(h2d, proj2d, flat(ch), flat(gla_f), flat(gla_b), flat(gdn_f), flat(gdn_b),
                  lw['dsk'], lw['wglu'], lw['gng'], lw['dng'], lw['wbr'], lw['wout'])


def _layer(h, p_i, lw):
    b, l, _ = h.shape
    h1 = _mixers(h, lw)
    h3 = _moe_ple(h1, p_i.reshape(b * l, PLE_DIM), lw['g_ffn'], lw['wr_t'], lw['wg'], lw['wu'], lw['wd'],
                  lw['g_ple'], lw['wpg'], lw['wpp'])
    return h3.reshape(b, l, D_MODEL)


def kernel(x_prompt, x_sample, p_prompt, p_sample, norm_mix, w_in, s5_B_re, s5_B_im, s5_C_re, s5_C_im, s5_D, s5_lam_re, s5_lam_im, s5_log_dt, s5_w_glu, gla_w_gate, gla_b_gate, gla_norm, gdn_conv, gdn_A_log, gdn_dt_bias, gdn_norm, w_branch, w_out, norm_ffn, w_router, w_exp_gate, w_exp_up, w_exp_down, norm_ple, w_ple_gate, w_ple_proj, norm_final):
    weights = dict(norm_mix=norm_mix, w_in=w_in, s5_B_re=s5_B_re, s5_B_im=s5_B_im, s5_C_re=s5_C_re, s5_C_im=s5_C_im,
                   s5_D=s5_D, s5_lam_re=s5_lam_re, s5_lam_im=s5_lam_im, s5_log_dt=s5_log_dt, s5_w_glu=s5_w_glu,
                   gla_w_gate=gla_w_gate, gla_b_gate=gla_b_gate, gla_norm=gla_norm, gdn_conv=gdn_conv,
                   gdn_A_log=gdn_A_log, gdn_dt_bias=gdn_dt_bias, gdn_norm=gdn_norm, w_branch=w_branch, w_out=w_out,
                   norm_ffn=norm_ffn, w_router=w_router, w_exp_gate=w_exp_gate, w_exp_up=w_exp_up,
                   w_exp_down=w_exp_down, norm_ple=norm_ple, w_ple_gate=w_ple_gate, w_ple_proj=w_ple_proj)

    def body(carry, xs):
        hp, hs = carry
        w_i, pp, ps = xs
        lw = _layer_weights(w_i)
        return (_layer(hp, pp, lw), _layer(hs, ps, lw)), None

    (hp, hs), _ = lax.scan(body, (x_prompt.astype(f32), x_sample.astype(f32)), (weights, p_prompt, p_sample))
    g_fin = norm_final.reshape(1, D_MODEL)
    yp = _final_norm(hp.reshape(-1, D_MODEL), g_fin).reshape(x_prompt.shape).astype(x_prompt.dtype)
    ys = _final_norm(hs.reshape(-1, D_MODEL), g_fin).reshape(x_sample.shape).astype(x_sample.dtype)
    return (yp, ys)
```

```python
import functools
import math

import jax
import jax.numpy as jnp
import numpy as np
from jax import lax
from jax.experimental import pallas as pl
from jax.experimental.pallas import tpu as pltpu

f32 = jnp.float32
bf16 = jnp.bfloat16
i32 = jnp.int32
HIGHEST = lax.Precision.HIGHEST

D_MODEL = 1024
DEPTH = 4
PLE_DIM = 256
BW = 512
EPS = 1e-6
CHUNK = 64
S5_GROUPS, S5_GC, S5_STATE = 32, 16, 64
GLA_H, GLA_DK, GLA_DV, GLA_RANK, GLA_TAU = 4, 64, 128, 16, 16.0
GDN_H, GDN_DK, GDN_DV, CONV_W = 4, 128, 128, 5
N_EXPERTS, EXPERT_FF, EC_FACTOR = 16, 2048, 2

LANES = 128
SUBLANES = 8
VMEM_LIMIT_BYTES = 56 * 1024 * 1024

C_GATE, C_S5, C_GLA_V, C_GLA_R = 0, 3072, 3584, 4096
C_GDN_QKV, C_GDN_Z, C_GLA_Q, C_GLA_K, C_GLA_LR, C_GDN_AB = 4608, 6144, 6656, 6912, 7168, 7296
D_INP = 7424

S5_T = 8
S5_LB = BW // LANES
S5_SW = 8 * S5_STATE * 2

GDN_GROUP = 2
GLA_GROUP = 4

MOE_TT = 512
MOE_SB = 256


def _cparams(*sem):
    return pltpu.CompilerParams(dimension_semantics=sem, vmem_limit_bytes=VMEM_LIMIT_BYTES)


def _nt(a, b):
    return lax.dot_general(a, b, (((1,), (1,)), ((), ())), preferred_element_type=f32)


def _tn(a, b):
    return lax.dot_general(a, b, (((0,), (0,)), ((), ())), preferred_element_type=f32)


def _mm(a, b):
    return jnp.dot(a, b, preferred_element_type=f32)


def _mm_hi(a, b):
    return jnp.dot(a, b, preferred_element_type=f32, precision=HIGHEST)


def _ind(mask, dtype=f32):
    return jnp.where(mask, 1.0, 0.0).astype(dtype)


def _sigmoid(x):
    return 1.0 / (1.0 + jnp.exp(-x))


def _silu(x):
    return x * _sigmoid(x)


def _gelu_tanh(x):
    return 0.5 * x * (1.0 + jnp.tanh(math.sqrt(2.0 / math.pi) * (x + 0.044715 * (x * x * x))))


def _rms(x, g):
    return x * lax.rsqrt(jnp.mean(x * x, axis=-1, keepdims=True) + EPS) * g


def _inproj_body(x_ref, g_ref, w_ref, o_ref):
    xn = _rms(x_ref[...], g_ref[...]).astype(bf16)
    o_ref[...] = _mm(xn, w_ref[...])


def _inproj(x2d, gain, w_p, tm=256):
    m = x2d.shape[0]
    return pl.pallas_call(
        _inproj_body, name="inproj",
        out_shape=jax.ShapeDtypeStruct((m, D_INP), f32),
        grid=(m // tm,),
        in_specs=[pl.BlockSpec((tm, D_MODEL), lambda i: (i, 0)),
                  pl.BlockSpec((1, D_MODEL), lambda i: (0, 0)),
                  pl.BlockSpec((D_MODEL, D_INP), lambda i: (0, 0), pipeline_mode=pl.Buffered(1))],
        out_specs=pl.BlockSpec((tm, D_INP), lambda i: (i, 0)),
        compiler_params=_cparams("parallel"),
    )(x2d, gain, w_p)


def _reorder_w_in(w_in):
    o = np.cumsum((0, 512, 256, 256, 512, 512, 32, 512, 512, 512, 512, 8, 8, 3072))
    seg = lambda k: w_in[:, o[k]:o[k + 1]]
    zpad = lambda a: jnp.pad(a, ((0, 0), (0, LANES - a.shape[1])))
    parts = [seg(12), seg(0), seg(3), seg(4), seg(6), seg(7), seg(8), seg(9), seg(1), seg(2),
             zpad(seg(5)), zpad(jnp.concatenate([seg(10), seg(11)], axis=1))]
    return jnp.concatenate(parts, axis=1).astype(bf16)


def _s5_weights(b_re, b_im, c_re, c_im, lam_re, lam_im, log_dt):
    T = S5_T
    dt = jnp.exp(log_dt)[:, :, None]
    lr, li = lam_re, lam_im
    mag = jnp.exp(lr * dt)
    ab_re, ab_im = mag * jnp.cos(li * dt), mag * jnp.sin(li * dt)
    den = lr * lr + li * li
    num_re = ab_re - 1.0
    coef_re = (num_re * lr + ab_im * li) / den
    coef_im = (ab_im * lr - num_re * li) / den
    xb_re = coef_re[..., None] * b_re[None] - coef_im[..., None] * b_im[None]
    xb_im = coef_re[..., None] * b_im[None] + coef_im[..., None] * b_re[None]

    def powers(taus):
        tau = jnp.asarray(taus, lr.dtype)
        pm = jnp.exp((lr * dt)[..., None] * tau)
        ang = (li * dt)[..., None] * tau
        return pm * jnp.cos(ang), pm * jnp.sin(ang)

    p_re, p_im = powers(np.arange(T + 1))
    cp_re = c_re[None, :, :, :, None] * p_re[:, :, None] - c_im[None, :, :, :, None] * p_im[:, :, None]
    cp_im = c_re[None, :, :, :, None] * p_im[:, :, None] + c_im[None, :, :, :, None] * p_re[:, :, None]
    kern = (jnp.einsum('dgknt,dgnc->dgtkc', cp_re, xb_re, precision=HIGHEST)
            - jnp.einsum('dgknt,dgnc->dgtkc', cp_im, xb_im, precision=HIGHEST))
    s_idx = np.arange(T)[:, None]
    t_idx = np.arange(T)[None, :]
    lag_f = np.clip(t_idx - s_idx, 0, T)
    lag_b = np.clip(s_idx - t_idx, 0, T)
    m_f = jnp.asarray((t_idx >= s_idx), kern.dtype)[None, :, :, None, None]
    m_b = jnp.asarray((s_idx >= t_idx), kern.dtype)[None, :, :, None, None]
    a_g = kern[0][:, lag_f] * m_f + kern[1][:, lag_b] * m_b
    eye8 = jnp.eye(8, dtype=kern.dtype)
    a_g = a_g.reshape(S5_LB, 8, T, T, S5_GC, S5_GC)
    a_blk = jnp.einsum('jgstkc,gh->jsgcthk', a_g, eye8).reshape(S5_LB, T * LANES, T * LANES)

    def state_in(d, taus):
        e_re = p_re[d][:, :, taus][..., None] * xb_re[d][:, :, None, :] - p_im[d][:, :, taus][..., None] * xb_im[d][:, :, None, :]
        e_im = p_re[d][:, :, taus][..., None] * xb_im[d][:, :, None, :] + p_im[d][:, :, taus][..., None] * xb_re[d][:, :, None, :]
        e = jnp.stack([e_re, e_im], axis=0)
        e = e.reshape(2, S5_LB, 8, S5_STATE, T, S5_GC)
        m = jnp.einsum('pjgnsc,gh->jsgchpn', e, eye8)
        m = m.reshape(S5_LB, T, 8, S5_GC, 4, 2, 2, S5_STATE)
        m = jnp.transpose(m, (0, 1, 2, 3, 4, 6, 5, 7))
        return m.reshape(S5_LB, T * LANES, S5_SW)

    m_f_w = state_in(0, np.arange(T - 1, -1, -1))
    m_b_w = state_in(1, np.arange(T))

    def state_out(d, taus):
        r = cp_re[d][..., taus]
        im = -cp_im[d][..., taus]
        w = jnp.stack([r, im], axis=0).reshape(2, S5_LB, 8, S5_GC, S5_STATE, T)
        w = jnp.einsum('pjhknt,gh->jhpntgk', w, eye8)
        w = w.reshape(S5_LB, 4, 2, 2, S5_STATE, T, 8, S5_GC)
        w = jnp.transpose(w, (0, 1, 3, 2, 4, 5, 6, 7))
        return w.reshape(S5_LB, S5_SW, T * LANES)

    n_f_w = state_out(0, np.arange(1, T + 1))
    n_b_w = state_out(1, np.arange(T, 0, -1))

    q_re, q_im = powers(T * np.arange(8))

    def table(arr_re, arr_im, d, order):
        t = jnp.stack([arr_re[d][..., order], arr_im[d][..., order]], axis=0)
        t = t.reshape(2, 16, 2, S5_STATE, len(order))
        return jnp.transpose(t, (4, 1, 0, 2, 3)).reshape(len(order), 16 * 2 * LANES)

    asc = np.arange(8)
    dbl = np.array([1, 2, 4, 0, 0, 0, 0, 0])
    tabs = (table(q_re, q_im, 0, asc), table(q_re, q_im, 0, dbl),
            table(q_re, q_im, 1, asc[::-1]), table(q_re, q_im, 1, dbl))
    return (a_blk.astype(bf16), m_f_w.astype(bf16), m_b_w.astype(bf16), n_f_w.astype(bf16), n_b_w.astype(bf16),
            tuple(t.astype(f32) for t in tabs))


def _s5_load_chunks(u_ref):
    rows = u_ref.shape[0] // S5_T
    parts = [u_ref[pl.ds(s, rows, stride=S5_T), :] for s in range(S5_T)]
    return jnp.concatenate(parts, axis=1).astype(bf16)


def _s5_in_body(u_ref, mf_ref, mb_ref, xf_ref, xb_ref):
    lhs = _s5_load_chunks(u_ref)
    xf_ref[...] = _mm(lhs, mf_ref[...])
    xb_ref[...] = _mm(lhs, mb_ref[...])


def _s5_in(proj, m_f_w, m_b_w, tl):
    b, l, _ = proj.shape
    nc = l // S5_T
    out = jax.ShapeDtypeStruct((b, nc, S5_LB * S5_SW), f32)
    wspec = pl.BlockSpec((None, S5_T * LANES, S5_SW), lambda bi, j, t: (j, 0, 0))
    ospec = pl.BlockSpec((None, tl // S5_T, S5_SW), lambda bi, j, t: (bi, t, j))
    return pl.pallas_call(
        _s5_in_body, name="s5_in",
        out_shape=(out, out),
        grid=(b, S5_LB, l // tl),
        in_specs=[pl.BlockSpec((None, tl, LANES), lambda bi, j, t: (bi, t, C_S5 // LANES + j)), wspec, wspec],
        out_specs=(ospec, ospec),
        compiler_params=_cparams("parallel", "parallel", "parallel"),
    )(proj, m_f_w, m_b_w)


def _s5_scan_body(xf_ref, xb_ref, cf_ref, hf_ref, cb_ref, hb_ref, sf_ref, sb_ref):
    n8 = xf_ref.shape[0] // SUBLANES
    rows = lax.broadcasted_iota(i32, (SUBLANES, LANES), 0)
    re, im = slice(0, LANES), slice(LANES, 2 * LANES)

    def shifted(x, d, fwd):
        if fwd:
            return jnp.where(rows >= d, pltpu.roll(x, d, 0), 0.0)
        return jnp.where(rows < SUBLANES - d, pltpu.roll(x, SUBLANES - d, 0), 0.0)

    def local_scan(xr, xi, h_ref, fwd):
        er, ei = shifted(xr, 1, fwd), shifted(xi, 1, fwd)
        for k, d in enumerate((1, 2, 4)):
            ar, ai = h_ref[k:k + 1, re], h_ref[k:k + 1, im]
            sr, si = shifted(er, d, fwd), shifted(ei, d, fwd)
            er, ei = er + ar * sr - ai * si, ei + ar * si + ai * sr
        return er, ei

    def tile(x_ref, c_ref, h_ref, o_ref, i, sr, si, fwd):
        r0 = pl.multiple_of(i * SUBLANES, SUBLANES)
        xr, xi = x_ref[pl.ds(r0, SUBLANES), re], x_ref[pl.ds(r0, SUBLANES), im]
        er, ei = local_scan(xr, xi, h_ref, fwd)
        cr, ci = c_ref[:, re], c_ref[:, im]
        outr = er + cr * sr - ci * si
        outi = ei + cr * si + ci * sr
        o_ref[pl.ds(r0, SUBLANES), re] = outr
        o_ref[pl.ds(r0, SUBLANES), im] = outi
        e = SUBLANES - 1 if fwd else 0
        ar, ai = h_ref[0:1, re], h_ref[0:1, im]
        nr = ar * outr[e:e + 1] - ai * outi[e:e + 1] + xr[e:e + 1]
        ni = ar * outi[e:e + 1] + ai * outr[e:e + 1] + xi[e:e + 1]
        return nr, ni

    def step(i, carry):
        fr, fi, br, bi = carry
        fr, fi = tile(xf_ref, cf_ref, hf_ref, sf_ref, i, fr, fi, True)
        br, bi = tile(xb_ref, cb_ref, hb_ref, sb_ref, n8 - 1 - i, br, bi, False)
        return fr, fi, br, bi

    z = jnp.zeros((1, LANES), f32)
    lax.fori_loop(0, n8, step, (z, z, z, z))


def _s5_scan(xf, xb, tabs):
    b, nc, w = xf.shape
    ncb = w // (2 * LANES)
    xspec = pl.BlockSpec((None, nc, 2 * LANES), lambda bi, c: (bi, 0, c))
    tspec = pl.BlockSpec((SUBLANES, 2 * LANES), lambda bi, c: (0, c))
    out = jax.ShapeDtypeStruct((b, nc, w), f32)
    return pl.pallas_call(
        _s5_scan_body, name="s5_scan",
        out_shape=(out, out),
        grid=(b, ncb),
        in_specs=[xspec, xspec, tspec, tspec, tspec, tspec],
        out_specs=(xspec, xspec),
        compiler_params=_cparams("parallel", "parallel"),
    )(xf, xb, *tabs)


def _s5_out_body(u_ref, sf_ref, sb_ref, a_ref, nf_ref, nb_ref, y_ref):
    lhs = _s5_load_chunks(u_ref)
    y = (_mm(lhs, a_ref[...]) + _mm(sf_ref[...].astype(bf16), nf_ref[...])
         + _mm(sb_ref[...].astype(bf16), nb_ref[...]))
    rows = y.shape[0]
    for t in range(S5_T):
        y_ref[pl.ds(t, rows, stride=S5_T), :] = y[:, t * LANES:(t + 1) * LANES]


def _s5_out(proj, sf, sb, a_blk, n_f_w, n_b_w, tl):
    b, l, _ = proj.shape
    sspec = pl.BlockSpec((None, tl // S5_T, S5_SW), lambda bi, j, t: (bi, t, j))
    return pl.pallas_call(
        _s5_out_body, name="s5_out",
        out_shape=jax.ShapeDtypeStruct((b, l, BW), f32),
        grid=(b, S5_LB, l // tl),
        in_specs=[pl.BlockSpec((None, tl, LANES), lambda bi, j, t: (bi, t, C_S5 // LANES + j)), sspec, sspec,
                  pl.BlockSpec((None, S5_T * LANES, S5_T * LANES), lambda bi, j, t: (j, 0, 0)),
                  pl.BlockSpec((None, S5_SW, S5_T * LANES), lambda bi, j, t: (j, 0, 0)),
                  pl.BlockSpec((None, S5_SW, S5_T * LANES), lambda bi, j, t: (j, 0, 0))],
        out_specs=pl.BlockSpec((None, tl, LANES), lambda bi, j, t: (bi, t, j)),
        compiler_params=_cparams("parallel", "parallel", "parallel"),
    )(proj, sf, sb, a_blk, n_f_w, n_b_w)


def _s5_mixer(proj, s5w, tl=2048):
    a_blk, m_f_w, m_b_w, n_f_w, n_b_w, tabs = s5w
    tl = min(tl, proj.shape[1])
    xf, xb = _s5_in(proj, m_f_w, m_b_w, tl)
    sf, sb = _s5_scan(xf, xb, tabs)
    return _s5_out(proj, sf, sb, a_blk, n_f_w, n_b_w, tl)


def _gla_body(qf_ref, kf_ref, vf_ref, lrf_ref, qb_ref, kb_ref, vb_ref, lrb_ref, wgf_ref, bgf_ref, wgb_ref, bgb_ref,
              of_ref, ob_ref, st_ref):
    ins = ((qf_ref, kf_ref, vf_ref, lrf_ref, wgf_ref, bgf_ref, of_ref),
           (qb_ref, kb_ref, vb_ref, lrb_ref, wgb_ref, bgb_ref, ob_ref))
    nch = qf_ref.shape[0] // CHUNK
    hk = GLA_H * GLA_DK
    hv = GLA_H * GLA_DV

    @pl.when(pl.program_id(1) == 0)
    def _():
        st_ref[...] = jnp.zeros_like(st_ref)

    r64 = lax.broadcasted_iota(i32, (CHUNK, CHUNK), 0)
    c64 = lax.broadcasted_iota(i32, (CHUNK, CHUNK), 1)
    tris = (_ind(r64 >= c64), _ind(r64 <= c64))
    rr = lax.broadcasted_iota(i32, (CHUNK, hk), 0)
    cc = lax.broadcasted_iota(i32, (CHUNK, hk), 1) % CHUNK
    causals = (rr >= cc, rr <= cc)
    kmask = (lax.broadcasted_iota(i32, (hk, hk), 0) // CHUNK
             == lax.broadcasted_iota(i32, (hk, hk), 1) // GLA_DK)
    vmask = (lax.broadcasted_iota(i32, (hk, hv), 0) // CHUNK
             == lax.broadcasted_iota(i32, (hk, hv), 1) // GLA_DV)
    smask = (lax.broadcasted_iota(i32, (hv, hk), 0) // GLA_DV
             == lax.broadcasted_iota(i32, (hv, hk), 1) // GLA_DK)
    scale = GLA_DK ** -0.5
    zero = jnp.zeros((), bf16)

    def group(gi, carry):
        chains = []
        for j in range(GLA_GROUP):
            for d in range(2):
                cj = gi * GLA_GROUP + j
                c = cj if d == 0 else nch - 1 - cj
                chains.append(dict(d=d, rows=pl.ds(pl.multiple_of(c * CHUNK, CHUNK), CHUNK)))
        for ch in chains:
            r = ins[ch['d']]
            ch['gl'] = _mm_hi(r[3][ch['rows'], :], r[4][...]) + r[5][...]
        for ch in chains:
            gl = ch['gl']
            g = (jnp.minimum(gl, 0.0) - jnp.log(1.0 + jnp.exp(-jnp.abs(gl)))) * (1.0 / GLA_TAU)
            ch['gc'] = _mm_hi(tris[ch['d']], g)
        for ch in chains:
            r, rows, gc = ins[ch['d']], ch['rows'], ch['gc']
            gtot = gc[CHUNK - 1:CHUNK, :] if ch['d'] == 0 else gc[0:1, :]
            k = r[1][rows, :]
            ch['qd'] = (r[0][rows, :] * scale * jnp.exp(gc)).astype(bf16)
            ki = (k * jnp.exp(-gc)).astype(bf16)
            ch['kt'] = (k * jnp.exp(gtot - gc)).astype(bf16)
            ch['dec'] = jnp.exp(gtot)
            kstack = jnp.where(kmask, jnp.concatenate([ki] * GLA_H, axis=0), zero)
            ch['sc'] = jnp.where(causals[ch['d']], _nt(ch['qd'], kstack), 0.0).astype(bf16)
        for ch in chains:
            v = ins[ch['d']][2][ch['rows'], :].astype(bf16)
            vbd = jnp.where(vmask, jnp.concatenate([v] * GLA_H, axis=0), zero)
            ch['oi'] = _mm(ch['sc'], vbd)
            ch['kv'] = _tn(v, ch['kt'])
        for ch in chains:
            d = ch['d']
            st = st_ref[d]
            ins[d][6][ch['rows'], :] = ch['oi'] + _nt(ch['qd'], st.astype(bf16))
            st_ref[d] = st * ch['dec'] + jnp.where(smask, ch['kv'], 0.0)
        return carry

    lax.fori_loop(0, nch // GLA_GROUP, group, 0)


def _gla(proj, wgs, bgs, blk=512):
    b, l, _ = proj.shape
    blk = min(blk, l)
    nb = l // blk
    hk, hv = GLA_H * GLA_DK, GLA_H * GLA_DV
    up, down = (lambda i: i), (lambda i: nb - 1 - i)

    def cols(bidx):
        col = lambda width, off: pl.BlockSpec((None, blk, width), lambda bi, i: (bi, bidx(i), off // width))
        return [col(hk, C_GLA_Q), col(hk, C_GLA_K), col(hv, C_GLA_V), col(LANES, C_GLA_LR)]

    wspec = pl.BlockSpec((LANES, hk), lambda bi, i: (0, 0))
    bspec = pl.BlockSpec((1, hk), lambda bi, i: (0, 0))
    out = jax.ShapeDtypeStruct((b, l, hv), f32)
    return pl.pallas_call(
        _gla_body, name="gla",
        out_shape=(out, out),
        grid=(b, nb),
        in_specs=cols(up) + cols(down) + [wspec, bspec, wspec, bspec],
        out_specs=(pl.BlockSpec((None, blk, hv), lambda bi, i: (bi, up(i), 0)),
                   pl.BlockSpec((None, blk, hv), lambda bi, i: (bi, down(i), 0))),
        scratch_shapes=[pltpu.VMEM((2, hv, hk), f32)],
        compiler_params=_cparams("parallel", "arbitrary"),
    )(*([proj] * 8), wgs[0], bgs[0], wgs[1], bgs[1])


def _gdn_prep_body(x_ref, xp_ref, xn_ref, ab_ref, cw_ref, par_ref, qkv_ref, gb_ref):
    i, n = pl.program_id(1), pl.num_programs(1)
    blk = x_ref.shape[0]
    prev = jnp.where(i > 0, xp_ref[...], 0.0)
    nxt = jnp.where(i < n - 1, xn_ref[...], 0.0)
    ext = jnp.concatenate([prev, x_ref[...], nxt], axis=0)
    tot = blk + 2 * SUBLANES
    acc = None
    for t in range(CONV_W):
        sh = (CONV_W // 2 - t) % tot
        xs = ext if sh == 0 else pltpu.roll(ext, sh, 0)
        term = xs[SUBLANES:SUBLANES + blk, :] * cw_ref[t:t + 1, :]
        acc = term if acc is None else acc + term
    y = _silu(acc)
    nqk = 2 * GDN_H
    for h in range(3 * GDN_H):
        sl = slice(h * LANES, (h + 1) * LANES)
        yh = y[:, sl]
        if h < nqk:
            yh = yh * lax.rsqrt(jnp.sum(yh * yh, axis=-1, keepdims=True) + EPS)
            if h < GDN_H:
                yh = yh * GDN_DK ** -0.5
        qkv_ref[:, sl] = yh
    x = ab_ref[...]
    lane = lax.broadcasted_iota(i32, x.shape, 1)
    xa = x + par_ref[1:2, :]
    softplus = jnp.maximum(xa, 0.0) + jnp.log(1.0 + jnp.exp(-jnp.abs(xa)))
    gb_ref[...] = jnp.where(lane < nqk, par_ref[0:1, :] * softplus, _sigmoid(x))


def _gdn_prep(proj, conv_w, par, blk=256):
    b, l, _ = proj.shape
    blk = min(blk, l)
    nb = l // blk
    w = 3 * BW
    r8 = blk // SUBLANES
    last8 = l // SUBLANES - 1
    return pl.pallas_call(
        _gdn_prep_body, name="gdn_prep",
        out_shape=(jax.ShapeDtypeStruct((b, l, w), f32), jax.ShapeDtypeStruct((b, l, LANES), f32)),
        grid=(b, nb),
        in_specs=[pl.BlockSpec((None, blk, w), lambda bi, i: (bi, i, C_GDN_QKV // w)),
                  pl.BlockSpec((None, SUBLANES, w), lambda bi, i: (bi, jnp.maximum(i * r8 - 1, 0), C_GDN_QKV // w)),
                  pl.BlockSpec((None, SUBLANES, w), lambda bi, i: (bi, jnp.minimum((i + 1) * r8, last8), C_GDN_QKV // w)),
                  pl.BlockSpec((None, blk, LANES), lambda bi, i: (bi, i, C_GDN_AB // LANES)),
                  pl.BlockSpec((SUBLANES, w), lambda bi, i: (0, 0)),
                  pl.BlockSpec((SUBLANES, LANES), lambda bi, i: (0, 0))],
        out_specs=(pl.BlockSpec((None, blk, w), lambda bi, i: (bi, i, 0)),
                   pl.BlockSpec((None, blk, LANES), lambda bi, i: (bi, i, 0))),
        compiler_params=_cparams("parallel", "parallel"),
    )(proj, proj, proj, proj, conv_w, par)


def _mmb(a, b):
    return _mm(a.astype(bf16), b.astype(bf16))


def _unit_tri_inverses(lws, eye, bd16):
    lds = [jnp.where(bd16, lw, 0.0) for lw in lws]
    los = [lw - ld for lw, ld in zip(lws, lds)]
    ps = [eye - ld for ld in lds]
    pw = lds
    for _ in range(3):
        pw = [_mmb(x, x) for x in pw]
        ps = [p + _mmb(p, x) for p, x in zip(ps, pw)]
    ms = [_mmb(p, lo) for p, lo in zip(ps, los)]
    m2s = [_mmb(m, m) for m in ms]
    qs = [eye - m for m in ms]
    qs = [q + _mmb(q, m2) for q, m2 in zip(qs, m2s)]
    return [_mmb(q, p) for q, p in zip(qs, ps)]


def _gdn_par_body(q_ref, k_ref, v_ref, gb_ref, *out_refs):
    nch = q_ref.shape[0] // CHUNK
    r64 = lax.broadcasted_iota(i32, (CHUNK, CHUNK), 0)
    c64 = lax.broadcasted_iota(i32, (CHUNK, CHUNK), 1)
    eye = _ind(r64 == c64)
    bd16 = (r64 // 16) == (c64 // 16)
    incls = (r64 >= c64, r64 <= c64)
    stricts = (r64 > c64, r64 < c64)
    tris = tuple(_ind(m) for m in incls)

    def chunk_group(ci, carry):
        chains = []
        for j in range(GDN_GROUP):
            c = ci * GDN_GROUP + j
            rows = pl.ds(pl.multiple_of(c * CHUNK, CHUNK), CHUNK)
            gcols = gb_ref[rows, :]
            gams = tuple(_mm_hi(t, gcols) for t in tris)
            gam_ts = tuple(g.T for g in gams)
            for d in range(2):
                out_refs[6 * d + 5][c] = jnp.exp(gams[d][CHUNK - 1:CHUNK, :] if d == 0 else gams[d][0:1, :])
            for h in range(GDN_H):
                sl = slice(h * LANES, (h + 1) * LANES)
                qh, kh, vh = q_ref[rows, sl], k_ref[rows, sl], v_ref[rows, sl]
                raw = _nt(jnp.concatenate([qh, kh], axis=0).astype(bf16), kh.astype(bf16))
                for d in range(2):
                    lg = d * GDN_H + h
                    chains.append(dict(rows=rows, h=h, d=d, qh=qh, kh=kh, vh=vh, raw=raw,
                                       gcol=gams[d][:, lg:lg + 1], grow=gam_ts[d][lg:lg + 1, :],
                                       beta=gcols[:, 2 * GDN_H + lg:2 * GDN_H + lg + 1]))
        lws = []
        for ch in chains:
            d, rows, h = ch['d'], ch['rows'], ch['h']
            dec = jnp.where(incls[d], jnp.exp(jnp.where(incls[d], ch['gcol'] - ch['grow'], 0.0)), 0.0)
            out_refs[6 * d + 4][rows, h * CHUNK:(h + 1) * CHUNK] = (ch['raw'][:CHUNK] * dec).astype(bf16)
            lws.append(jnp.where(stricts[d], ch['raw'][CHUNK:] * dec * ch['beta'], 0.0))
        tinvs = _unit_tri_inverses(lws, eye, bd16)
        sols = []
        for ch, tinv in zip(chains, tinvs):
            ch['eg'] = jnp.exp(ch['gcol'])
            kb = ch['kh'] * ch['beta']
            sols.append(_mmb(tinv, jnp.concatenate([ch['vh'] * ch['beta'], kb * ch['eg']], axis=1)))
        for ch, sol in zip(chains, sols):
            d, rows = ch['d'], ch['rows']
            sl = slice(ch['h'] * LANES, (ch['h'] + 1) * LANES)
            u_ref, w_ref, qd_ref, kt_ref = out_refs[6 * d:6 * d + 4]
            gcol = ch['gcol']
            gtot = gcol[CHUNK - 1:CHUNK, :] if d == 0 else gcol[0:1, :]
            u_ref[rows, sl] = sol[:, :GDN_DV]
            w_ref[rows, sl] = sol[:, GDN_DV:].astype(bf16)
            qd_ref[rows, sl] = (ch['qh'] * ch['eg']).astype(bf16)
            kt_ref[rows, sl] = (ch['kh'] * jnp.exp(gtot - gcol)).astype(bf16)
        return carry

    lax.fori_loop(0, nch // GDN_GROUP, chunk_group, 0)


def _gdn_par(qkv, gb, blk=512):
    b, l, _ = qkv.shape
    blk = min(blk, l)
    nch = blk // CHUNK
    col = lambda j: pl.BlockSpec((None, blk, BW), lambda bi, i: (bi, i, j))
    wide = pl.BlockSpec((None, blk, BW), lambda bi, i: (bi, i, 0))
    one_dir_shapes = (jax.ShapeDtypeStruct((b, l, BW), f32), jax.ShapeDtypeStruct((b, l, BW), bf16),
                      jax.ShapeDtypeStruct((b, l, BW), bf16), jax.ShapeDtypeStruct((b, l, BW), bf16),
                      jax.ShapeDtypeStruct((b, l, GDN_H * CHUNK), bf16),
                      jax.ShapeDtypeStruct((b, l // CHUNK, 1, LANES), f32))
    one_dir_specs = (wide, wide, wide, wide,
                     pl.BlockSpec((None, blk, GDN_H * CHUNK), lambda bi, i: (bi, i, 0)),
                     pl.BlockSpec((None, nch, 1, LANES), lambda bi, i: (bi, i, 0, 0)))
    outs = pl.pallas_call(
        _gdn_par_body, name="gdn_par",
        out_shape=one_dir_shapes * 2,
        grid=(b, l // blk),
        in_specs=[col(0), col(1), col(2), pl.BlockSpec((None, blk, LANES), lambda bi, i: (bi, i, 0))],
        out_specs=one_dir_specs * 2,
        compiler_params=_cparams("parallel", "parallel"),
    )(qkv, qkv, qkv, gb)
    return outs[:6], outs[6:]


def _gdn_seq_body(*refs):
    ins = (refs[0:6], refs[6:12])
    o_refs = refs[12:14]
    s_ref = refs[14]
    nch = o_refs[0].shape[0] // CHUNK

    @pl.when(pl.program_id(1) == 0)
    def _():
        s_ref[...] = jnp.zeros_like(s_ref)

    def chunk(ci, carry):
        rows, cds = [], []
        for d in range(2):
            c = ci if d == 0 else nch - 1 - ci
            rows.append(pl.ds(pl.multiple_of(c * CHUNK, CHUNK), CHUNK))
            cds.append(ins[d][5][c])
        dh = [(d, h) for d in range(2) for h in range(GDN_H)]
        sl = lambda h: slice(h * LANES, (h + 1) * LANES)
        ss = [s_ref[d, sl(h), :] for d, h in dh]
        sbs = [s.astype(bf16) for s in ss]
        wss = [_mm(ins[d][1][rows[d], sl(h)], sb) for (d, h), sb in zip(dh, sbs)]
        qss = [_mm(ins[d][2][rows[d], sl(h)], sb) for (d, h), sb in zip(dh, sbs)]
        vns = [(ins[d][0][rows[d], sl(h)] - ws).astype(bf16) for (d, h), ws in zip(dh, wss)]
        avs = [_mm(ins[d][4][rows[d], h * CHUNK:(h + 1) * CHUNK], vn) for (d, h), vn in zip(dh, vns)]
        kvs = [_tn(ins[d][3][rows[d], sl(h)], vn) for (d, h), vn in zip(dh, vns)]
        for (d, h), s, kv in zip(dh, ss, kvs):
            lg = d * GDN_H + h
            s_ref[d, sl(h), :] = s * cds[d][:, lg:lg + 1] + kv
        for d in range(2):
            o_refs[d][rows[d], :] = jnp.concatenate(
                [qs + av for (dd, _), qs, av in zip(dh, qss, avs) if dd == d], axis=1)
        return carry

    lax.fori_loop(0, nch, chunk, 0)


def _gdn_seq(fwd_in, bwd_in, blk=512):
    b, l, _ = fwd_in[0].shape
    blk = min(blk, l)
    nb = l // blk
    nch = blk // CHUNK

    def specs(bidx):
        wide = pl.BlockSpec((None, blk, BW), lambda bi, i: (bi, bidx(i), 0))
        return [wide, wide, wide, wide,
                pl.BlockSpec((None, blk, GDN_H * CHUNK), lambda bi, i: (bi, bidx(i), 0)),
                pl.BlockSpec((None, nch, 1, LANES), lambda bi, i: (bi, bidx(i), 0, 0))]

    up, down = (lambda i: i), (lambda i: nb - 1 - i)
    out = jax.ShapeDtypeStruct((b, l, BW), f32)
    return pl.pallas_call(
        _gdn_seq_body, name="gdn_seq",
        out_shape=(out, out),
        grid=(b, nb),
        in_specs=specs(up) + specs(down),
        out_specs=(pl.BlockSpec((None, blk, BW), lambda bi, i: (bi, up(i), 0)),
                   pl.BlockSpec((None, blk, BW), lambda bi, i: (bi, down(i), 0))),
        scratch_shapes=[pltpu.VMEM((2, GDN_H * GDN_DK, GDN_DV), f32)],
        compiler_params=_cparams("parallel", "arbitrary"),
    )(*fwd_in, *bwd_in)


def _head_norm_gate(o, gain, z):
    outs = []
    for h in range(BW // LANES):
        oh = o[:, h * LANES:(h + 1) * LANES]
        outs.append(oh * lax.rsqrt(jnp.mean(oh * oh, axis=-1, keepdims=True) + EPS) * gain)
    return jnp.concatenate(outs, axis=1) * _silu(z)


def _merge_body(h_ref, gate_ref, ch_ref, u_ref, glaf_ref, glab_ref, r_ref, gdnf_ref, gdnb_ref, z_ref,
                dsk_ref, wglu_ref, gng_ref, dng_ref, wbr_ref, wout_ref, o_ref):
    y0 = _gelu_tanh(ch_ref[...] + dsk_ref[...] * u_ref[...])
    y_s5 = y0 * _sigmoid(_mm(y0.astype(bf16), wglu_ref[...]))
    y_gla = _head_norm_gate(glaf_ref[...] + glab_ref[...], gng_ref[...], r_ref[...])
    y_gdn = _head_norm_gate(gdnf_ref[...] + gdnb_ref[...], dng_ref[...], z_ref[...])
    merged = None
    for r, y in enumerate((y_s5, y_gla, y_gdn)):
        gate = _sigmoid(gate_ref[:, r * D_MODEL:(r + 1) * D_MODEL])
        term = gate * _mm(y.astype(bf16), wbr_ref[r])
        merged = term if merged is None else merged + term
    o_ref[...] = h_ref[...] + _mm(merged.astype(bf16), wout_ref[...])


def _merge(h2d, proj2d, ch, gla_f, gla_b, gdn_f, gdn_b, dsk, wglu, gng, dng, wbr, wout, tm=256):
    m = h2d.shape[0]
    row = lambda width, off=0: pl.BlockSpec((tm, width), lambda i: (i, off // width))
    full = lambda shape: pl.BlockSpec(shape, lambda i: (0,) * len(shape))
    return pl.pallas_call(
        _merge_body, name="merge",
        out_shape=jax.ShapeDtypeStruct((m, D_MODEL), f32),
        grid=(m // tm,),
        in_specs=[row(D_MODEL), row(3 * D_MODEL, C_GATE), row(BW), row(BW, C_S5), row(BW), row(BW),
                  row(BW, C_GLA_R), row(BW), row(BW), row(BW, C_GDN_Z),
                  full((1, BW)), full((BW, BW)), full((1, LANES)), full((1, LANES)),
                  full((3, BW, D_MODEL)), full((D_MODEL, D_MODEL))],
        out_specs=row(D_MODEL),
        compiler_params=_cparams("parallel"),
    )(h2d, proj2d, ch, proj2d, gla_f, gla_b, proj2d, gdn_f, gdn_b, proj2d, dsk, wglu, gng, dng, wbr, wout)


def _router_body(h_ref, g_ref, wr_ref, hn_ref, aff_ref):
    hn = _rms(h_ref[...], g_ref[...])
    hn_ref[...] = hn.astype(bf16)
    logits = lax.dot_general(wr_ref[...], hn, (((1,), (1,)), ((), ())), preferred_element_type=f32,
                             precision=HIGHEST)
    e = jnp.exp(logits - jnp.max(logits, axis=0, keepdims=True))
    aff_ref[...] = e / jnp.sum(e, axis=0, keepdims=True)


def _router(h2d, gain, wr_t, tm=512):
    m = h2d.shape[0]
    return pl.pallas_call(
        _router_body, name="router",
        out_shape=(jax.ShapeDtypeStruct((m, D_MODEL), bf16), jax.ShapeDtypeStruct((N_EXPERTS, m), f32)),
        grid=(m // tm,),
        in_specs=[pl.BlockSpec((tm, D_MODEL), lambda i: (i, 0)),
                  pl.BlockSpec((1, D_MODEL), lambda i: (0, 0)),
                  pl.BlockSpec((N_EXPERTS, D_MODEL), lambda i: (0, 0))],
        out_specs=(pl.BlockSpec((tm, D_MODEL), lambda i: (i, 0)), pl.BlockSpec((N_EXPERTS, tm), lambda i: (0, i))),
        compiler_params=_cparams("parallel"),
    )(h2d, gain, wr_t)


def _threshold_body(aff_ref, thr_ref, *, cap):
    keys = pltpu.bitcast(aff_ref[...], i32)

    def count(mask):
        return jnp.sum(jnp.where(mask, 1.0, 0.0), axis=1, keepdims=True).astype(i32)

    def bit(bi, t):
        cand = t | (1 << (30 - bi))
        return jnp.where(count(keys >= cand) >= cap, cand, t)

    t = lax.fori_loop(0, 31, bit, jnp.zeros((N_EXPERTS, 1), i32))
    budget = cap - count(keys > t)
    lane = lax.broadcasted_iota(i32, (N_EXPERTS, LANES), 1)
    thr_ref[...] = jnp.where(lane == 0, t, jnp.where(lane == 1, budget, 0))


def _threshold(aff_t, cap):
    n = aff_t.shape[1]
    return pl.pallas_call(
        functools.partial(_threshold_body, cap=cap), name="topc_threshold",
        out_shape=jax.ShapeDtypeStruct((N_EXPERTS, LANES), i32),
        in_specs=[pl.BlockSpec((N_EXPERTS, n), lambda: (0, 0))],
        out_specs=pl.BlockSpec((N_EXPERTS, LANES), lambda: (0, 0)),
        compiler_params=pltpu.CompilerParams(vmem_limit_bytes=VMEM_LIMIT_BYTES),
    )(aff_t)


def _slots_body(aff_ref, thr_ref, slot_ref, wts_ref, cnt_ref, run_ref):
    @pl.when(pl.program_id(0) == 0)
    def _():
        run_ref[...] = jnp.zeros_like(run_ref)

    tt = aff_ref.shape[1]
    aff = aff_ref[...]
    keys = pltpu.bitcast(aff, i32)
    t = thr_ref[:, 0:1]
    budget = thr_ref[:, 1:2]
    upper = _ind(lax.broadcasted_iota(i32, (tt, tt), 0) <= lax.broadcasted_iota(i32, (tt, tt), 1), bf16)
    eq = keys == t
    sel_run = run_ref[:, 0:1]
    tie_run = run_ref[:, 1:2]
    cs_eq = _mm(_ind(eq, bf16), upper).astype(i32)
    tie_rank = tie_run + cs_eq - 1
    sel = (keys > t) | (eq & (tie_rank < budget))
    cs_sel = _mm(_ind(sel, bf16), upper).astype(i32)
    slot_ref[...] = jnp.where(sel, sel_run + cs_sel - 1, -1)
    wts_ref[...] = jnp.where(sel, aff, 0.0)
    n_sel = cs_sel[:, tt - 1:tt]
    n_eq = cs_eq[:, tt - 1:tt]
    cnt_ref[...] = jnp.broadcast_to(n_sel, cnt_ref.shape)
    lane = lax.broadcasted_iota(i32, run_ref.shape, 1)
    run_ref[...] = run_ref[...] + jnp.where(lane == 0, n_sel, jnp.where(lane == 1, n_eq, 0))


def _slots(aff_t, thr):
    n = aff_t.shape[1]
    nt = n // MOE_TT
    return pl.pallas_call(
        _slots_body, name="topc_slots",
        out_shape=(jax.ShapeDtypeStruct((N_EXPERTS, n), i32), jax.ShapeDtypeStruct((N_EXPERTS, n), f32),
                   jax.ShapeDtypeStruct((nt, N_EXPERTS, LANES), i32)),
        grid=(nt,),
        in_specs=[pl.BlockSpec((N_EXPERTS, MOE_TT), lambda i: (0, i)),
                  pl.BlockSpec((N_EXPERTS, LANES), lambda i: (0, 0))],
        out_specs=(pl.BlockSpec((N_EXPERTS, MOE_TT), lambda i: (0, i)),
                   pl.BlockSpec((N_EXPERTS, MOE_TT), lambda i: (0, i)),
                   pl.BlockSpec((None, N_EXPERTS, LANES), lambda i: (i, 0, 0))),
        scratch_shapes=[pltpu.VMEM((N_EXPERTS, LANES), i32)],
        compiler_params=_cparams("arbitrary"),
    )(aff_t, thr)


def _pair_tables(cnt, cap):
    e, nt = cnt.shape
    nj = cap // MOE_SB
    npair = nt + nj - 1
    off = jnp.concatenate([jnp.zeros((e, 1), i32), jnp.cumsum(cnt, axis=1, dtype=i32)], axis=1)
    j_lo = off[:, :-1] // MOE_SB
    j_hi = (off[:, 1:] - 1) // MOE_SB
    per_tile = jnp.where(cnt > 0, j_hi - j_lo + 1, 0)
    ps = jnp.concatenate([jnp.zeros((e, 1), i32), jnp.cumsum(per_tile, axis=1, dtype=i32)], axis=1)
    total = ps[:, -1:]
    p = jnp.arange(npair, dtype=i32)[None, :]
    pc = jnp.minimum(p, total - 1)
    tile = jax.vmap(lambda a, v: jnp.searchsorted(a, v, side='right'))(ps, pc).astype(i32) - 1
    tile = jnp.clip(tile, 0, nt - 1)
    blk = jnp.take_along_axis(j_lo, tile, axis=1) + (pc - jnp.take_along_axis(ps, tile, axis=1))
    valid = p < total
    prev_blk = jnp.concatenate([jnp.full((e, 1), -1, i32), blk[:, :-1]], axis=1)
    next_blk = jnp.concatenate([blk[:, 1:], jnp.full((e, 1), -1, i32)], axis=1)
    first = valid & (blk != prev_blk)
    last = valid & ((blk != next_blk) | (p == total - 1))
    flags = valid.astype(i32) + 2 * first.astype(i32) + 4 * last.astype(i32)
    ee = jnp.broadcast_to(jnp.arange(e, dtype=i32)[:, None], (e, npair))
    big = nt * (e + 1)
    key_pairs = jnp.where(valid, tile * (e + 1) + 1 + ee, big).reshape(-1)
    key_init = jnp.arange(nt, dtype=i32) * (e + 1)
    keys = jnp.concatenate([key_init, key_pairs])
    c_tile = jnp.concatenate([jnp.arange(nt, dtype=i32), tile.reshape(-1)])
    c_exp = jnp.concatenate([jnp.zeros((nt,), i32), ee.reshape(-1)])
    c_blk = jnp.concatenate([jnp.zeros((nt,), i32), blk.reshape(-1)])
    c_valid = jnp.concatenate([jnp.zeros((nt,), i32), valid.reshape(-1).astype(i32)])
    order = jnp.argsort(keys, stable=True)
    keys_s = keys[order]
    live = keys_s < big
    c_tile = jnp.where(live, c_tile[order], nt - 1)
    c_exp = jnp.where(live, c_exp[order], 0)
    c_blk = jnp.where(live, c_blk[order], 0)
    c_valid = jnp.where(live, c_valid[order], 0)
    is_init = live & (keys_s % (e + 1) == 0)
    nxt_tile = jnp.concatenate([c_tile[1:], jnp.full((1,), -1, i32)])
    nxt_live = jnp.concatenate([live[1:], jnp.zeros((1,), bool)])
    c_last = live & ((nxt_tile != c_tile) | ~nxt_live)
    c_flags = c_valid + 2 * is_init.astype(i32) + 4 * c_last.astype(i32)
    return (tile, blk, flags), (c_tile, c_exp, c_blk, c_flags)


def _experts_body(tile_ref, blk_ref, flag_ref, x_ref, slot_ref, wts_ref, wg_ref, wu_ref, wd_ref, y_ref,
                  xe_ref, ge_ref):
    e, p = pl.program_id(0), pl.program_id(1)
    fl = flag_ref[e, p]
    sb, tt = xe_ref.shape[0], x_ref.shape[0]

    @pl.when((fl & 2) != 0)
    def _():
        xe_ref[...] = jnp.zeros_like(xe_ref)
        ge_ref[...] = jnp.zeros_like(ge_ref)

    @pl.when((fl & 1) != 0)
    def _():
        rel = slot_ref[...] - blk_ref[e, p] * sb
        hit = lax.broadcasted_iota(i32, (sb, tt), 0) == rel
        xe_ref[...] += _mm(_ind(hit, bf16), x_ref[...])
        gsel = jnp.where(hit, wts_ref[...], 0.0)
        acc = gsel[:, 0:LANES]
        for c in range(1, tt // LANES):
            acc = acc + gsel[:, c * LANES:(c + 1) * LANES]
        ge_ref[...] += acc

    @pl.when((fl & 4) != 0)
    def _():
        x = xe_ref[...].astype(bf16)
        hid = (_silu(_mm(x, wg_ref[...])) * _mm(x, wu_ref[...])).astype(bf16)
        gate = jnp.sum(ge_ref[...], axis=1, keepdims=True)
        y_ref[...] = (_mm(hid, wd_ref[...]) * gate).astype(y_ref.dtype)


def _experts(hn, slot3, wts3, tabs, wg, wu, wd, cap):
    tile, blk, flags = tabs
    npair = tile.shape[1]
    grid_spec = pltpu.PrefetchScalarGridSpec(
        num_scalar_prefetch=3,
        grid=(N_EXPERTS, npair),
        in_specs=[pl.BlockSpec((MOE_TT, D_MODEL), lambda e, p, t, b, f: (t[e, p], 0)),
                  pl.BlockSpec((None, 1, MOE_TT), lambda e, p, t, b, f: (e, 0, t[e, p])),
                  pl.BlockSpec((None, 1, MOE_TT), lambda e, p, t, b, f: (e, 0, t[e, p])),
                  pl.BlockSpec((None, D_MODEL, EXPERT_FF), lambda e, p, t, b, f: (e, 0, 0)),
                  pl.BlockSpec((None, D_MODEL, EXPERT_FF), lambda e, p, t, b, f: (e, 0, 0)),
                  pl.BlockSpec((None, EXPERT_FF, D_MODEL), lambda e, p, t, b, f: (e, 0, 0))],
        out_specs=pl.BlockSpec((None, MOE_SB, D_MODEL), lambda e, p, t, b, f: (e, b[e, p], 0)),
        scratch_shapes=[pltpu.VMEM((MOE_SB, D_MODEL), f32), pltpu.VMEM((MOE_SB, LANES), f32)])
    return pl.pallas_call(
        _experts_body, name="experts",
        out_shape=jax.ShapeDtypeStruct((N_EXPERTS, cap, D_MODEL), bf16),
        grid_spec=grid_spec,
        compiler_params=_cparams("arbitrary", "arbitrary"),
    )(tile, blk, flags, hn, slot3, wts3, wg, wu, wd)


def _combine_body(tile_ref, exp_ref, blk_ref, flag_ref, h_ref, slot_ref, y_ref, p_ref, g_ref, wpg_ref, wpp_ref,
                  o_ref, acc_ref):
    s = pl.program_id(0)
    fl = flag_ref[s]
    tt, sb = acc_ref.shape[0], y_ref.shape[0]

    @pl.when((fl & 2) != 0)
    def _():
        acc_ref[...] = h_ref[...]

    @pl.when((fl & 1) != 0)
    def _():
        rel = slot_ref[...] - blk_ref[s] * sb
        hit = lax.broadcasted_iota(i32, (sb, tt), 0) == rel
        acc_ref[...] += _tn(_ind(hit, bf16), y_ref[...])

    @pl.when((fl & 4) != 0)
    def _():
        h2 = acc_ref[...]
        gate = _sigmoid(_mm(_rms(h2, g_ref[...]).astype(bf16), wpg_ref[...]))
        o_ref[...] = h2 + gate * _mm(p_ref[...].astype(bf16), wpp_ref[...])


def _combine(h2d, slot3, ye, p2d, tabs, g_ple, wpg, wpp):
    c_tile, c_exp, c_blk, c_flags = tabs
    m = h2d.shape[0]
    full = lambda shape: pl.BlockSpec(shape, lambda s, t, e, b, f: (0,) * len(shape))
    grid_spec = pltpu.PrefetchScalarGridSpec(
        num_scalar_prefetch=4,
        grid=(c_tile.shape[0],),
        in_specs=[pl.BlockSpec((MOE_TT, D_MODEL), lambda s, t, e, b, f: (t[s], 0)),
                  pl.BlockSpec((None, 1, MOE_TT), lambda s, t, e, b, f: (e[s], 0, t[s])),
                  pl.BlockSpec((None, MOE_SB, D_MODEL), lambda s, t, e, b, f: (e[s], b[s], 0)),
                  pl.BlockSpec((MOE_TT, PLE_DIM), lambda s, t, e, b, f: (t[s], 0)),
                  full((1, D_MODEL)), full((D_MODEL, D_MODEL)), full((PLE_DIM, D_MODEL))],
        out_specs=pl.BlockSpec((MOE_TT, D_MODEL), lambda s, t, e, b, f: (t[s], 0)),
        scratch_shapes=[pltpu.VMEM((MOE_TT, D_MODEL), f32)])
    return pl.pallas_call(
        _combine_body, name="combine_ple",
        out_shape=jax.ShapeDtypeStruct((m, D_MODEL), f32),
        grid_spec=grid_spec,
        compiler_params=_cparams("arbitrary"),
    )(c_tile, c_exp, c_blk, c_flags, h2d, slot3, ye, p2d, g_ple, wpg, wpp)


def _moe_ple(h2d, p2d, g_ffn, wr_t, wg, wu, wd, g_ple, wpg, wpp):
    n = h2d.shape[0]
    cap = max(1, EC_FACTOR * n // N_EXPERTS)
    hn, aff_t = _router(h2d, g_ffn, wr_t)
    thr = _threshold(aff_t, cap)
    slot, wts, cnt = _slots(aff_t, thr)
    etabs, ctabs = _pair_tables(jnp.transpose(cnt[:, :, 0]), cap)
    slot3 = slot.reshape(N_EXPERTS, 1, n)
    ye = _experts(hn, slot3, wts.reshape(N_EXPERTS, 1, n), etabs, wg, wu, wd, cap)
    return _combine(h2d, slot3, ye, p2d, ctabs, g_ple, wpg, wpp)


def _final_norm_body(x_ref, g_ref, o_ref):
    o_ref[...] = _rms(x_ref[...], g_ref[...])


def _final_norm(h2d, gain, tm=1024):
    m = h2d.shape[0]
    return pl.pallas_call(
        _final_norm_body, name="final_norm",
        out_shape=jax.ShapeDtypeStruct((m, D_MODEL), f32),
        grid=(m // tm,),
        in_specs=[pl.BlockSpec((tm, D_MODEL), lambda i: (i, 0)), pl.BlockSpec((1, D_MODEL), lambda i: (0, 0))],
        out_specs=pl.BlockSpec((tm, D_MODEL), lambda i: (i, 0)),
        compiler_params=_cparams("parallel"),
    )(h2d, gain)


def _layer_weights(w):
    lw = {}
    lw['g_mix'] = w['norm_mix'].reshape(1, D_MODEL)
    lw['w_in'] = _reorder_w_in(w['w_in'])
    lw['s5'] = _s5_weights(w['s5_B_re'], w['s5_B_im'], w['s5_C_re'], w['s5_C_im'],
                           w['s5_lam_re'], w['s5_lam_im'], w['s5_log_dt'])
    wgate = w['gla_w_gate']
    lw['gla_wg'] = tuple(jnp.zeros((LANES, GLA_H * GLA_DK), f32).at[d * GLA_RANK:(d + 1) * GLA_RANK].set(wgate[d])
                         for d in range(2))
    lw['gla_bg'] = tuple(w['gla_b_gate'][d].reshape(1, -1) for d in range(2))
    lw['conv_w'] = jnp.pad(jnp.transpose(w['gdn_conv']), ((0, SUBLANES - CONV_W), (0, 0)))
    neg_a = -jnp.exp(w['gdn_A_log']).reshape(-1)
    par = jnp.zeros((SUBLANES, LANES), f32).at[0, :2 * GDN_H].set(neg_a).at[1, :2 * GDN_H].set(
        w['gdn_dt_bias'].reshape(-1))
    lw['gdn_par'] = par
    lw['dsk'] = w['s5_D'].reshape(1, BW)
    lw['wglu'] = w['s5_w_glu'].astype(bf16)
    lw['gng'] = w['gla_norm'].reshape(1, LANES)
    lw['dng'] = w['gdn_norm'].reshape(1, LANES)
    lw['wbr'] = w['w_branch'].astype(bf16)
    lw['wout'] = w['w_out'].astype(bf16)
    lw['g_ffn'] = w['norm_ffn'].reshape(1, D_MODEL)
    lw['wr_t'] = jnp.transpose(w['w_router'])
    lw['wg'] = w['w_exp_gate'].astype(bf16)
    lw['wu'] = w['w_exp_up'].astype(bf16)
    lw['wd'] = w['w_exp_down'].astype(bf16)
    lw['g_ple'] = w['norm_ple'].reshape(1, D_MODEL)
    lw['wpg'] = w['w_ple_gate'].astype(bf16)
    lw['wpp'] = w['w_ple_proj'].astype(bf16)
    return lw


def _mixers(h, lw):
    b, l, _ = h.shape
    h2d = h.reshape(b * l, D_MODEL)
    proj2d = _inproj(h2d, lw['g_mix'], lw['w_in'])
    proj = proj2d.reshape(b, l, D_INP)
    ch = _s5_mixer(proj, lw['s5'])
    gla_f, gla_b = _gla(proj, lw['gla_wg'], lw['gla_bg'])
    qkv, gb = _gdn_prep(proj, lw['conv_w'], lw['gdn_par'])
    gdn_f, gdn_b = _gdn_seq(*_gdn_par(qkv, gb))
    flat = lambda a: a.reshape(b * l, a.shape[-1])
    return _merge(h2d, proj2d, flat(ch), flat(gla_f), flat(gla_b), flat(gdn_f), flat(gdn_b),
                  lw['dsk'], lw['wglu'], lw['gng'], lw['dng'], lw['wbr'], lw['wout'])


def _layer(h, p_i, lw):
    b, l, _ = h.shape
    h1 = _mixers(h, lw)
    h3 = _moe_ple(h1, p_i.reshape(b * l, PLE_DIM), lw['g_ffn'], lw['wr_t'], lw['wg'], lw['wu'], lw['wd'],
                  lw['g_ple'], lw['wpg'], lw['wpp'])
    return h3.reshape(b, l, D_MODEL)


def kernel(x_prompt, x_sample, p_prompt, p_sample, norm_mix, w_in, s5_B_re, s5_B_im, s5_C_re, s5_C_im, s5_D, s5_lam_re, s5_lam_im, s5_log_dt, s5_w_glu, gla_w_gate, gla_b_gate, gla_norm, gdn_conv, gdn_A_log, gdn_dt_bias, gdn_norm, w_branch, w_out, norm_ffn, w_router, w_exp_gate, w_exp_up, w_exp_down, norm_ple, w_ple_gate, w_ple_proj, norm_final):
    weights = dict(norm_mix=norm_mix, w_in=w_in, s5_B_re=s5_B_re, s5_B_im=s5_B_im, s5_C_re=s5_C_re, s5_C_im=s5_C_im,
                   s5_D=s5_D, s5_lam_re=s5_lam_re, s5_lam_im=s5_lam_im, s5_log_dt=s5_log_dt, s5_w_glu=s5_w_glu,
                   gla_w_gate=gla_w_gate, gla_b_gate=gla_b_gate, gla_norm=gla_norm, gdn_conv=gdn_conv,
                   gdn_A_log=gdn_A_log, gdn_dt_bias=gdn_dt_bias, gdn_norm=gdn_norm, w_branch=w_branch, w_out=w_out,
                   norm_ffn=norm_ffn, w_router=w_router, w_exp_gate=w_exp_gate, w_exp_up=w_exp_up,
                   w_exp_down=w_exp_down, norm_ple=norm_ple, w_ple_gate=w_ple_gate, w_ple_proj=w_ple_proj)

    def body(carry, xs):
        hp, hs = carry
        w_i, pp, ps = xs
        lw = _layer_weights(w_i)
        return (_layer(hp, pp, lw), _layer(hs, ps, lw)), None

    (hp, hs), _ = lax.scan(body, (x_prompt.astype(f32), x_sample.astype(f32)), (weights, p_prompt, p_sample))
    g_fin = norm_final.reshape(1, D_MODEL)
    yp = _final_norm(hp.reshape(-1, D_MODEL), g_fin).reshape(x_prompt.shape).astype(x_prompt.dtype)
    ys = _final_norm(hs.reshape(-1, D_MODEL), g_fin).reshape(x_sample.shape).astype(x_sample.dtype)
    return (yp, ys)
```

```python
import functools
import math

import jax
import jax.numpy as jnp
import numpy as np
from jax import lax
from jax.experimental import pallas as pl
from jax.experimental.pallas import tpu as pltpu

f32 = jnp.float32
bf16 = jnp.bfloat16
i32 = jnp.int32
HIGHEST = lax.Precision.HIGHEST

D_MODEL = 1024
DEPTH = 4
PLE_DIM = 256
BW = 512
EPS = 1e-6
CHUNK = 64
S5_GROUPS, S5_GC, S5_STATE = 32, 16, 64
GLA_H, GLA_DK, GLA_DV, GLA_RANK, GLA_TAU = 4, 64, 128, 16, 16.0
GDN_H, GDN_DK, GDN_DV, CONV_W = 4, 128, 128, 5
N_EXPERTS, EXPERT_FF, EC_FACTOR = 16, 2048, 2

LANES = 128
SUBLANES = 8
VMEM_LIMIT_BYTES = 56 * 1024 * 1024

C_GATE, C_S5, C_GLA_V, C_GLA_R = 0, 3072, 3584, 4096
C_GDN_QKV, C_GDN_Z, C_GLA_Q, C_GLA_K, C_GLA_LR, C_GDN_AB = 4608, 6144, 6656, 6912, 7168, 7296
D_INP = 7424

S5_T = 8
S5_LB = BW // LANES
S5_SW = 8 * S5_STATE * 2

GDN_GROUP = 2
GLA_GROUP = 4

MOE_TT = 512
MOE_W = 128
MOE_FB = 512


def _cparams(*sem):
    return pltpu.CompilerParams(dimension_semantics=sem, vmem_limit_bytes=VMEM_LIMIT_BYTES)


def _nt(a, b):
    return lax.dot_general(a, b, (((1,), (1,)), ((), ())), preferred_element_type=f32)


def _tn(a, b):
    return lax.dot_general(a, b, (((0,), (0,)), ((), ())), preferred_element_type=f32)


def _mm(a, b):
    return jnp.dot(a, b, preferred_element_type=f32)


def _mm_hi(a, b):
    return jnp.dot(a, b, preferred_element_type=f32, precision=HIGHEST)


def _mmb(a, b):
    return _mm(a.astype(bf16), b.astype(bf16))


def _ind(mask, dtype=f32):
    return jnp.where(mask, 1.0, 0.0).astype(dtype)


def _sigmoid(x):
    return 1.0 / (1.0 + jnp.exp(-x))


def _silu(x):
    return x * _sigmoid(x)


def _gelu_tanh(x):
    return 0.5 * x * (1.0 + jnp.tanh(math.sqrt(2.0 / math.pi) * (x + 0.044715 * (x * x * x))))


def _rms(x, g):
    return x * lax.rsqrt(jnp.mean(x * x, axis=-1, keepdims=True) + EPS) * g


def _inproj_body(x_ref, g_ref, w_ref, o_ref):
    xn = _rms(x_ref[...], g_ref[...]).astype(bf16)
    o_ref[...] = _mm(xn, w_ref[...])


def _inproj(x2d, gain, w_p, tm=256):
    m = x2d.shape[0]
    return pl.pallas_call(
        _inproj_body, name="inproj",
        out_shape=jax.ShapeDtypeStruct((m, D_INP), f32),
        grid=(m // tm,),
        in_specs=[pl.BlockSpec((tm, D_MODEL), lambda i: (i, 0)),
                  pl.BlockSpec((1, D_MODEL), lambda i: (0, 0)),
                  pl.BlockSpec((D_MODEL, D_INP), lambda i: (0, 0), pipeline_mode=pl.Buffered(1))],
        out_specs=pl.BlockSpec((tm, D_INP), lambda i: (i, 0)),
        compiler_params=_cparams("parallel"),
    )(x2d, gain, w_p)


def _reorder_w_in(w_in):
    o = np.cumsum((0, 512, 256, 256, 512, 512, 32, 512, 512, 512, 512, 8, 8, 3072))
    seg = lambda k: w_in[:, o[k]:o[k + 1]]
    zpad = lambda a: jnp.pad(a, ((0, 0), (0, LANES - a.shape[1])))
    parts = [seg(12), seg(0), seg(3), seg(4), seg(6), seg(7), seg(8), seg(9), seg(1), seg(2),
             zpad(seg(5)), zpad(jnp.concatenate([seg(10), seg(11)], axis=1))]
    return jnp.concatenate(parts, axis=1).astype(bf16)


def _s5_weights(b_re, b_im, c_re, c_im, lam_re, lam_im, log_dt):
    T = S5_T
    dt = jnp.exp(log_dt)[:, :, None]
    lr, li = lam_re, lam_im
    mag = jnp.exp(lr * dt)
    ab_re, ab_im = mag * jnp.cos(li * dt), mag * jnp.sin(li * dt)
    den = lr * lr + li * li
    num_re = ab_re - 1.0
    coef_re = (num_re * lr + ab_im * li) / den
    coef_im = (ab_im * lr - num_re * li) / den
    xb_re = coef_re[..., None] * b_re[None] - coef_im[..., None] * b_im[None]
    xb_im = coef_re[..., None] * b_im[None] + coef_im[..., None] * b_re[None]

    def powers(taus):
        tau = jnp.asarray(taus, lr.dtype)
        pm = jnp.exp((lr * dt)[..., None] * tau)
        ang = (li * dt)[..., None] * tau
        return pm * jnp.cos(ang), pm * jnp.sin(ang)

    p_re, p_im = powers(np.arange(T + 1))
    cp_re = c_re[None, :, :, :, None] * p_re[:, :, None] - c_im[None, :, :, :, None] * p_im[:, :, None]
    cp_im = c_re[None, :, :, :, None] * p_im[:, :, None] + c_im[None, :, :, :, None] * p_re[:, :, None]
    kern = (jnp.einsum('dgknt,dgnc->dgtkc', cp_re, xb_re, precision=HIGHEST)
            - jnp.einsum('dgknt,dgnc->dgtkc', cp_im, xb_im, precision=HIGHEST))
    s_idx = np.arange(T)[:, None]
    t_idx = np.arange(T)[None, :]
    lag_f = np.clip(t_idx - s_idx, 0, T)
    lag_b = np.clip(s_idx - t_idx, 0, T)
    m_f = jnp.asarray((t_idx >= s_idx), kern.dtype)[None, :, :, None, None]
    m_b = jnp.asarray((s_idx >= t_idx), kern.dtype)[None, :, :, None, None]
    a_g = kern[0][:, lag_f] * m_f + kern[1][:, lag_b] * m_b
    eye8 = jnp.eye(8, dtype=kern.dtype)
    a_g = a_g.reshape(S5_LB, 8, T, T, S5_GC, S5_GC)
    a_blk = jnp.einsum('jgstkc,gh->jsgcthk', a_g, eye8).reshape(S5_LB, T * LANES, T * LANES)

    def state_in(d, taus):
        e_re = p_re[d][:, :, taus][..., None] * xb_re[d][:, :, None, :] - p_im[d][:, :, taus][..., None] * xb_im[d][:, :, None, :]
        e_im = p_re[d][:, :, taus][..., None] * xb_im[d][:, :, None, :] + p_im[d][:, :, taus][..., None] * xb_re[d][:, :, None, :]
        e = jnp.stack([e_re, e_im], axis=0)
        e = e.reshape(2, S5_LB, 8, S5_STATE, T, S5_GC)
        m = jnp.einsum('pjgnsc,gh->jsgchpn', e, eye8)
        m = m.reshape(S5_LB, T, 8, S5_GC, 4, 2, 2, S5_STATE)
        m = jnp.transpose(m, (0, 1, 2, 3, 4, 6, 5, 7))
        return m.reshape(S5_LB, T * LANES, S5_SW)

    m_f_w = state_in(0, np.arange(T - 1, -1, -1))
    m_b_w = state_in(1, np.arange(T))

    def state_out(d, taus):
        r = cp_re[d][..., taus]
        im = -cp_im[d][..., taus]
        w = jnp.stack([r, im], axis=0).reshape(2, S5_LB, 8, S5_GC, S5_STATE, T)
        w = jnp.einsum('pjhknt,gh->jhpntgk', w, eye8)
        w = w.reshape(S5_LB, 4, 2, 2, S5_STATE, T, 8, S5_GC)
        w = jnp.transpose(w, (0, 1, 3, 2, 4, 5, 6, 7))
        return w.reshape(S5_LB, S5_SW, T * LANES)

    n_f_w = state_out(0, np.arange(1, T + 1))
    n_b_w = state_out(1, np.arange(T, 0, -1))

    q_re, q_im = powers(T * np.arange(8))

    def table(arr_re, arr_im, d, order):
        t = jnp.stack([arr_re[d][..., order], arr_im[d][..., order]], axis=0)
        t = t.reshape(2, 16, 2, S5_STATE, len(order))
        return jnp.transpose(t, (4, 1, 0, 2, 3)).reshape(len(order), 16 * 2 * LANES)

    asc = np.arange(8)
    dbl = np.array([1, 2, 4, 0, 0, 0, 0, 0])
    tabs = (table(q_re, q_im, 0, asc), table(q_re, q_im, 0, dbl),
            table(q_re, q_im, 1, asc[::-1]), table(q_re, q_im, 1, dbl))
    return (a_blk.astype(bf16), m_f_w.astype(bf16), m_b_w.astype(bf16), n_f_w.astype(bf16), n_b_w.astype(bf16),
            tuple(t.astype(f32) for t in tabs))


def _s5_load_chunks(u_ref):
    rows = u_ref.shape[0] // S5_T
    parts = [u_ref[pl.ds(s, rows, stride=S5_T), :] for s in range(S5_T)]
    return jnp.concatenate(parts, axis=1).astype(bf16)


def _s5_in_body(u_ref, mf_ref, mb_ref, xf_ref, xb_ref):
    lhs = _s5_load_chunks(u_ref)
    xf_ref[...] = _mm(lhs, mf_ref[...])
    xb_ref[...] = _mm(lhs, mb_ref[...])


def _s5_in(proj, m_f_w, m_b_w, tl):
    b, l, _ = proj.shape
    nc = l // S5_T
    out = jax.ShapeDtypeStruct((b, nc, S5_LB * S5_SW), f32)
    wspec = pl.BlockSpec((None, S5_T * LANES, S5_SW), lambda bi, j, t: (j, 0, 0))
    ospec = pl.BlockSpec((None, tl // S5_T, S5_SW), lambda bi, j, t: (bi, t, j))
    return pl.pallas_call(
        _s5_in_body, name="s5_in",
        out_shape=(out, out),
        grid=(b, S5_LB, l // tl),
        in_specs=[pl.BlockSpec((None, tl, LANES), lambda bi, j, t: (bi, t, C_S5 // LANES + j)), wspec, wspec],
        out_specs=(ospec, ospec),
        compiler_params=_cparams("parallel", "parallel", "parallel"),
    )(proj, m_f_w, m_b_w)


def _s5_scan_body(xf_ref, xb_ref, cf_ref, hf_ref, cb_ref, hb_ref, sf_ref, sb_ref):
    n8 = xf_ref.shape[0] // SUBLANES
    rows = lax.broadcasted_iota(i32, (SUBLANES, LANES), 0)
    re, im = slice(0, LANES), slice(LANES, 2 * LANES)

    def shifted(x, d, fwd):
        if fwd:
            return jnp.where(rows >= d, pltpu.roll(x, d, 0), 0.0)
        return jnp.where(rows < SUBLANES - d, pltpu.roll(x, SUBLANES - d, 0), 0.0)

    def local_scan(xr, xi, h_ref, fwd):
        er, ei = shifted(xr, 1, fwd), shifted(xi, 1, fwd)
        for k, d in enumerate((1, 2, 4)):
            ar, ai = h_ref[k:k + 1, re], h_ref[k:k + 1, im]
            sr, si = shifted(er, d, fwd), shifted(ei, d, fwd)
            er, ei = er + ar * sr - ai * si, ei + ar * si + ai * sr
        return er, ei

    def tile(x_ref, c_ref, h_ref, o_ref, i, sr, si, fwd):
        r0 = pl.multiple_of(i * SUBLANES, SUBLANES)
        xr, xi = x_ref[pl.ds(r0, SUBLANES), re], x_ref[pl.ds(r0, SUBLANES), im]
        er, ei = local_scan(xr, xi, h_ref, fwd)
        cr, ci = c_ref[:, re], c_ref[:, im]
        outr = er + cr * sr - ci * si
        outi = ei + cr * si + ci * sr
        o_ref[pl.ds(r0, SUBLANES), re] = outr
        o_ref[pl.ds(r0, SUBLANES), im] = outi
        e = SUBLANES - 1 if fwd else 0
        ar, ai = h_ref[0:1, re], h_ref[0:1, im]
        nr = ar * outr[e:e + 1] - ai * outi[e:e + 1] + xr[e:e + 1]
        ni = ar * outi[e:e + 1] + ai * outr[e:e + 1] + xi[e:e + 1]
        return nr, ni

    def step(i, carry):
        fr, fi, br, bi = carry
        fr, fi = tile(xf_ref, cf_ref, hf_ref, sf_ref, i, fr, fi, True)
        br, bi = tile(xb_ref, cb_ref, hb_ref, sb_ref, n8 - 1 - i, br, bi, False)
        return fr, fi, br, bi

    z = jnp.zeros((1, LANES), f32)
    lax.fori_loop(0, n8, step, (z, z, z, z))


def _s5_scan(xf, xb, tabs):
    b, nc, w = xf.shape
    ncb = w // (2 * LANES)
    xspec = pl.BlockSpec((None, nc, 2 * LANES), lambda bi, c: (bi, 0, c))
    tspec = pl.BlockSpec((SUBLANES, 2 * LANES), lambda bi, c: (0, c))
    out = jax.ShapeDtypeStruct((b, nc, w), f32)
    return pl.pallas_call(
        _s5_scan_body, name="s5_scan",
        out_shape=(out, out),
        grid=(b, ncb),
        in_specs=[xspec, xspec, tspec, tspec, tspec, tspec],
        out_specs=(xspec, xspec),
        compiler_params=_cparams("parallel", "parallel"),
    )(xf, xb, *tabs)


def _s5_out_body(u_ref, sf_ref, sb_ref, a_ref, nf_ref, nb_ref, y_ref):
    lhs = _s5_load_chunks(u_ref)
    y = (_mm(lhs, a_ref[...]) + _mm(sf_ref[...].astype(bf16), nf_ref[...])
         + _mm(sb_ref[...].astype(bf16), nb_ref[...]))
    rows = y.shape[0]
    for t in range(S5_T):
        y_ref[pl.ds(t, rows, stride=S5_T), :] = y[:, t * LANES:(t + 1) * LANES]


def _s5_out(proj, sf, sb, a_blk, n_f_w, n_b_w, tl):
    b, l, _ = proj.shape
    sspec = pl.BlockSpec((None, tl // S5_T, S5_SW), lambda bi, j, t: (bi, t, j))
    return pl.pallas_call(
        _s5_out_body, name="s5_out",
        out_shape=jax.ShapeDtypeStruct((b, l, BW), f32),
        grid=(b, S5_LB, l // tl),
        in_specs=[pl.BlockSpec((None, tl, LANES), lambda bi, j, t: (bi, t, C_S5 // LANES + j)), sspec, sspec,
                  pl.BlockSpec((None, S5_T * LANES, S5_T * LANES), lambda bi, j, t: (j, 0, 0)),
                  pl.BlockSpec((None, S5_SW, S5_T * LANES), lambda bi, j, t: (j, 0, 0)),
                  pl.BlockSpec((None, S5_SW, S5_T * LANES), lambda bi, j, t: (j, 0, 0))],
        out_specs=pl.BlockSpec((None, tl, LANES), lambda bi, j, t: (bi, t, j)),
        compiler_params=_cparams("parallel", "parallel", "parallel"),
    )(proj, sf, sb, a_blk, n_f_w, n_b_w)


def _s5_mixer(proj, s5w, tl=2048):
    a_blk, m_f_w, m_b_w, n_f_w, n_b_w, tabs = s5w
    tl = min(tl, proj.shape[1])
    xf, xb = _s5_in(proj, m_f_w, m_b_w, tl)
    sf, sb = _s5_scan(xf, xb, tabs)
    return _s5_out(proj, sf, sb, a_blk, n_f_w, n_b_w, tl)


def _gla_body(qf_ref, kf_ref, vf_ref, lrf_ref, qb_ref, kb_ref, vb_ref, lrb_ref, wgf_ref, bgf_ref, wgb_ref, bgb_ref,
              of_ref, ob_ref, st_ref):
    ins = ((qf_ref, kf_ref, vf_ref, lrf_ref, wgf_ref, bgf_ref, of_ref),
           (qb_ref, kb_ref, vb_ref, lrb_ref, wgb_ref, bgb_ref, ob_ref))
    nch = qf_ref.shape[0] // CHUNK
    hk = GLA_H * GLA_DK
    hv = GLA_H * GLA_DV

    @pl.when(pl.program_id(1) == 0)
    def _():
        st_ref[...] = jnp.zeros_like(st_ref)

    r64 = lax.broadcasted_iota(i32, (CHUNK, CHUNK), 0)
    c64 = lax.broadcasted_iota(i32, (CHUNK, CHUNK), 1)
    tris = (_ind(r64 >= c64), _ind(r64 <= c64))
    rr = lax.broadcasted_iota(i32, (CHUNK, hk), 0)
    cc = lax.broadcasted_iota(i32, (CHUNK, hk), 1) % CHUNK
    causals = (rr >= cc, rr <= cc)
    kmask = (lax.broadcasted_iota(i32, (hk, hk), 0) // CHUNK
             == lax.broadcasted_iota(i32, (hk, hk), 1) // GLA_DK)
    vmask = (lax.broadcasted_iota(i32, (hk, hv), 0) // CHUNK
             == lax.broadcasted_iota(i32, (hk, hv), 1) // GLA_DV)
    smask = (lax.broadcasted_iota(i32, (hv, hk), 0) // GLA_DV
             == lax.broadcasted_iota(i32, (hv, hk), 1) // GLA_DK)
    scale = GLA_DK ** -0.5
    zero = jnp.zeros((), bf16)

    def group(gi, carry):
        chains = []
        for j in range(GLA_GROUP):
            for d in range(2):
                cj = gi * GLA_GROUP + j
                c = cj if d == 0 else nch - 1 - cj
                chains.append(dict(d=d, rows=pl.ds(pl.multiple_of(c * CHUNK, CHUNK), CHUNK)))
        for ch in chains:
            r = ins[ch['d']]
            ch['gl'] = _mm_hi(r[3][ch['rows'], :], r[4][...]) + r[5][...]
        for ch in chains:
            gl = ch['gl']
            g = (jnp.minimum(gl, 0.0) - jnp.log(1.0 + jnp.exp(-jnp.abs(gl)))) * (1.0 / GLA_TAU)
            ch['gc'] = _mm_hi(tris[ch['d']], g)
        for ch in chains:
            r, rows, gc = ins[ch['d']], ch['rows'], ch['gc']
            gtot = gc[CHUNK - 1:CHUNK, :] if ch['d'] == 0 else gc[0:1, :]
            k = r[1][rows, :]
            ch['qd'] = (r[0][rows, :] * scale * jnp.exp(gc)).astype(bf16)
            ki = (k * jnp.exp(-gc)).astype(bf16)
            ch['kt'] = (k * jnp.exp(gtot - gc)).astype(bf16)
            ch['dec'] = jnp.exp(gtot)
            kstack = jnp.where(kmask, jnp.concatenate([ki] * GLA_H, axis=0), zero)
            ch['sc'] = jnp.where(causals[ch['d']], _nt(ch['qd'], kstack), 0.0).astype(bf16)
        for ch in chains:
            v = ins[ch['d']][2][ch['rows'], :].astype(bf16)
            vbd = jnp.where(vmask, jnp.concatenate([v] * GLA_H, axis=0), zero)
            ch['oi'] = _mm(ch['sc'], vbd)
            ch['kv'] = _tn(v, ch['kt'])
        for ch in chains:
            d = ch['d']
            st = st_ref[d]
            ins[d][6][ch['rows'], :] = ch['oi'] + _nt(ch['qd'], st.astype(bf16))
            st_ref[d] = st * ch['dec'] + jnp.where(smask, ch['kv'], 0.0)
        return carry

    lax.fori_loop(0, nch // GLA_GROUP, group, 0)


def _gla(proj, wgs, bgs, blk=512):
    b, l, _ = proj.shape
    blk = min(blk, l)
    nb = l // blk
    hk, hv = GLA_H * GLA_DK, GLA_H * GLA_DV
    up, down = (lambda i: i), (lambda i: nb - 1 - i)

    def cols(bidx):
        col = lambda width, off: pl.BlockSpec((None, blk, width), lambda bi, i: (bi, bidx(i), off // width))
        return [col(hk, C_GLA_Q), col(hk, C_GLA_K), col(hv, C_GLA_V), col(LANES, C_GLA_LR)]

    wspec = pl.BlockSpec((LANES, hk), lambda bi, i: (0, 0))
    bspec = pl.BlockSpec((1, hk), lambda bi, i: (0, 0))
    out = jax.ShapeDtypeStruct((b, l, hv), f32)
    return pl.pallas_call(
        _gla_body, name="gla",
        out_shape=(out, out),
        grid=(b, nb),
        in_specs=cols(up) + cols(down) + [wspec, bspec, wspec, bspec],
        out_specs=(pl.BlockSpec((None, blk, hv), lambda bi, i: (bi, up(i), 0)),
                   pl.BlockSpec((None, blk, hv), lambda bi, i: (bi, down(i), 0))),
        scratch_shapes=[pltpu.VMEM((2, hv, hk), f32)],
        compiler_params=_cparams("parallel", "arbitrary"),
    )(*([proj] * 8), wgs[0], bgs[0], wgs[1], bgs[1])


def _gdn_prep_body(x_ref, xp_ref, xn_ref, ab_ref, cw_ref, par_ref, qkv_ref, gb_ref):
    i, n = pl.program_id(1), pl.num_programs(1)
    blk = x_ref.shape[0]
    prev = jnp.where(i > 0, xp_ref[...], 0.0)
    nxt = jnp.where(i < n - 1, xn_ref[...], 0.0)
    ext = jnp.concatenate([prev, x_ref[...], nxt], axis=0)
    tot = blk + 2 * SUBLANES
    acc = None
    for t in range(CONV_W):
        sh = (CONV_W // 2 - t) % tot
        xs = ext if sh == 0 else pltpu.roll(ext, sh, 0)
        term = xs[SUBLANES:SUBLANES + blk, :] * cw_ref[t:t + 1, :]
        acc = term if acc is None else acc + term
    y = _silu(acc)
    nqk = 2 * GDN_H
    for h in range(3 * GDN_H):
        sl = slice(h * LANES, (h + 1) * LANES)
        yh = y[:, sl]
        if h < nqk:
            yh = yh * lax.rsqrt(jnp.sum(yh * yh, axis=-1, keepdims=True) + EPS)
            if h < GDN_H:
                yh = yh * GDN_DK ** -0.5
        qkv_ref[:, sl] = yh
    x = ab_ref[...]
    lane = lax.broadcasted_iota(i32, x.shape, 1)
    xa = x + par_ref[1:2, :]
    softplus = jnp.maximum(xa, 0.0) + jnp.log(1.0 + jnp.exp(-jnp.abs(xa)))
    gb_ref[...] = jnp.where(lane < nqk, par_ref[0:1, :] * softplus, _sigmoid(x))


def _gdn_prep(proj, conv_w, par, blk=256):
    b, l, _ = proj.shape
    blk = min(blk, l)
    nb = l // blk
    w = 3 * BW
    r8 = blk // SUBLANES
    last8 = l // SUBLANES - 1
    return pl.pallas_call(
        _gdn_prep_body, name="gdn_prep",
        out_shape=(jax.ShapeDtypeStruct((b, l, w), f32), jax.ShapeDtypeStruct((b, l, LANES), f32)),
        grid=(b, nb),
        in_specs=[pl.BlockSpec((None, blk, w), lambda bi, i: (bi, i, C_GDN_QKV // w)),
                  pl.BlockSpec((None, SUBLANES, w), lambda bi, i: (bi, jnp.maximum(i * r8 - 1, 0), C_GDN_QKV // w)),
                  pl.BlockSpec((None, SUBLANES, w), lambda bi, i: (bi, jnp.minimum((i + 1) * r8, last8), C_GDN_QKV // w)),
                  pl.BlockSpec((None, blk, LANES), lambda bi, i: (bi, i, C_GDN_AB // LANES)),
                  pl.BlockSpec((SUBLANES, w), lambda bi, i: (0, 0)),
                  pl.BlockSpec((SUBLANES, LANES), lambda bi, i: (0, 0))],
        out_specs=(pl.BlockSpec((None, blk, w), lambda bi, i: (bi, i, 0)),
                   pl.BlockSpec((None, blk, LANES), lambda bi, i: (bi, i, 0))),
        compiler_params=_cparams("parallel", "parallel"),
    )(proj, proj, proj, proj, conv_w, par)


def _unit_tri_inverses(lws, eye, bd16):
    lds = [jnp.where(bd16, lw, 0.0) for lw in lws]
    los = [lw - ld for lw, ld in zip(lws, lds)]
    ps = [eye - ld for ld in lds]
    pw = lds
    for _ in range(3):
        pw = [_mmb(x, x) for x in pw]
        ps = [p + _mmb(p, x) for p, x in zip(ps, pw)]
    ms = [_mmb(p, lo) for p, lo in zip(ps, los)]
    m2s = [_mmb(m, m) for m in ms]
    qs = [eye - m for m in ms]
    qs = [q + _mmb(q, m2) for q, m2 in zip(qs, m2s)]
    return [_mmb(q, p) for q, p in zip(qs, ps)]


def _gdn_par_body(q_ref, k_ref, v_ref, gb_ref, *out_refs):
    nch = q_ref.shape[0] // CHUNK
    r64 = lax.broadcasted_iota(i32, (CHUNK, CHUNK), 0)
    c64 = lax.broadcasted_iota(i32, (CHUNK, CHUNK), 1)
    eye = _ind(r64 == c64)
    bd16 = (r64 // 16) == (c64 // 16)
    incls = (r64 >= c64, r64 <= c64)
    stricts = (r64 > c64, r64 < c64)
    tris = tuple(_ind(m) for m in incls)

    def chunk_group(ci, carry):
        chains = []
        for j in range(GDN_GROUP):
            c = ci * GDN_GROUP + j
            rows = pl.ds(pl.multiple_of(c * CHUNK, CHUNK), CHUNK)
            gcols = gb_ref[rows, :]
            gams = tuple(_mm_hi(t, gcols) for t in tris)
            gam_ts = tuple(g.T for g in gams)
            for d in range(2):
                out_refs[6 * d + 5][c] = jnp.exp(gams[d][CHUNK - 1:CHUNK, :] if d == 0 else gams[d][0:1, :])
            for h in range(GDN_H):
                sl = slice(h * LANES, (h + 1) * LANES)
                qh, kh, vh = q_ref[rows, sl], k_ref[rows, sl], v_ref[rows, sl]
                raw = _nt(jnp.concatenate([qh, kh], axis=0).astype(bf16), kh.astype(bf16))
                for d in range(2):
                    lg = d * GDN_H + h
                    chains.append(dict(rows=rows, h=h, d=d, qh=qh, kh=kh, vh=vh, raw=raw,
                                       gcol=gams[d][:, lg:lg + 1], grow=gam_ts[d][lg:lg + 1, :],
                                       beta=gcols[:, 2 * GDN_H + lg:2 * GDN_H + lg + 1]))
        lws = []
        for ch in chains:
            d, rows, h = ch['d'], ch['rows'], ch['h']
            dec = jnp.where(incls[d], jnp.exp(jnp.where(incls[d], ch['gcol'] - ch['grow'], 0.0)), 0.0)
            out_refs[6 * d + 4][rows, h * CHUNK:(h + 1) * CHUNK] = (ch['raw'][:CHUNK] * dec).astype(bf16)
            lws.append(jnp.where(stricts[d], ch['raw'][CHUNK:] * dec * ch['beta'], 0.0))
        tinvs = _unit_tri_inverses(lws, eye, bd16)
        sols = []
        for ch, tinv in zip(chains, tinvs):
            ch['eg'] = jnp.exp(ch['gcol'])
            kb = ch['kh'] * ch['beta']
            sols.append(_mmb(tinv, jnp.concatenate([ch['vh'] * ch['beta'], kb * ch['eg']], axis=1)))
        for ch, sol in zip(chains, sols):
            d, rows = ch['d'], ch['rows']
            sl = slice(ch['h'] * LANES, (ch['h'] + 1) * LANES)
            u_ref, w_ref, qd_ref, kt_ref = out_refs[6 * d:6 * d + 4]
            gcol = ch['gcol']
            gtot = gcol[CHUNK - 1:CHUNK, :] if d == 0 else gcol[0:1, :]
            u_ref[rows, sl] = sol[:, :GDN_DV]
            w_ref[rows, sl] = sol[:, GDN_DV:].astype(bf16)
            qd_ref[rows, sl] = (ch['qh'] * ch['eg']).astype(bf16)
            kt_ref[rows, sl] = (ch['kh'] * jnp.exp(gtot - gcol)).astype(bf16)
        return carry

    lax.fori_loop(0, nch // GDN_GROUP, chunk_group, 0)


def _gdn_par(qkv, gb, blk=512):
    b, l, _ = qkv.shape
    blk = min(blk, l)
    nch = blk // CHUNK
    col = lambda j: pl.BlockSpec((None, blk, BW), lambda bi, i: (bi, i, j))
    wide = pl.BlockSpec((None, blk, BW), lambda bi, i: (bi, i, 0))
    one_dir_shapes = (jax.ShapeDtypeStruct((b, l, BW), f32), jax.ShapeDtypeStruct((b, l, BW), bf16),
                      jax.ShapeDtypeStruct((b, l, BW), bf16), jax.ShapeDtypeStruct((b, l, BW), bf16),
                      jax.ShapeDtypeStruct((b, l, GDN_H * CHUNK), bf16),
                      jax.ShapeDtypeStruct((b, l // CHUNK, 1, LANES), f32))
    one_dir_specs = (wide, wide, wide, wide,
                     pl.BlockSpec((None, blk, GDN_H * CHUNK), lambda bi, i: (bi, i, 0)),
                     pl.BlockSpec((None, nch, 1, LANES), lambda bi, i: (bi, i, 0, 0)))
    outs = pl.pallas_call(
        _gdn_par_body, name="gdn_par",
        out_shape=one_dir_shapes * 2,
        grid=(b, l // blk),
        in_specs=[col(0), col(1), col(2), pl.BlockSpec((None, blk, LANES), lambda bi, i: (bi, i, 0))],
        out_specs=one_dir_specs * 2,
        compiler_params=_cparams("parallel", "parallel"),
    )(qkv, qkv, qkv, gb)
    return outs[:6], outs[6:]


def _gdn_seq_body(*refs):
    ins = (refs[0:6], refs[6:12])
    o_refs = refs[12:14]
    s_ref = refs[14]
    nch = o_refs[0].shape[0] // CHUNK

    @pl.when(pl.program_id(1) == 0)
    def _():
        s_ref[...] = jnp.zeros_like(s_ref)

    def chunk(ci, carry):
        rows, cds = [], []
        for d in range(2):
            c = ci if d == 0 else nch - 1 - ci
            rows.append(pl.ds(pl.multiple_of(c * CHUNK, CHUNK), CHUNK))
            cds.append(ins[d][5][c])
        dh = [(d, h) for d in range(2) for h in range(GDN_H)]
        sl = lambda h: slice(h * LANES, (h + 1) * LANES)
        ss = [s_ref[d, sl(h), :] for d, h in dh]
        sbs = [s.astype(bf16) for s in ss]
        wss = [_mm(ins[d][1][rows[d], sl(h)], sb) for (d, h), sb in zip(dh, sbs)]
        qss = [_mm(ins[d][2][rows[d], sl(h)], sb) for (d, h), sb in zip(dh, sbs)]
        vns = [(ins[d][0][rows[d], sl(h)] - ws).astype(bf16) for (d, h), ws in zip(dh, wss)]
        avs = [_mm(ins[d][4][rows[d], h * CHUNK:(h + 1) * CHUNK], vn) for (d, h), vn in zip(dh, vns)]
        kvs = [_tn(ins[d][3][rows[d], sl(h)], vn) for (d, h), vn in zip(dh, vns)]
        for (d, h), s, kv in zip(dh, ss, kvs):
            lg = d * GDN_H + h
            s_ref[d, sl(h), :] = s * cds[d][:, lg:lg + 1] + kv
        for d in range(2):
            o_refs[d][rows[d], :] = jnp.concatenate(
                [qs + av for (dd, _), qs, av in zip(dh, qss, avs) if dd == d], axis=1)
        return carry

    lax.fori_loop(0, nch, chunk, 0)


def _gdn_seq(fwd_in, bwd_in, blk=512):
    b, l, _ = fwd_in[0].shape
    blk = min(blk, l)
    nb = l // blk
    nch = blk // CHUNK

    def specs(bidx):
        wide = pl.BlockSpec((None, blk, BW), lambda bi, i: (bi, bidx(i), 0))
        return [wide, wide, wide, wide,
                pl.BlockSpec((None, blk, GDN_H * CHUNK), lambda bi, i: (bi, bidx(i), 0)),
                pl.BlockSpec((None, nch, 1, LANES), lambda bi, i: (bi, bidx(i), 0, 0))]

    up, down = (lambda i: i), (lambda i: nb - 1 - i)
    out = jax.ShapeDtypeStruct((b, l, BW), f32)
    return pl.pallas_call(
        _gdn_seq_body, name="gdn_seq",
        out_shape=(out, out),
        grid=(b, nb),
        in_specs=specs(up) + specs(down),
        out_specs=(pl.BlockSpec((None, blk, BW), lambda bi, i: (bi, up(i), 0)),
                   pl.BlockSpec((None, blk, BW), lambda bi, i: (bi, down(i), 0))),
        scratch_shapes=[pltpu.VMEM((2, GDN_H * GDN_DK, GDN_DV), f32)],
        compiler_params=_cparams("parallel", "arbitrary"),
    )(*fwd_in, *bwd_in)


def _head_norm_gate(o, gain, z):
    outs = []
    for h in range(BW // LANES):
        oh = o[:, h * LANES:(h + 1) * LANES]
        outs.append(oh * lax.rsqrt(jnp.mean(oh * oh, axis=-1, keepdims=True) + EPS) * gain)
    return jnp.concatenate(outs, axis=1) * _silu(z)


def _merge_body(h_ref, gate_ref, ch_ref, u_ref, glaf_ref, glab_ref, r_ref, gdnf_ref, gdnb_ref, z_ref,
                dsk_ref, wglu_ref, gng_ref, dng_ref, wbr_ref, wout_ref, o_ref):
    y0 = _gelu_tanh(ch_ref[...] + dsk_ref[...] * u_ref[...])
    y_s5 = y0 * _sigmoid(_mm(y0.astype(bf16), wglu_ref[...]))
    y_gla = _head_norm_gate(glaf_ref[...] + glab_ref[...], gng_ref[...], r_ref[...])
    y_gdn = _head_norm_gate(gdnf_ref[...] + gdnb_ref[...], dng_ref[...], z_ref[...])
    merged = None
    for r, y in enumerate((y_s5, y_gla, y_gdn)):
        gate = _sigmoid(gate_ref[:, r * D_MODEL:(r + 1) * D_MODEL])
        term = gate * _mm(y.astype(bf16), wbr_ref[r])
        merged = term if merged is None else merged + term
    o_ref[...] = h_ref[...] + _mm(merged.astype(bf16), wout_ref[...])


def _merge(h2d, proj2d, ch, gla_f, gla_b, gdn_f, gdn_b, dsk, wglu, gng, dng, wbr, wout, tm=256):
    m = h2d.shape[0]
    row = lambda width, off=0: pl.BlockSpec((tm, width), lambda i: (i, off // width))
    full = lambda shape: pl.BlockSpec(shape, lambda i: (0,) * len(shape))
    return pl.pallas_call(
        _merge_body, name="merge",
        out_shape=jax.ShapeDtypeStruct((m, D_MODEL), f32),
        grid=(m // tm,),
        in_specs=[row(D_MODEL), row(3 * D_MODEL, C_GATE), row(BW), row(BW, C_S5), row(BW), row(BW),
                  row(BW, C_GLA_R), row(BW), row(BW), row(BW, C_GDN_Z),
                  full((1, BW)), full((BW, BW)), full((1, LANES)), full((1, LANES)),
                  full((3, BW, D_MODEL)), full((D_MODEL, D_MODEL))],
        out_specs=row(D_MODEL),
        compiler_params=_cparams("parallel"),
    )(h2d, proj2d, ch, proj2d, gla_f, gla_b, proj2d, gdn_f, gdn_b, proj2d, dsk, wglu, gng, dng, wbr, wout)


def _router_body(h_ref, g_ref, wr_ref, hn_ref, aff_ref):
    hn = _rms(h_ref[...], g_ref[...])
    hn_ref[...] = hn.astype(bf16)
    logits = lax.dot_general(wr_ref[...], hn, (((1,), (1,)), ((), ())), preferred_element_type=f32,
                             precision=HIGHEST)
    e = jnp.exp(logits - jnp.max(logits, axis=0, keepdims=True))
    aff_ref[...] = e / jnp.sum(e, axis=0, keepdims=True)


def _router(h2d, gain, wr_t, tm=512):
    m = h2d.shape[0]
    return pl.pallas_call(
        _router_body, name="router",
        out_shape=(jax.ShapeDtypeStruct((m, D_MODEL), bf16), jax.ShapeDtypeStruct((N_EXPERTS, m), f32)),
        grid=(m // tm,),
        in_specs=[pl.BlockSpec((tm, D_MODEL), lambda i: (i, 0)),
                  pl.BlockSpec((1, D_MODEL), lambda i: (0, 0)),
                  pl.BlockSpec((N_EXPERTS, D_MODEL), lambda i: (0, 0))],
        out_specs=(pl.BlockSpec((tm, D_MODEL), lambda i: (i, 0)), pl.BlockSpec((N_EXPERTS, tm), lambda i: (0, i))),
        compiler_params=_cparams("parallel"),
    )(h2d, gain, wr_t)


def _threshold_body(aff_ref, thr_ref, *, cap):
    keys = pltpu.bitcast(aff_ref[...], i32)

    def count(mask):
        return jnp.sum(jnp.where(mask, 1.0, 0.0), axis=1, keepdims=True).astype(i32)

    def bit(bi, t):
        cand = t | (1 << (30 - bi))
        return jnp.where(count(keys >= cand) >= cap, cand, t)

    t = lax.fori_loop(0, 31, bit, jnp.zeros((N_EXPERTS, 1), i32))
    budget = cap - count(keys > t)
    lane = lax.broadcasted_iota(i32, (N_EXPERTS, LANES), 1)
    thr_ref[...] = jnp.where(lane == 0, t, jnp.where(lane == 1, budget, 0))


def _threshold(aff_t, cap):
    n = aff_t.shape[1]
    return pl.pallas_call(
        functools.partial(_threshold_body, cap=cap), name="topc_threshold",
        out_shape=jax.ShapeDtypeStruct((N_EXPERTS, LANES), i32),
        in_specs=[pl.BlockSpec((N_EXPERTS, n), lambda: (0, 0))],
        out_specs=pl.BlockSpec((N_EXPERTS, LANES), lambda: (0, 0)),
        compiler_params=pltpu.CompilerParams(vmem_limit_bytes=VMEM_LIMIT_BYTES),
    )(aff_t)


def _slots_body(aff_ref, thr_ref, slot_ref, wts_ref, cnt_ref, run_ref):
    @pl.when(pl.program_id(0) == 0)
    def _():
        run_ref[...] = jnp.zeros_like(run_ref)

    tt = aff_ref.shape[1]
    aff = aff_ref[...]
    keys = pltpu.bitcast(aff, i32)
    t = thr_ref[:, 0:1]
    budget = thr_ref[:, 1:2]
    upper = _ind(lax.broadcasted_iota(i32, (tt, tt), 0) <= lax.broadcasted_iota(i32, (tt, tt), 1), bf16)
    eq = keys == t
    sel_run = run_ref[:, 0:1]
    tie_run = run_ref[:, 1:2]
    cs_eq = _mm(_ind(eq, bf16), upper).astype(i32)
    tie_rank = tie_run + cs_eq - 1
    sel = (keys > t) | (eq & (tie_rank < budget))
    cs_sel = _mm(_ind(sel, bf16), upper).astype(i32)
    slot_ref[...] = jnp.where(sel, sel_run + cs_sel - 1, -1)
    wts_ref[...] = jnp.where(sel, aff, 0.0)
    n_sel = cs_sel[:, tt - 1:tt]
    n_eq = cs_eq[:, tt - 1:tt]
    n_rows = ((n_sel + (SUBLANES - 1)) // SUBLANES) * SUBLANES
    cnt_ref[...] = jnp.broadcast_to(n_rows, cnt_ref.shape)
    lane = lax.broadcasted_iota(i32, run_ref.shape, 1)
    run_ref[...] = run_ref[...] + jnp.where(lane == 0, n_rows, jnp.where(lane == 1, n_eq, 0))


def _slots(aff_t, thr):
    n = aff_t.shape[1]
    nt = n // MOE_TT
    return pl.pallas_call(
        _slots_body, name="topc_slots",
        out_shape=(jax.ShapeDtypeStruct((N_EXPERTS, n), i32), jax.ShapeDtypeStruct((N_EXPERTS, n), f32),
                   jax.ShapeDtypeStruct((nt, N_EXPERTS, LANES), i32)),
        grid=(nt,),
        in_specs=[pl.BlockSpec((N_EXPERTS, MOE_TT), lambda i: (0, i)),
                  pl.BlockSpec((N_EXPERTS, LANES), lambda i: (0, 0))],
        out_specs=(pl.BlockSpec((N_EXPERTS, MOE_TT), lambda i: (0, i)),
                   pl.BlockSpec((N_EXPERTS, MOE_TT), lambda i: (0, i)),
                   pl.BlockSpec((None, N_EXPERTS, LANES), lambda i: (i, 0, 0))),
        scratch_shapes=[pltpu.VMEM((N_EXPERTS, LANES), i32)],
        compiler_params=_cparams("arbitrary"),
    )(aff_t, thr)


def _window_hits(slot_rows, starts, width):
    tt = slot_rows.shape[1]
    r = lax.broadcasted_iota(i32, (width, tt), 0)
    return [r == (slot_rows[e:e + 1, :] - starts[e]) for e in range(len(starts))]


def _dispatch_body(off_ref, x_ref, slot_ref, xe_hbm, buf_ref, xbuf_ref, sem, *, cap):
    i = pl.program_id(0)
    w = MOE_W
    x = x_ref[...]
    slot_rows = slot_ref[...]
    starts = [off_ref[e, i] for e in range(N_EXPERTS)]

    def window_copy(e, start, src):
        return pltpu.make_async_copy(src, xe_hbm.at[e, pl.ds(pl.multiple_of(start, SUBLANES), w)], sem.at[e])

    def window_wait(e):
        pltpu.make_async_copy(buf_ref.at[e], xe_hbm.at[e, pl.ds(0, w)], sem.at[e]).wait()

    @pl.when(i > 0)
    def _():
        for e in range(N_EXPERTS):
            window_wait(e)

    @pl.when(i == 0)
    def _():
        xbuf_ref[...] = jnp.zeros_like(xbuf_ref)
        tails = [pltpu.make_async_copy(xbuf_ref, xe_hbm.at[e, pl.ds(r0, w)], sem.at[N_EXPERTS])
                 for e in range(N_EXPERTS) for r0 in range(cap, xe_hbm.shape[1], w)]
        for cp in tails:
            cp.start()
        for cp in tails:
            cp.wait()

    hits = _window_hits(slot_rows, starts, w)
    lhs = jnp.concatenate([_ind(h, bf16) for h in hits], axis=0)
    rows = _mm(lhs, x).astype(bf16)
    for e in range(N_EXPERTS):
        buf_ref[e] = rows[e * w:(e + 1) * w, :]
        window_copy(e, starts[e], buf_ref.at[e]).start()

    for e in range(N_EXPERTS):
        n_rows = off_ref[e, i + 1] - starts[e]

        def extra(k, carry, e=e):
            start = starts[e] + k * w
            hit = _window_hits(slot_rows[e:e + 1, :], [start], w)[0]
            xbuf_ref[...] = _mm(_ind(hit, bf16), x).astype(bf16)
            cp = pltpu.make_async_copy(xbuf_ref, xe_hbm.at[e, pl.ds(pl.multiple_of(start, SUBLANES), w)],
                                       sem.at[N_EXPERTS])
            cp.start()
            cp.wait()
            return carry

        lax.fori_loop(1, (n_rows + (w - 1)) // w, extra, 0)

    @pl.when(i == pl.num_programs(0) - 1)
    def _():
        for e in range(N_EXPERTS):
            window_wait(e)


def _dispatch(hn, slot, off, rows_alloc, cap):
    n = hn.shape[0]
    grid_spec = pltpu.PrefetchScalarGridSpec(
        num_scalar_prefetch=1,
        grid=(n // MOE_TT,),
        in_specs=[pl.BlockSpec((MOE_TT, D_MODEL), lambda i, o: (i, 0)),
                  pl.BlockSpec((N_EXPERTS, MOE_TT), lambda i, o: (0, i))],
        out_specs=pl.BlockSpec(memory_space=pl.ANY),
        scratch_shapes=[pltpu.VMEM((N_EXPERTS, MOE_W, D_MODEL), bf16), pltpu.VMEM((MOE_W, D_MODEL), bf16),
                        pltpu.SemaphoreType.DMA((N_EXPERTS + 1,))])
    return pl.pallas_call(
        functools.partial(_dispatch_body, cap=cap), name="dispatch",
        out_shape=jax.ShapeDtypeStruct((N_EXPERTS, rows_alloc, D_MODEL), bf16),
        grid_spec=grid_spec,
        compiler_params=_cparams("arbitrary"),
    )(off, hn, slot)


def _experts_body(tot_ref, x_ref, wg_ref, wu_ref, wd_ref, y_ref):
    e, j = pl.program_id(0), pl.program_id(1)
    fb = x_ref.shape[0]
    n_valid = tot_ref[e] - j * fb

    def ffn(x):
        hid = (_silu(_mm(x, wg_ref[...])) * _mm(x, wu_ref[...])).astype(bf16)
        y_ref[...] = _mm(hid, wd_ref[...]).astype(y_ref.dtype)

    @pl.when(n_valid >= fb)
    def _():
        ffn(x_ref[...])

    @pl.when((n_valid > 0) & (n_valid < fb))
    def _():
        row = lax.broadcasted_iota(i32, (fb, 1), 0)
        ffn(jnp.where(row < n_valid, x_ref[...], jnp.zeros((), bf16)))

    @pl.when(n_valid <= 0)
    def _():
        y_ref[...] = jnp.zeros_like(y_ref)


def _experts(xe, total, wg, wu, wd, rows):
    grid_spec = pltpu.PrefetchScalarGridSpec(
        num_scalar_prefetch=1,
        grid=(N_EXPERTS, rows // MOE_FB),
        in_specs=[pl.BlockSpec((None, MOE_FB, D_MODEL), lambda e, j, t: (e, j, 0)),
                  pl.BlockSpec((None, D_MODEL, EXPERT_FF), lambda e, j, t: (e, 0, 0)),
                  pl.BlockSpec((None, D_MODEL, EXPERT_FF), lambda e, j, t: (e, 0, 0)),
                  pl.BlockSpec((None, EXPERT_FF, D_MODEL), lambda e, j, t: (e, 0, 0))],
        out_specs=pl.BlockSpec((None, MOE_FB, D_MODEL), lambda e, j, t: (e, j, 0)))
    return pl.pallas_call(
        _experts_body, name="experts",
        out_shape=jax.ShapeDtypeStruct((N_EXPERTS, rows, D_MODEL), bf16),
        grid_spec=grid_spec,
        compiler_params=_cparams("parallel", "arbitrary"),
    )(total, xe, wg, wu, wd)


def _combine_body(off_ref, h_ref, slot_ref, wts_ref, p_ref, g_ref, wpg_ref, wpp_ref, ye_hbm, o_ref,
                  win_ref, xwin_ref, acc_ref, sem, *, rows):
    i, n = pl.program_id(0), pl.num_programs(0)
    w = MOE_W
    tt = h_ref.shape[0]

    def wstart(e, tile, k=0):
        return pl.multiple_of(jnp.minimum(off_ref[e, tile] + k * w, rows - w), SUBLANES)

    def window_copy(e, tile, par):
        return pltpu.make_async_copy(ye_hbm.at[e, pl.ds(wstart(e, tile), w)], win_ref.at[par, e], sem.at[par, e])

    @pl.when(i == 0)
    def _():
        for e in range(N_EXPERTS):
            window_copy(e, 0, 0).start()

    @pl.when(i + 1 < n)
    def _():
        for e in range(N_EXPERTS):
            window_copy(e, i + 1, (i + 1) % 2).start()

    par = i % 2
    slot_rows = slot_ref[...]
    gates = jnp.transpose(wts_ref[...])
    acc_ref[...] = h_ref[...]
    r = lax.broadcasted_iota(i32, (w, tt), 0)
    for e in range(N_EXPERTS):
        window_copy(e, i, par).wait()
        s_row = slot_rows[e:e + 1, :]
        lo = off_ref[e, i]
        hit = (r == s_row - wstart(e, i)) & (s_row < lo + w)
        acc_ref[...] += gates[:, e:e + 1] * _tn(_ind(hit, bf16), win_ref[par, e])

    for e in range(N_EXPERTS):
        lo = off_ref[e, i]
        n_rows = off_ref[e, i + 1] - lo

        def extra(k, carry, e=e, lo=lo):
            cp = pltpu.make_async_copy(ye_hbm.at[e, pl.ds(wstart(e, i, k), w)], xwin_ref, sem.at[2, 0])
            cp.start()
            cp.wait()
            s_row = slot_rows[e:e + 1, :]
            hit = (r == s_row - wstart(e, i, k)) & (s_row >= lo + k * w) & (s_row < lo + (k + 1) * w)
            acc_ref[...] += gates[:, e:e + 1] * _tn(_ind(hit, bf16), xwin_ref[...])
            return carry

        lax.fori_loop(1, (n_rows + (w - 1)) // w, extra, 0)

    h2 = acc_ref[...]
    gate = _sigmoid(_mm(_rms(h2, g_ref[...]).astype(bf16), wpg_ref[...]))
    o_ref[...] = h2 + gate * _mm(p_ref[...].astype(bf16), wpp_ref[...])


def _combine(h2d, slot, wts, ye, p2d, off, g_ple, wpg, wpp):
    m = h2d.shape[0]
    rows = ye.shape[1]
    full = lambda shape: pl.BlockSpec(shape, lambda i, o: (0,) * len(shape))
    grid_spec = pltpu.PrefetchScalarGridSpec(
        num_scalar_prefetch=1,
        grid=(m // MOE_TT,),
        in_specs=[pl.BlockSpec((MOE_TT, D_MODEL), lambda i, o: (i, 0)),
                  pl.BlockSpec((N_EXPERTS, MOE_TT), lambda i, o: (0, i)),
                  pl.BlockSpec((N_EXPERTS, MOE_TT), lambda i, o: (0, i)),
                  pl.BlockSpec((MOE_TT, PLE_DIM), lambda i, o: (i, 0)),
                  full((1, D_MODEL)), full((D_MODEL, D_MODEL)), full((PLE_DIM, D_MODEL)),
                  pl.BlockSpec(memory_space=pl.ANY)],
        out_specs=pl.BlockSpec((MOE_TT, D_MODEL), lambda i, o: (i, 0)),
        scratch_shapes=[pltpu.VMEM((2, N_EXPERTS, MOE_W, D_MODEL), bf16), pltpu.VMEM((MOE_W, D_MODEL), bf16),
                        pltpu.VMEM((MOE_TT, D_MODEL), f32), pltpu.SemaphoreType.DMA((3, N_EXPERTS))])
    return pl.pallas_call(
        functools.partial(_combine_body, rows=rows), name="combine_ple",
        out_shape=jax.ShapeDtypeStruct((m, D_MODEL), f32),
        grid_spec=grid_spec,
        compiler_params=_cparams("arbitrary"),
    )(off, h2d, slot, wts, p2d, g_ple, wpg, wpp, ye)


def _moe_ple(h2d, p2d, g_ffn, wr_t, wg, wu, wd, g_ple, wpg, wpp):
    n = h2d.shape[0]
    nt = n // MOE_TT
    cap = max(1, EC_FACTOR * n // N_EXPERTS)
    rows = -(-(cap + SUBLANES * nt) // MOE_FB) * MOE_FB
    hn, aff_t = _router(h2d, g_ffn, wr_t)
    thr = _threshold(aff_t, cap)
    slot, wts, cnt = _slots(aff_t, thr)
    off = jnp.concatenate([jnp.zeros((1, N_EXPERTS), i32), jnp.cumsum(cnt[:, :, 0], axis=0, dtype=i32)], axis=0)
    off = jnp.transpose(off)
    xe = _dispatch(hn, slot, off, rows + MOE_W, cap)
    ye = _experts(xe, off[:, nt], wg, wu, wd, rows)
    return _combine(h2d, slot, wts, ye, p2d, off, g_ple, wpg, wpp)


def _final_norm_body(x_ref, g_ref, o_ref):
    o_ref[...] = _rms(x_ref[...], g_ref[...])


def _final_norm(h2d, gain, tm=1024):
    m = h2d.shape[0]
    return pl.pallas_call(
        _final_norm_body, name="final_norm",
        out_shape=jax.ShapeDtypeStruct((m, D_MODEL), f32),
        grid=(m // tm,),
        in_specs=[pl.BlockSpec((tm, D_MODEL), lambda i: (i, 0)), pl.BlockSpec((1, D_MODEL), lambda i: (0, 0))],
        out_specs=pl.BlockSpec((tm, D_MODEL), lambda i: (i, 0)),
        compiler_params=_cparams("parallel"),
    )(h2d, gain)


def _layer_weights(w):
    lw = {}
    lw['g_mix'] = w['norm_mix'].reshape(1, D_MODEL)
    lw['w_in'] = _reorder_w_in(w['w_in'])
    lw['s5'] = _s5_weights(w['s5_B_re'], w['s5_B_im'], w['s5_C_re'], w['s5_C_im'],
                           w['s5_lam_re'], w['s5_lam_im'], w['s5_log_dt'])
    wgate = w['gla_w_gate']
    lw['gla_wg'] = tuple(jnp.zeros((LANES, GLA_H * GLA_DK), f32).at[d * GLA_RANK:(d + 1) * GLA_RANK].set(wgate[d])
                         for d in range(2))
    lw['gla_bg'] = tuple(w['gla_b_gate'][d].reshape(1, -1) for d in range(2))
    lw['conv_w'] = jnp.pad(jnp.transpose(w['gdn_conv']), ((0, SUBLANES - CONV_W), (0, 0)))
    neg_a = -jnp.exp(w['gdn_A_log']).reshape(-1)
    par = jnp.zeros((SUBLANES, LANES), f32).at[0, :2 * GDN_H].set(neg_a).at[1, :2 * GDN_H].set(
        w['gdn_dt_bias'].reshape(-1))
    lw['gdn_par'] = par
    lw['dsk'] = w['s5_D'].reshape(1, BW)
    lw['wglu'] = w['s5_w_glu'].astype(bf16)
    lw['gng'] = w['gla_norm'].reshape(1, LANES)
    lw['dng'] = w['gdn_norm'].reshape(1, LANES)
    lw['wbr'] = w['w_branch'].astype(bf16)
    lw['wout'] = w['w_out'].astype(bf16)
    lw['g_ffn'] = w['norm_ffn'].reshape(1, D_MODEL)
    lw['wr_t'] = jnp.transpose(w['w_router'])
    lw['wg'] = w['w_exp_gate'].astype(bf16)
    lw['wu'] = w['w_exp_up'].astype(bf16)
    lw['wd'] = w['w_exp_down'].astype(bf16)
    lw['g_ple'] = w['norm_ple'].reshape(1, D_MODEL)
    lw['wpg'] = w['w_ple_gate'].astype(bf16)
    lw['wpp'] = w['w_ple_proj'].astype(bf16)
    return lw


def _mixers(h, lw):
    b, l, _ = h.shape
    h2d = h.reshape(b * l, D_MODEL)
    proj2d = _inproj(h2d, lw['g_mix'], lw['w_in'])
    proj = proj2d.reshape(b, l, D_INP)
    ch = _s5_mixer(proj, lw['s5'])
    gla_f, gla_b = _gla(proj, lw['gla_wg'], lw['gla_bg'])
    qkv, gb = _gdn_prep(proj, lw['conv_w'], lw['gdn_par'])
    gdn_f, gdn_b = _gdn_seq(*_gdn_par(qkv, gb))
    flat = lambda a: a.reshape(b * l, a.shape[-1])
    return _merge(h2d, proj2d, flat(ch), flat(gla_f), flat(gla_b), flat(gdn_f), flat(gdn_b),
                  lw['dsk'], lw['wglu'], lw['gng'], lw['dng'], lw['wbr'], lw['wout'])


def _layer(h, p_i, lw):
    b, l, _ = h.shape
    h1 = _mixers(h, lw)
    h3 = _moe_ple(h1, p_i.reshape(b * l, PLE_DIM), lw['g_ffn'], lw['wr_t'], lw['wg'], lw['wu'], lw['wd'],
                  lw['g_ple'], lw['wpg'], lw['wpp'])
    return h3.reshape(b, l, D_MODEL)


def kernel(x_prompt, x_sample, p_prompt, p_sample, norm_mix, w_in, s5_B_re, s5_B_im, s5_C_re, s5_C_im, s5_D, s5_lam_re, s5_lam_im, s5_log_dt, s5_w_glu, gla_w_gate, gla_b_gate, gla_norm, gdn_conv, gdn_A_log, gdn_dt_bias, gdn_norm, w_branch, w_out, norm_ffn, w_router, w_exp_gate, w_exp_up, w_exp_down, norm_ple, w_ple_gate, w_ple_proj, norm_final):
    weights = dict(norm_mix=norm_mix, w_in=w_in, s5_B_re=s5_B_re, s5_B_im=s5_B_im, s5_C_re=s5_C_re, s5_C_im=s5_C_im,
                   s5_D=s5_D, s5_lam_re=s5_lam_re, s5_lam_im=s5_lam_im, s5_log_dt=s5_log_dt, s5_w_glu=s5_w_glu,
                   gla_w_gate=gla_w_gate, gla_b_gate=gla_b_gate, gla_norm=gla_norm, gdn_conv=gdn_conv,
                   gdn_A_log=gdn_A_log, gdn_dt_bias=gdn_dt_bias, gdn_norm=gdn_norm, w_branch=w_branch, w_out=w_out,
                   norm_ffn=norm_ffn, w_router=w_router, w_exp_gate=w_exp_gate, w_exp_up=w_exp_up,
                   w_exp_down=w_exp_down, norm_ple=norm_ple, w_ple_gate=w_ple_gate, w_ple_proj=w_ple_proj)

    def body(carry, xs):
        hp, hs = carry
        w_i, pp, ps = xs
        lw = _layer_weights(w_i)
        return (_layer(hp, pp, lw), _layer(hs, ps, lw)), None

    (hp, hs), _ = lax.scan(body, (x_prompt.astype(f32), x_sample.astype(f32)), (weights, p_prompt, p_sample))
    g_fin = norm_final.reshape(1, D_MODEL)
    yp = _final_norm(hp.reshape(-1, D_MODEL), g_fin).reshape(x_prompt.shape).astype(x_prompt.dtype)
    ys = _final_norm(hs.reshape(-1, D_MODEL), g_fin).reshape(x_sample.shape).astype(x_sample.dtype)
    return (yp, ys)
```

```python
import functools
import math

import jax
import jax.numpy as jnp
import numpy as np
from jax import lax
from jax.experimental import pallas as pl
from jax.experimental.pallas import tpu as pltpu

f32 = jnp.float32
bf16 = jnp.bfloat16
i32 = jnp.int32
HIGHEST = lax.Precision.HIGHEST

D_MODEL = 1024
DEPTH = 4
PLE_DIM = 256
BW = 512
EPS = 1e-6
CHUNK = 64
S5_GROUPS, S5_GC, S5_STATE = 32, 16, 64
GLA_H, GLA_DK, GLA_DV, GLA_RANK, GLA_TAU = 4, 64, 128, 16, 16.0
GDN_H, GDN_DK, GDN_DV, CONV_W = 4, 128, 128, 5
N_EXPERTS, EXPERT_FF, EC_FACTOR = 16, 2048, 2

LANES = 128
SUBLANES = 8
VMEM_LIMIT_BYTES = 56 * 1024 * 1024

A_GATE, A_GDN_QKV, A_GDN_Z, A_GLA_V, A_GLA_R, A_GLA_Q, A_GLA_K = 0, 3072, 4608, 5120, 5632, 6144, 6400
D_A = 6656
B_S5, B_GLA_LR, B_GDN_AB = 0, 512, 640
D_B = 768
D_INP = D_A + D_B

S5_T = 8
S5_LB = BW // LANES
S5_SW = 8 * S5_STATE * 2
S5_SCAN_CB = 2

GDN_GROUP = 4
GLA_GROUP = 4

MOE_TT = 512
MOE_W = 96
MOE_PAD = 128
MOE_FB = 512


def _cparams(*sem):
    return pltpu.CompilerParams(dimension_semantics=sem, vmem_limit_bytes=VMEM_LIMIT_BYTES)


def _nt(a, b):
    return lax.dot_general(a, b, (((1,), (1,)), ((), ())), preferred_element_type=f32)


def _tn(a, b):
    return lax.dot_general(a, b, (((0,), (0,)), ((), ())), preferred_element_type=f32)


def _mm(a, b):
    return jnp.dot(a, b, preferred_element_type=f32)


def _mm_hi(a, b):
    return jnp.dot(a, b, preferred_element_type=f32, precision=HIGHEST)


def _mmb(a, b):
    return _mm(a.astype(bf16), b.astype(bf16))


def _ind(mask, dtype=f32):
    return jnp.where(mask, 1.0, 0.0).astype(dtype)


def _sigmoid(x):
    return 0.5 * jnp.tanh(0.5 * x) + 0.5


def _silu(x):
    return x * _sigmoid(x)


def _gelu_tanh(x):
    return 0.5 * x * (1.0 + jnp.tanh(math.sqrt(2.0 / math.pi) * (x + 0.044715 * (x * x * x))))


def _rms(x, g):
    return x * lax.rsqrt(jnp.mean(x * x, axis=-1, keepdims=True) + EPS) * g


def _inproj_body(x_ref, g_ref, w_ref, oa_ref, ob_ref):
    xn = _rms(x_ref[...], g_ref[...]).astype(bf16)
    oa_ref[...] = _mm(xn, w_ref[:, :D_A]).astype(bf16)
    ob_ref[...] = _mm(xn, w_ref[:, D_A:])


def _inproj(x2d, gain, w_p, tm=256):
    m = x2d.shape[0]
    return pl.pallas_call(
        _inproj_body, name="inproj",
        out_shape=(jax.ShapeDtypeStruct((m, D_A), bf16), jax.ShapeDtypeStruct((m, D_B), f32)),
        grid=(m // tm,),
        in_specs=[pl.BlockSpec((tm, D_MODEL), lambda i: (i, 0)),
                  pl.BlockSpec((1, D_MODEL), lambda i: (0, 0)),
                  pl.BlockSpec((D_MODEL, D_INP), lambda i: (0, 0), pipeline_mode=pl.Buffered(1))],
        out_specs=(pl.BlockSpec((tm, D_A), lambda i: (i, 0)), pl.BlockSpec((tm, D_B), lambda i: (i, 0))),
        compiler_params=_cparams("parallel"),
    )(x2d, gain, w_p)


def _reorder_w_in(w_in):
    o = np.cumsum((0, 512, 256, 256, 512, 512, 32, 512, 512, 512, 512, 8, 8, 3072))
    seg = lambda k: w_in[:, o[k]:o[k + 1]]
    zpad = lambda a: jnp.pad(a, ((0, 0), (0, LANES - a.shape[1])))
    parts = [seg(12), seg(6), seg(7), seg(8), seg(9), seg(3), seg(4), seg(1), seg(2),
             seg(0), zpad(seg(5)), zpad(jnp.concatenate([seg(10), seg(11)], axis=1))]
    return jnp.concatenate(parts, axis=1).astype(bf16)


def _s5_weights(b_re, b_im, c_re, c_im, lam_re, lam_im, log_dt):
    T = S5_T
    dt = jnp.exp(log_dt)[:, :, None]
    lr, li = lam_re, lam_im
    mag = jnp.exp(lr * dt)
    ab_re, ab_im = mag * jnp.cos(li * dt), mag * jnp.sin(li * dt)
    den = lr * lr + li * li
    num_re = ab_re - 1.0
    coef_re = (num_re * lr + ab_im * li) / den
    coef_im = (ab_im * lr - num_re * li) / den
    xb_re = coef_re[..., None] * b_re[None] - coef_im[..., None] * b_im[None]
    xb_im = coef_re[..., None] * b_im[None] + coef_im[..., None] * b_re[None]

    def powers(taus):
        tau = jnp.asarray(taus, lr.dtype)
        pm = jnp.exp((lr * dt)[..., None] * tau)
        ang = (li * dt)[..., None] * tau
        return pm * jnp.cos(ang), pm * jnp.sin(ang)

    p_re, p_im = powers(np.arange(T + 1))
    cp_re = c_re[None, :, :, :, None] * p_re[:, :, None] - c_im[None, :, :, :, None] * p_im[:, :, None]
    cp_im = c_re[None, :, :, :, None] * p_im[:, :, None] + c_im[None, :, :, :, None] * p_re[:, :, None]
    kern = (jnp.einsum('dgknt,dgnc->dgtkc', cp_re, xb_re, precision=HIGHEST)
            - jnp.einsum('dgknt,dgnc->dgtkc', cp_im, xb_im, precision=HIGHEST))
    s_idx = np.arange(T)[:, None]
    t_idx = np.arange(T)[None, :]
    lag_f = np.clip(t_idx - s_idx, 0, T)
    lag_b = np.clip(s_idx - t_idx, 0, T)
    m_f = jnp.asarray((t_idx >= s_idx), kern.dtype)[None, :, :, None, None]
    m_b = jnp.asarray((s_idx >= t_idx), kern.dtype)[None, :, :, None, None]
    a_g = kern[0][:, lag_f] * m_f + kern[1][:, lag_b] * m_b
    dtype = kern.dtype
    wide = T * LANES
    col = np.arange(wide)
    row_grp_lane = lax.broadcasted_iota(i32, (wide, wide), 0) // S5_GC % 8
    col_grp_lane = lax.broadcasted_iota(i32, (wide, wide), 1) // S5_GC % 8
    st_idx = lambda ax: (2 * (lax.broadcasted_iota(i32, (wide, wide), ax) // (2 * LANES))
                         + lax.broadcasted_iota(i32, (wide, wide), ax) % LANES // S5_STATE)
    exp_tk = jnp.asarray((np.arange(LANES)[:, None] // S5_GC == col[None, :] // LANES)
                         & (np.arange(LANES)[:, None] % S5_GC == col[None, :] % S5_GC), dtype)
    exp_pn = jnp.asarray((np.arange(LANES)[:, None] // S5_STATE == col[None, :] % (2 * LANES) // LANES)
                         & (np.arange(LANES)[:, None] % S5_STATE == col[None, :] % S5_STATE), dtype)

    def expand(compact, expansion, mask):
        return jnp.where(mask, jnp.einsum('jrm,mn->jrn', compact, expansion, precision=HIGHEST), 0.0)

    a_c = jnp.transpose(a_g.reshape(S5_LB, 8, T, T, S5_GC, S5_GC), (0, 2, 1, 5, 3, 4))
    a_blk = expand(a_c.reshape(S5_LB, wide, LANES), exp_tk, row_grp_lane == col_grp_lane)

    def state_in(d, taus):
        e_re = p_re[d][:, :, taus][..., None] * xb_re[d][:, :, None, :] - p_im[d][:, :, taus][..., None] * xb_im[d][:, :, None, :]
        e_im = p_re[d][:, :, taus][..., None] * xb_im[d][:, :, None, :] + p_im[d][:, :, taus][..., None] * xb_re[d][:, :, None, :]
        e = jnp.stack([e_re, e_im], axis=0)
        e = e.reshape(2, S5_LB, 8, S5_STATE, T, S5_GC)
        e = jnp.transpose(e, (1, 4, 2, 5, 0, 3))
        return expand(e.reshape(S5_LB, wide, LANES), exp_pn, row_grp_lane == st_idx(1))

    m_f_w = state_in(0, np.arange(T - 1, -1, -1))
    m_b_w = state_in(1, np.arange(T))

    def state_out(d, taus):
        r = cp_re[d][..., taus]
        im = -cp_im[d][..., taus]
        w = jnp.stack([r, im], axis=0).reshape(2, S5_LB, 4, 2, S5_GC, S5_STATE, T)
        w = jnp.transpose(w, (1, 2, 0, 3, 5, 6, 4))
        return expand(w.reshape(S5_LB, S5_SW, LANES), exp_tk, st_idx(0) == col_grp_lane)

    n_f_w = state_out(0, np.arange(1, T + 1))
    n_b_w = state_out(1, np.arange(T, 0, -1))

    q_re, q_im = powers(T * np.arange(8))

    def table(arr_re, arr_im, d, order):
        t = jnp.stack([arr_re[d][..., order], arr_im[d][..., order]], axis=0)
        t = t.reshape(2, 16, 2, S5_STATE, len(order))
        return jnp.transpose(t, (4, 1, 0, 2, 3)).reshape(len(order), 16 * 2 * LANES)

    asc = np.arange(8)
    dbl = np.array([1, 2, 4, 0, 0, 0, 0, 0])
    tabs = (table(q_re, q_im, 0, asc), table(q_re, q_im, 0, dbl),
            table(q_re, q_im, 1, asc[::-1]), table(q_re, q_im, 1, dbl))
    return (a_blk.astype(bf16), m_f_w.astype(bf16), m_b_w.astype(bf16), n_f_w.astype(bf16), n_b_w.astype(bf16),
            tuple(t.astype(f32) for t in tabs))


def _s5_load_chunks(u_ref):
    rows = u_ref.shape[0] // S5_T
    parts = [u_ref[pl.ds(s, rows, stride=S5_T), :] for s in range(S5_T)]
    return jnp.concatenate(parts, axis=1).astype(bf16)


def _s5_in_body(u_ref, mf_ref, mb_ref, xf_ref, xb_ref):
    lhs = _s5_load_chunks(u_ref)
    xf_ref[...] = _mm(lhs, mf_ref[...])
    xb_ref[...] = _mm(lhs, mb_ref[...])


def _s5_in(proj, m_f_w, m_b_w, tl):
    b, l, _ = proj.shape
    nc = l // S5_T
    out = jax.ShapeDtypeStruct((b, nc, S5_LB * S5_SW), f32)
    wspec = pl.BlockSpec((None, S5_T * LANES, S5_SW), lambda bi, j, t: (j, 0, 0))
    ospec = pl.BlockSpec((None, tl // S5_T, S5_SW), lambda bi, j, t: (bi, t, j))
    return pl.pallas_call(
        _s5_in_body, name="s5_in",
        out_shape=(out, out),
        grid=(b, S5_LB, l // tl),
        in_specs=[pl.BlockSpec((None, tl, LANES), lambda bi, j, t: (bi, t, B_S5 // LANES + j)), wspec, wspec],
        out_specs=(ospec, ospec),
        compiler_params=_cparams("parallel", "parallel", "parallel"),
    )(proj, m_f_w, m_b_w)


def _s5_scan_body(xf_ref, xb_ref, cf_ref, hf_ref, cb_ref, hb_ref, sf_ref, sb_ref):
    n8 = xf_ref.shape[0] // SUBLANES
    ncb = xf_ref.shape[1] // (2 * LANES)
    rows = lax.broadcasted_iota(i32, (SUBLANES, LANES), 0)

    def shifted(x, d, fwd):
        if fwd:
            return jnp.where(rows >= d, pltpu.roll(x, d, 0), 0.0)
        return jnp.where(rows < SUBLANES - d, pltpu.roll(x, SUBLANES - d, 0), 0.0)

    def local_scan(xr, xi, h_ref, re, im, fwd):
        er, ei = shifted(xr, 1, fwd), shifted(xi, 1, fwd)
        for k, d in enumerate((1, 2, 4)):
            ar, ai = h_ref[k:k + 1, re], h_ref[k:k + 1, im]
            sr, si = shifted(er, d, fwd), shifted(ei, d, fwd)
            er, ei = er + ar * sr - ai * si, ei + ar * si + ai * sr
        return er, ei

    def tile(x_ref, c_ref, h_ref, o_ref, i, c, sr, si, fwd):
        re = slice(2 * c * LANES, (2 * c + 1) * LANES)
        im = slice((2 * c + 1) * LANES, (2 * c + 2) * LANES)
        r0 = pl.multiple_of(i * SUBLANES, SUBLANES)
        xr, xi = x_ref[pl.ds(r0, SUBLANES), re], x_ref[pl.ds(r0, SUBLANES), im]
        er, ei = local_scan(xr, xi, h_ref, re, im, fwd)
        cr, ci = c_ref[:, re], c_ref[:, im]
        outr = er + cr * sr - ci * si
        outi = ei + cr * si + ci * sr
        o_ref[pl.ds(r0, SUBLANES), re] = outr
        o_ref[pl.ds(r0, SUBLANES), im] = outi
        e = SUBLANES - 1 if fwd else 0
        ar, ai = h_ref[0:1, re], h_ref[0:1, im]
        nr = ar * outr[e:e + 1] - ai * outi[e:e + 1] + xr[e:e + 1]
        ni = ar * outi[e:e + 1] + ai * outr[e:e + 1] + xi[e:e + 1]
        return nr, ni

    def step(i, carry):
        out = []
        for c in range(ncb):
            fr, fi, br, bi = carry[4 * c:4 * c + 4]
            fr, fi = tile(xf_ref, cf_ref, hf_ref, sf_ref, i, c, fr, fi, True)
            br, bi = tile(xb_ref, cb_ref, hb_ref, sb_ref, n8 - 1 - i, c, br, bi, False)
            out += [fr, fi, br, bi]
        return tuple(out)

    z = jnp.zeros((1, LANES), f32)
    lax.fori_loop(0, n8, step, (z,) * (4 * ncb))


def _s5_scan(xf, xb, tabs):
    b, nc, w = xf.shape
    bw = S5_SCAN_CB * 2 * LANES
    xspec = pl.BlockSpec((None, nc, bw), lambda bi, c: (bi, 0, c))
    tspec = pl.BlockSpec((SUBLANES, bw), lambda bi, c: (0, c))
    out = jax.ShapeDtypeStruct((b, nc, w), f32)
    return pl.pallas_call(
        _s5_scan_body, name="s5_scan",
        out_shape=(out, out),
        grid=(b, w // bw),
        in_specs=[xspec, xspec, tspec, tspec, tspec, tspec],
        out_specs=(xspec, xspec),
        compiler_params=_cparams("parallel", "parallel"),
    )(xf, xb, *tabs)


def _s5_out_body(u_ref, sf_ref, sb_ref, a_ref, nf_ref, nb_ref, y_ref):
    lhs = _s5_load_chunks(u_ref)
    y = (_mm(lhs, a_ref[...]) + _mm(sf_ref[...].astype(bf16), nf_ref[...])
         + _mm(sb_ref[...].astype(bf16), nb_ref[...]))
    rows = y.shape[0]
    for t in range(S5_T):
        y_ref[pl.ds(t, rows, stride=S5_T), :] = y[:, t * LANES:(t + 1) * LANES]


def _s5_out(proj, sf, sb, a_blk, n_f_w, n_b_w, tl):
    b, l, _ = proj.shape
    sspec = pl.BlockSpec((None, tl // S5_T, S5_SW), lambda bi, j, t: (bi, t, j))
    return pl.pallas_call(
        _s5_out_body, name="s5_out",
        out_shape=jax.ShapeDtypeStruct((b, l, BW), f32),
        grid=(b, S5_LB, l // tl),
        in_specs=[pl.BlockSpec((None, tl, LANES), lambda bi, j, t: (bi, t, B_S5 // LANES + j)), sspec, sspec,
                  pl.BlockSpec((None, S5_T * LANES, S5_T * LANES), lambda bi, j, t: (j, 0, 0)),
                  pl.BlockSpec((None, S5_SW, S5_T * LANES), lambda bi, j, t: (j, 0, 0)),
                  pl.BlockSpec((None, S5_SW, S5_T * LANES), lambda bi, j, t: (j, 0, 0))],
        out_specs=pl.BlockSpec((None, tl, LANES), lambda bi, j, t: (bi, t, j)),
        compiler_params=_cparams("parallel", "parallel", "parallel"),
    )(proj, sf, sb, a_blk, n_f_w, n_b_w)


def _s5_mixer(proj, s5w, tl=2048):
    a_blk, m_f_w, m_b_w, n_f_w, n_b_w, tabs = s5w
    tl = min(tl, proj.shape[1])
    xf, xb = _s5_in(proj, m_f_w, m_b_w, tl)
    sf, sb = _s5_scan(xf, xb, tabs)
    return _s5_out(proj, sf, sb, a_blk, n_f_w, n_b_w, tl)


def _gla_body(qf_ref, kf_ref, vf_ref, lrf_ref, qb_ref, kb_ref, vb_ref, lrb_ref, wgf_ref, bgf_ref, wgb_ref, bgb_ref,
              of_ref, ob_ref, st_ref):
    ins = ((qf_ref, kf_ref, vf_ref, lrf_ref, wgf_ref, bgf_ref, of_ref),
           (qb_ref, kb_ref, vb_ref, lrb_ref, wgb_ref, bgb_ref, ob_ref))
    nch = qf_ref.shape[0] // CHUNK
    hk = GLA_H * GLA_DK
    hv = GLA_H * GLA_DV

    @pl.when(pl.program_id(1) == 0)
    def _():
        st_ref[...] = jnp.zeros_like(st_ref)

    r64 = lax.broadcasted_iota(i32, (CHUNK, CHUNK), 0)
    c64 = lax.broadcasted_iota(i32, (CHUNK, CHUNK), 1)
    tris = (_ind(r64 >= c64), _ind(r64 <= c64))
    rr = lax.broadcasted_iota(i32, (CHUNK, hk), 0)
    cc = lax.broadcasted_iota(i32, (CHUNK, hk), 1) % CHUNK
    causals = (rr >= cc, rr <= cc)
    kmask = (lax.broadcasted_iota(i32, (hk, hk), 0) // CHUNK
             == lax.broadcasted_iota(i32, (hk, hk), 1) // GLA_DK)
    vmask = (lax.broadcasted_iota(i32, (hk, hv), 0) // CHUNK
             == lax.broadcasted_iota(i32, (hk, hv), 1) // GLA_DV)
    smask = (lax.broadcasted_iota(i32, (hv, hk), 0) // GLA_DV
             == lax.broadcasted_iota(i32, (hv, hk), 1) // GLA_DK)
    scale = GLA_DK ** -0.5
    zero = jnp.zeros((), bf16)

    def group(gi, carry):
        chains = []
        for j in range(GLA_GROUP):
            for d in range(2):
                cj = gi * GLA_GROUP + j
                c = cj if d == 0 else nch - 1 - cj
                chains.append(dict(d=d, rows=pl.ds(pl.multiple_of(c * CHUNK, CHUNK), CHUNK)))
        for ch in chains:
            r = ins[ch['d']]
            ch['gl'] = _mm_hi(r[3][ch['rows'], :], r[4][...]) + r[5][...]
        for ch in chains:
            gl = ch['gl']
            g = (jnp.minimum(gl, 0.0) - jnp.log(1.0 + jnp.exp(-jnp.abs(gl)))) * (1.0 / GLA_TAU)
            ch['gc'] = _mm_hi(tris[ch['d']], g)
        for ch in chains:
            r, rows, gc = ins[ch['d']], ch['rows'], ch['gc']
            gtot = gc[CHUNK - 1:CHUNK, :] if ch['d'] == 0 else gc[0:1, :]
            k = r[1][rows, :].astype(f32)
            ch['qd'] = (r[0][rows, :].astype(f32) * scale * jnp.exp(gc)).astype(bf16)
            ki = (k * jnp.exp(-gc)).astype(bf16)
            ch['kt'] = (k * jnp.exp(gtot - gc)).astype(bf16)
            ch['dec'] = jnp.exp(gtot)
            kstack = jnp.where(kmask, jnp.concatenate([ki] * GLA_H, axis=0), zero)
            ch['sc'] = jnp.where(causals[ch['d']], _nt(ch['qd'], kstack), 0.0).astype(bf16)
        for ch in chains:
            v = ins[ch['d']][2][ch['rows'], :].astype(bf16)
            vbd = jnp.where(vmask, jnp.concatenate([v] * GLA_H, axis=0), zero)
            ch['oi'] = _mm(ch['sc'], vbd)
            ch['kv'] = _tn(v, ch['kt'])
        for ch in chains:
            d = ch['d']
            st = st_ref[d]
            ins[d][6][ch['rows'], :] = (ch['oi'] + _nt(ch['qd'], st.astype(bf16))).astype(bf16)
            st_ref[d] = st * ch['dec'] + jnp.where(smask, ch['kv'], 0.0)
        return carry

    lax.fori_loop(0, nch // GLA_GROUP, group, 0)


def _gla(pa, pb, wgs, bgs, blk=512):
    b, l, _ = pa.shape
    blk = min(blk, l)
    nb = l // blk
    assert l % blk == 0 and (blk // CHUNK) % GLA_GROUP == 0
    hk, hv = GLA_H * GLA_DK, GLA_H * GLA_DV
    up, down = (lambda i: i), (lambda i: nb - 1 - i)

    def cols(bidx):
        col = lambda width, off: pl.BlockSpec((None, blk, width), lambda bi, i: (bi, bidx(i), off // width))
        return [col(hk, A_GLA_Q), col(hk, A_GLA_K), col(hv, A_GLA_V), col(LANES, B_GLA_LR)]

    wspec = pl.BlockSpec((LANES, hk), lambda bi, i: (0, 0))
    bspec = pl.BlockSpec((1, hk), lambda bi, i: (0, 0))
    out = jax.ShapeDtypeStruct((b, l, hv), bf16)
    return pl.pallas_call(
        _gla_body, name="gla",
        out_shape=(out, out),
        grid=(b, nb),
        in_specs=cols(up) + cols(down) + [wspec, bspec, wspec, bspec],
        out_specs=(pl.BlockSpec((None, blk, hv), lambda bi, i: (bi, up(i), 0)),
                   pl.BlockSpec((None, blk, hv), lambda bi, i: (bi, down(i), 0))),
        scratch_shapes=[pltpu.VMEM((2, hv, hk), f32)],
        compiler_params=_cparams("parallel", "arbitrary"),
    )(pa, pa, pa, pb, pa, pa, pa, pb, wgs[0], bgs[0], wgs[1], bgs[1])


def _gdn_prep_body(x_ref, xp_ref, xn_ref, ab_ref, cw_ref, par_ref, qkv_ref, gb_ref):
    i, n = pl.program_id(1), pl.num_programs(1)
    blk = x_ref.shape[0]
    halo = xp_ref.shape[0]
    prev = jnp.where(i > 0, xp_ref[...].astype(f32), 0.0)
    nxt = jnp.where(i < n - 1, xn_ref[...].astype(f32), 0.0)
    ext = jnp.concatenate([prev, x_ref[...].astype(f32), nxt], axis=0)
    tot = blk + 2 * halo
    acc = None
    for t in range(CONV_W):
        sh = (CONV_W // 2 - t) % tot
        xs = ext if sh == 0 else pltpu.roll(ext, sh, 0)
        term = xs[halo:halo + blk, :] * cw_ref[t:t + 1, :]
        acc = term if acc is None else acc + term
    y = _silu(acc)
    nqk = 2 * GDN_H
    for h in range(3 * GDN_H):
        sl = slice(h * LANES, (h + 1) * LANES)
        yh = y[:, sl]
        if h < nqk:
            yh = yh * lax.rsqrt(jnp.sum(yh * yh, axis=-1, keepdims=True) + EPS)
            if h < GDN_H:
                yh = yh * GDN_DK ** -0.5
        qkv_ref[:, sl] = yh.astype(qkv_ref.dtype)
    x = ab_ref[...]
    lane = lax.broadcasted_iota(i32, x.shape, 1)
    xa = x + par_ref[1:2, :]
    softplus = jnp.maximum(xa, 0.0) + jnp.log(1.0 + jnp.exp(-jnp.abs(xa)))
    gb_ref[...] = jnp.where(lane < nqk, par_ref[0:1, :] * softplus, _sigmoid(x))


def _gdn_prep(pa, pb, conv_w, par, blk=256):
    b, l, _ = pa.shape
    blk = min(blk, l)
    nb = l // blk
    w = 3 * BW
    halo = 2 * SUBLANES
    rh = blk // halo
    last = l // halo - 1
    return pl.pallas_call(
        _gdn_prep_body, name="gdn_prep",
        out_shape=(jax.ShapeDtypeStruct((b, l, w), bf16), jax.ShapeDtypeStruct((b, l, LANES), f32)),
        grid=(b, nb),
        in_specs=[pl.BlockSpec((None, blk, w), lambda bi, i: (bi, i, A_GDN_QKV // w)),
                  pl.BlockSpec((None, halo, w), lambda bi, i: (bi, jnp.maximum(i * rh - 1, 0), A_GDN_QKV // w)),
                  pl.BlockSpec((None, halo, w), lambda bi, i: (bi, jnp.minimum((i + 1) * rh, last), A_GDN_QKV // w)),
                  pl.BlockSpec((None, blk, LANES), lambda bi, i: (bi, i, B_GDN_AB // LANES)),
                  pl.BlockSpec((SUBLANES, w), lambda bi, i: (0, 0)),
                  pl.BlockSpec((SUBLANES, LANES), lambda bi, i: (0, 0))],
        out_specs=(pl.BlockSpec((None, blk, w), lambda bi, i: (bi, i, 0)),
                   pl.BlockSpec((None, blk, LANES), lambda bi, i: (bi, i, 0))),
        compiler_params=_cparams("parallel", "parallel"),
    )(pa, pa, pa, pb, conv_w, par)


def _unit_tri_inverses(lws, eye, bd16):
    lds = [jnp.where(bd16, lw, 0.0) for lw in lws]
    los = [lw - ld for lw, ld in zip(lws, lds)]
    ps = [eye - ld for ld in lds]
    pw = lds
    for _ in range(3):
        pw = [_mmb(x, x) for x in pw]
        ps = [p + _mmb(p, x) for p, x in zip(ps, pw)]
    ms = [_mmb(p, lo) for p, lo in zip(ps, los)]
    m2s = [_mmb(m, m) for m in ms]
    qs = [eye - m for m in ms]
    qs = [q + _mmb(q, m2) for q, m2 in zip(qs, m2s)]
    return [_mmb(q, p) for q, p in zip(qs, ps)]


def _gdn_par_body(q_ref, k_ref, v_ref, gb_ref, *out_refs):
    nch = q_ref.shape[0] // CHUNK
    r64 = lax.broadcasted_iota(i32, (CHUNK, CHUNK), 0)
    c64 = lax.broadcasted_iota(i32, (CHUNK, CHUNK), 1)
    eye = _ind(r64 == c64)
    bd16 = (r64 // 16) == (c64 // 16)
    incls = (r64 >= c64, r64 <= c64)
    stricts = (r64 > c64, r64 < c64)
    tris = tuple(_ind(m) for m in incls)

    def chunk_group(ci, carry):
        chains = []
        for j in range(GDN_GROUP):
            c = ci * GDN_GROUP + j
            rows = pl.ds(pl.multiple_of(c * CHUNK, CHUNK), CHUNK)
            gcols = gb_ref[rows, :]
            gams = tuple(_mm_hi(t, gcols) for t in tris)
            gam_ts = tuple(g.T for g in gams)
            for d in range(2):
                out_refs[6 * d + 5][c] = jnp.exp(gams[d][CHUNK - 1:CHUNK, :] if d == 0 else gams[d][0:1, :])
            for h in range(GDN_H):
                sl = slice(h * LANES, (h + 1) * LANES)
                qb, kb16 = q_ref[rows, sl], k_ref[rows, sl]
                raw = _nt(jnp.concatenate([qb, kb16], axis=0), kb16)
                qh, kh, vh = qb.astype(f32), kb16.astype(f32), v_ref[rows, sl].astype(f32)
                for d in range(2):
                    lg = d * GDN_H + h
                    chains.append(dict(rows=rows, h=h, d=d, qh=qh, kh=kh, vh=vh, raw=raw,
                                       gcol=gams[d][:, lg:lg + 1], grow=gam_ts[d][lg:lg + 1, :],
                                       beta=gcols[:, 2 * GDN_H + lg:2 * GDN_H + lg + 1]))
        lws = []
        for ch in chains:
            d, rows, h = ch['d'], ch['rows'], ch['h']
            dec = jnp.where(incls[d], jnp.exp(jnp.where(incls[d], ch['gcol'] - ch['grow'], 0.0)), 0.0)
            out_refs[6 * d + 4][rows, h * CHUNK:(h + 1) * CHUNK] = (ch['raw'][:CHUNK] * dec).astype(bf16)
            lws.append(jnp.where(stricts[d], ch['raw'][CHUNK:] * dec * ch['beta'], 0.0))
        tinvs = _unit_tri_inverses(lws, eye, bd16)
        sols = []
        for ch, tinv in zip(chains, tinvs):
            ch['eg'] = jnp.exp(ch['gcol'])
            kb = ch['kh'] * ch['beta']
            sols.append(_mmb(tinv, jnp.concatenate([ch['vh'] * ch['beta'], kb * ch['eg']], axis=1)))
        for ch, sol in zip(chains, sols):
            d, rows = ch['d'], ch['rows']
            sl = slice(ch['h'] * LANES, (ch['h'] + 1) * LANES)
            u_ref, w_ref, qd_ref, kt_ref = out_refs[6 * d:6 * d + 4]
            gcol = ch['gcol']
            gtot = gcol[CHUNK - 1:CHUNK, :] if d == 0 else gcol[0:1, :]
            u_ref[rows, sl] = sol[:, :GDN_DV]
            w_ref[rows, sl] = sol[:, GDN_DV:].astype(bf16)
            qd_ref[rows, sl] = (ch['qh'] * ch['eg']).astype(bf16)
            kt_ref[rows, sl] = (ch['kh'] * jnp.exp(gtot - gcol)).astype(bf16)
        return carry

    lax.fori_loop(0, nch // GDN_GROUP, chunk_group, 0)


def _gdn_par(qkv, gb, blk=512):
    b, l, _ = qkv.shape
    blk = min(blk, l)
    nch = blk // CHUNK
    assert l % blk == 0 and nch % GDN_GROUP == 0
    col = lambda j: pl.BlockSpec((None, blk, BW), lambda bi, i: (bi, i, j))
    wide = pl.BlockSpec((None, blk, BW), lambda bi, i: (bi, i, 0))
    one_dir_shapes = (jax.ShapeDtypeStruct((b, l, BW), f32), jax.ShapeDtypeStruct((b, l, BW), bf16),
                      jax.ShapeDtypeStruct((b, l, BW), bf16), jax.ShapeDtypeStruct((b, l, BW), bf16),
                      jax.ShapeDtypeStruct((b, l, GDN_H * CHUNK), bf16),
                      jax.ShapeDtypeStruct((b, l // CHUNK, 1, LANES), f32))
    one_dir_specs = (wide, wide, wide, wide,
                     pl.BlockSpec((None, blk, GDN_H * CHUNK), lambda bi, i: (bi, i, 0)),
                     pl.BlockSpec((None, nch, 1, LANES), lambda bi, i: (bi, i, 0, 0)))
    outs = pl.pallas_call(
        _gdn_par_body, name="gdn_par",
        out_shape=one_dir_shapes * 2,
        grid=(b, l // blk),
        in_specs=[col(0), col(1), col(2), pl.BlockSpec((None, blk, LANES), lambda bi, i: (bi, i, 0))],
        out_specs=one_dir_specs * 2,
        compiler_params=_cparams("parallel", "parallel"),
    )(qkv, qkv, qkv, gb)
    return outs[:6], outs[6:]


def _gdn_seq_body(*refs):
    ins = (refs[0:6], refs[6:12])
    o_refs = refs[12:14]
    s_ref = refs[14]
    nch = o_refs[0].shape[0] // CHUNK

    @pl.when(pl.program_id(1) == 0)
    def _():
        s_ref[...] = jnp.zeros_like(s_ref)

    def chunk(ci, carry):
        rows, cds = [], []
        for d in range(2):
            c = ci if d == 0 else nch - 1 - ci
            rows.append(pl.ds(pl.multiple_of(c * CHUNK, CHUNK), CHUNK))
            cds.append(ins[d][5][c])
        dh = [(d, h) for d in range(2) for h in range(GDN_H)]
        sl = lambda h: slice(h * LANES, (h + 1) * LANES)
        ss = [s_ref[d, sl(h), :] for d, h in dh]
        sbs = [s.astype(bf16) for s in ss]
        wss = [_mm(ins[d][1][rows[d], sl(h)], sb) for (d, h), sb in zip(dh, sbs)]
        qss = [_mm(ins[d][2][rows[d], sl(h)], sb) for (d, h), sb in zip(dh, sbs)]
        vns = [(ins[d][0][rows[d], sl(h)] - ws).astype(bf16) for (d, h), ws in zip(dh, wss)]
        avs = [_mm(ins[d][4][rows[d], h * CHUNK:(h + 1) * CHUNK], vn) for (d, h), vn in zip(dh, vns)]
        kvs = [_tn(ins[d][3][rows[d], sl(h)], vn) for (d, h), vn in zip(dh, vns)]
        for (d, h), s, kv in zip(dh, ss, kvs):
            lg = d * GDN_H + h
            s_ref[d, sl(h), :] = s * cds[d][:, lg:lg + 1] + kv
        for d in range(2):
            o_refs[d][rows[d], :] = jnp.concatenate(
                [qs + av for (dd, _), qs, av in zip(dh, qss, avs) if dd == d], axis=1).astype(bf16)
        return carry

    lax.fori_loop(0, nch, chunk, 0)


def _gdn_seq(fwd_in, bwd_in, blk=512):
    b, l, _ = fwd_in[0].shape
    blk = min(blk, l)
    nb = l // blk
    nch = blk // CHUNK

    def specs(bidx):
        wide = pl.BlockSpec((None, blk, BW), lambda bi, i: (bi, bidx(i), 0))
        return [wide, wide, wide, wide,
                pl.BlockSpec((None, blk, GDN_H * CHUNK), lambda bi, i: (bi, bidx(i), 0)),
                pl.BlockSpec((None, nch, 1, LANES), lambda bi, i: (bi, bidx(i), 0, 0))]

    up, down = (lambda i: i), (lambda i: nb - 1 - i)
    out = jax.ShapeDtypeStruct((b, l, BW), bf16)
    return pl.pallas_call(
        _gdn_seq_body, name="gdn_seq",
        out_shape=(out, out),
        grid=(b, nb),
        in_specs=specs(up) + specs(down),
        out_specs=(pl.BlockSpec((None, blk, BW), lambda bi, i: (bi, up(i), 0)),
                   pl.BlockSpec((None, blk, BW), lambda bi, i: (bi, down(i), 0))),
        scratch_shapes=[pltpu.VMEM((2, GDN_H * GDN_DK, GDN_DV), f32)],
        compiler_params=_cparams("parallel", "arbitrary"),
    )(*fwd_in, *bwd_in)


def _head_norm_gate(o, gain, z):
    outs = []
    for h in range(BW // LANES):
        oh = o[:, h * LANES:(h + 1) * LANES]
        outs.append(oh * lax.rsqrt(jnp.mean(oh * oh, axis=-1, keepdims=True) + EPS) * gain)
    return jnp.concatenate(outs, axis=1) * _silu(z)


def _merge_body(h_ref, gate_ref, ch_ref, u_ref, glaf_ref, glab_ref, r_ref, gdnf_ref, gdnb_ref, z_ref,
                dsk_ref, wglu_ref, gng_ref, dng_ref, wbr_ref, wout_ref, o_ref):
    up = lambda ref: ref[...].astype(f32)
    y0 = _gelu_tanh(ch_ref[...] + dsk_ref[...] * u_ref[...])
    y_s5 = y0 * _sigmoid(_mm(y0.astype(bf16), wglu_ref[...]))
    y_gla = _head_norm_gate(up(glaf_ref) + up(glab_ref), gng_ref[...], up(r_ref))
    y_gdn = _head_norm_gate(up(gdnf_ref) + up(gdnb_ref), dng_ref[...], up(z_ref))
    merged = None
    for r, y in enumerate((y_s5, y_gla, y_gdn)):
        gate = _sigmoid(gate_ref[:, r * D_MODEL:(r + 1) * D_MODEL].astype(f32))
        term = gate * _mm(y.astype(bf16), wbr_ref[r])
        merged = term if merged is None else merged + term
    o_ref[...] = h_ref[...] + _mm(merged.astype(bf16), wout_ref[...])


def _merge(h2d, pa2d, pb2d, ch, gla_f, gla_b, gdn_f, gdn_b, dsk, wglu, gng, dng, wbr, wout, tm=256):
    m = h2d.shape[0]
    row = lambda width, off=0: pl.BlockSpec((tm, width), lambda i: (i, off // width))
    full = lambda shape: pl.BlockSpec(shape, lambda i: (0,) * len(shape))
    return pl.pallas_call(
        _merge_body, name="merge",
        out_shape=jax.ShapeDtypeStruct((m, D_MODEL), f32),
        grid=(m // tm,),
        in_specs=[row(D_MODEL), row(3 * D_MODEL, A_GATE), row(BW), row(BW, B_S5), row(BW), row(BW),
                  row(BW, A_GLA_R), row(BW), row(BW), row(BW, A_GDN_Z),
                  full((1, BW)), full((BW, BW)), full((1, LANES)), full((1, LANES)),
                  full((3, BW, D_MODEL)), full((D_MODEL, D_MODEL))],
        out_specs=row(D_MODEL),
        compiler_params=_cparams("parallel"),
    )(h2d, pa2d, ch, pb2d, gla_f, gla_b, pa2d, gdn_f, gdn_b, pa2d, dsk, wglu, gng, dng, wbr, wout)


def _router_body(h_ref, g_ref, wr_ref, hn_ref, aff_ref):
    hn = _rms(h_ref[...], g_ref[...])
    hn_ref[...] = hn.astype(bf16)
    logits = lax.dot_general(wr_ref[...], hn, (((1,), (1,)), ((), ())), preferred_element_type=f32,
                             precision=HIGHEST)
    e = jnp.exp(logits - jnp.max(logits, axis=0, keepdims=True))
    aff_ref[...] = e / jnp.sum(e, axis=0, keepdims=True)


def _router(h2d, gain, wr_t, tm=512):
    m = h2d.shape[0]
    return pl.pallas_call(
        _router_body, name="router",
        out_shape=(jax.ShapeDtypeStruct((m, D_MODEL), bf16), jax.ShapeDtypeStruct((N_EXPERTS, m), f32)),
        grid=(m // tm,),
        in_specs=[pl.BlockSpec((tm, D_MODEL), lambda i: (i, 0)),
                  pl.BlockSpec((1, D_MODEL), lambda i: (0, 0)),
                  pl.BlockSpec((N_EXPERTS, D_MODEL), lambda i: (0, 0))],
        out_specs=(pl.BlockSpec((tm, D_MODEL), lambda i: (i, 0)), pl.BlockSpec((N_EXPERTS, tm), lambda i: (0, i))),
        compiler_params=_cparams("parallel"),
    )(h2d, gain, wr_t)


def _threshold_body(aff_ref, thr_ref, *, cap):
    keys = pltpu.bitcast(aff_ref[...], i32)

    def count(mask):
        return jnp.sum(jnp.where(mask, 1.0, 0.0), axis=1, keepdims=True).astype(i32)

    def bit(bi, t):
        cand = t | (1 << (30 - bi))
        return jnp.where(count(keys >= cand) >= cap, cand, t)

    t = lax.fori_loop(0, 31, bit, jnp.zeros((N_EXPERTS, 1), i32))
    budget = cap - count(keys > t)
    lane = lax.broadcasted_iota(i32, (N_EXPERTS, LANES), 1)
    thr_ref[...] = jnp.where(lane == 0, t, jnp.where(lane == 1, budget, 0))


def _threshold(aff_t, cap):
    n = aff_t.shape[1]
    return pl.pallas_call(
        functools.partial(_threshold_body, cap=cap), name="topc_threshold",
        out_shape=jax.ShapeDtypeStruct((N_EXPERTS, LANES), i32),
        in_specs=[pl.BlockSpec((N_EXPERTS, n), lambda: (0, 0))],
        out_specs=pl.BlockSpec((N_EXPERTS, LANES), lambda: (0, 0)),
        compiler_params=pltpu.CompilerParams(vmem_limit_bytes=VMEM_LIMIT_BYTES),
    )(aff_t)


def _slots_body(aff_ref, thr_ref, slot_ref, wts_ref, cnt_ref, run_ref):
    @pl.when(pl.program_id(0) == 0)
    def _():
        run_ref[...] = jnp.zeros_like(run_ref)

    tt = aff_ref.shape[1]
    aff = aff_ref[...]
    keys = pltpu.bitcast(aff, i32)
    t = thr_ref[:, 0:1]
    budget = thr_ref[:, 1:2]
    upper = _ind(lax.broadcasted_iota(i32, (tt, tt), 0) <= lax.broadcasted_iota(i32, (tt, tt), 1), bf16)
    eq = keys == t
    sel_run = run_ref[:, 0:1]
    tie_run = run_ref[:, 1:2]
    cs_eq = _mm(_ind(eq, bf16), upper).astype(i32)
    tie_rank = tie_run + cs_eq - 1
    sel = (keys > t) | (eq & (tie_rank < budget))
    cs_sel = _mm(_ind(sel, bf16), upper).astype(i32)
    slot_ref[...] = jnp.where(sel, sel_run + cs_sel - 1, -1)
    wts_ref[...] = jnp.where(sel, aff, 0.0)
    n_sel = cs_sel[:, tt - 1:tt]
    n_eq = cs_eq[:, tt - 1:tt]
    n_rows = ((n_sel + (SUBLANES - 1)) // SUBLANES) * SUBLANES
    cnt_ref[...] = jnp.broadcast_to(n_rows, cnt_ref.shape)
    lane = lax.broadcasted_iota(i32, run_ref.shape, 1)
    run_ref[...] = run_ref[...] + jnp.where(lane == 0, n_rows, jnp.where(lane == 1, n_eq, 0))


def _slots(aff_t, thr):
    n = aff_t.shape[1]
    nt = n // MOE_TT
    return pl.pallas_call(
        _slots_body, name="topc_slots",
        out_shape=(jax.ShapeDtypeStruct((N_EXPERTS, n), i32), jax.ShapeDtypeStruct((N_EXPERTS, n), f32),
                   jax.ShapeDtypeStruct((nt, N_EXPERTS, LANES), i32)),
        grid=(nt,),
        in_specs=[pl.BlockSpec((N_EXPERTS, MOE_TT), lambda i: (0, i)),
                  pl.BlockSpec((N_EXPERTS, LANES), lambda i: (0, 0))],
        out_specs=(pl.BlockSpec((N_EXPERTS, MOE_TT), lambda i: (0, i)),
                   pl.BlockSpec((N_EXPERTS, MOE_TT), lambda i: (0, i)),
                   pl.BlockSpec((None, N_EXPERTS, LANES), lambda i: (i, 0, 0))),
        scratch_shapes=[pltpu.VMEM((N_EXPERTS, LANES), i32)],
        compiler_params=_cparams("arbitrary"),
    )(aff_t, thr)


def _window_hits(slot_rows, starts, width):
    tt = slot_rows.shape[1]
    r = lax.broadcasted_iota(i32, (width, tt), 0)
    return [r == (slot_rows[e:e + 1, :] - starts[e]) for e in range(len(starts))]


def _dispatch_body(off_ref, x_ref, slot_ref, xe_hbm, buf_ref, xbuf_ref, sem, *, cap):
    i = pl.program_id(0)
    w = MOE_W
    x = x_ref[...]
    slot_rows = slot_ref[...]
    starts = [off_ref[e, i] for e in range(N_EXPERTS)]

    def window_copy(e, start, src):
        return pltpu.make_async_copy(src, xe_hbm.at[e, pl.ds(pl.multiple_of(start, SUBLANES), w)], sem.at[e])

    def window_wait(e):
        pltpu.make_async_copy(buf_ref.at[e], xe_hbm.at[e, pl.ds(0, w)], sem.at[e]).wait()

    @pl.when(i > 0)
    def _():
        for e in range(N_EXPERTS):
            window_wait(e)

    @pl.when(i == 0)
    def _():
        xbuf_ref[...] = jnp.zeros_like(xbuf_ref)
        tails = [pltpu.make_async_copy(xbuf_ref, xe_hbm.at[e, pl.ds(r0, MOE_PAD)], sem.at[N_EXPERTS])
                 for e in range(N_EXPERTS) for r0 in range(cap, xe_hbm.shape[1], MOE_PAD)]
        for cp in tails:
            cp.start()
        for cp in tails:
            cp.wait()

    hits = _window_hits(slot_rows, starts, w)
    lhs = jnp.concatenate([_ind(h, bf16) for h in hits], axis=0)
    rows = _mm(lhs, x).astype(bf16)
    for e in range(N_EXPERTS):
        buf_ref[e] = rows[e * w:(e + 1) * w, :]
        window_copy(e, starts[e], buf_ref.at[e]).start()

    for e in range(N_EXPERTS):
        n_rows = off_ref[e, i + 1] - starts[e]

        def extra(k, carry, e=e):
            start = starts[e] + k * w
            hit = _window_hits(slot_rows[e:e + 1, :], [start], w)[0]
            xbuf_ref[0:w, :] = _mm(_ind(hit, bf16), x).astype(bf16)
            cp = pltpu.make_async_copy(xbuf_ref.at[pl.ds(0, w)],
                                       xe_hbm.at[e, pl.ds(pl.multiple_of(start, SUBLANES), w)], sem.at[N_EXPERTS])
            cp.start()
            cp.wait()
            return carry

        lax.fori_loop(1, (n_rows + (w - 1)) // w, extra, 0)

    @pl.when(i == pl.num_programs(0) - 1)
    def _():
        for e in range(N_EXPERTS):
            window_wait(e)


def _dispatch(hn, slot, off, rows_alloc, cap):
    n = hn.shape[0]
    grid_spec = pltpu.PrefetchScalarGridSpec(
        num_scalar_prefetch=1,
        grid=(n // MOE_TT,),
        in_specs=[pl.BlockSpec((MOE_TT, D_MODEL), lambda i, o: (i, 0)),
                  pl.BlockSpec((N_EXPERTS, MOE_TT), lambda i, o: (0, i))],
        out_specs=pl.BlockSpec(memory_space=pl.ANY),
        scratch_shapes=[pltpu.VMEM((N_EXPERTS, MOE_W, D_MODEL), bf16), pltpu.VMEM((MOE_PAD, D_MODEL), bf16),
                        pltpu.SemaphoreType.DMA((N_EXPERTS + 1,))])
    return pl.pallas_call(
        functools.partial(_dispatch_body, cap=cap), name="dispatch",
        out_shape=jax.ShapeDtypeStruct((N_EXPERTS, rows_alloc, D_MODEL), bf16),
        grid_spec=grid_spec,
        compiler_params=_cparams("arbitrary"),
    )(off, hn, slot)


def _experts_body(tot_ref, x_ref, wg_ref, wu_ref, wd_ref, y_ref):
    e, j = pl.program_id(0), pl.program_id(1)
    fb = x_ref.shape[0]
    n_valid = tot_ref[e] - j * fb

    def ffn(x):
        hid = (_silu(_mm(x, wg_ref[...])) * _mm(x, wu_ref[...])).astype(bf16)
        y_ref[...] = _mm(hid, wd_ref[...]).astype(y_ref.dtype)

    @pl.when(n_valid >= fb)
    def _():
        ffn(x_ref[...])

    @pl.when((n_valid > 0) & (n_valid < fb))
    def _():
        row = lax.broadcasted_iota(i32, (fb, 1), 0)
        ffn(jnp.where(row < n_valid, x_ref[...], jnp.zeros((), bf16)))

    @pl.when(n_valid <= 0)
    def _():
        y_ref[...] = jnp.zeros_like(y_ref)


def _experts(xe, total, wg, wu, wd, rows):
    grid_spec = pltpu.PrefetchScalarGridSpec(
        num_scalar_prefetch=1,
        grid=(N_EXPERTS, rows // MOE_FB),
        in_specs=[pl.BlockSpec((None, MOE_FB, D_MODEL), lambda e, j, t: (e, j, 0)),
                  pl.BlockSpec((None, D_MODEL, EXPERT_FF), lambda e, j, t: (e, 0, 0)),
                  pl.BlockSpec((None, D_MODEL, EXPERT_FF), lambda e, j, t: (e, 0, 0)),
                  pl.BlockSpec((None, EXPERT_FF, D_MODEL), lambda e, j, t: (e, 0, 0))],
        out_specs=pl.BlockSpec((None, MOE_FB, D_MODEL), lambda e, j, t: (e, j, 0)))
    return pl.pallas_call(
        _experts_body, name="experts",
        out_shape=jax.ShapeDtypeStruct((N_EXPERTS, rows, D_MODEL), bf16),
        grid_spec=grid_spec,
        compiler_params=_cparams("parallel", "arbitrary"),
    )(total, xe, wg, wu, wd)


def _combine_body(off_ref, h_ref, slot_ref, wts_ref, p_ref, g_ref, wpg_ref, wpp_ref, ye_hbm, o_ref,
                  win_ref, xwin_ref, acc_ref, sem, *, rows):
    i, n = pl.program_id(0), pl.num_programs(0)
    w = MOE_W
    tt = h_ref.shape[0]

    def wstart(e, tile, k=0):
        return pl.multiple_of(jnp.minimum(off_ref[e, tile] + k * w, rows - w), SUBLANES)

    def window_copy(e, tile, par):
        return pltpu.make_async_copy(ye_hbm.at[e, pl.ds(wstart(e, tile), w)], win_ref.at[par, e], sem.at[par, e])

    @pl.when(i == 0)
    def _():
        for e in range(N_EXPERTS):
            window_copy(e, 0, 0).start()

    @pl.when(i + 1 < n)
    def _():
        for e in range(N_EXPERTS):
            window_copy(e, i + 1, (i + 1) % 2).start()

    par = i % 2
    slot_rows = slot_ref[...]
    gates = jnp.transpose(wts_ref[...])
    acc_ref[...] = h_ref[...]
    r = lax.broadcasted_iota(i32, (w, tt), 0)
    for e in range(N_EXPERTS):
        window_copy(e, i, par).wait()
        s_row = slot_rows[e:e + 1, :]
        lo = off_ref[e, i]
        hit = (r == s_row - wstart(e, i)) & (s_row < lo + w)
        acc_ref[...] += gates[:, e:e + 1] * _tn(_ind(hit, bf16), win_ref[par, e])

    for e in range(N_EXPERTS):
        lo = off_ref[e, i]
        n_rows = off_ref[e, i + 1] - lo

        def extra(k, carry, e=e, lo=lo):
            cp = pltpu.make_async_copy(ye_hbm.at[e, pl.ds(wstart(e, i, k), w)], xwin_ref, sem.at[2, 0])
            cp.start()
            cp.wait()
            s_row = slot_rows[e:e + 1, :]
            hit = (r == s_row - wstart(e, i, k)) & (s_row >= lo + k * w) & (s_row < lo + (k + 1) * w)
            acc_ref[...] += gates[:, e:e + 1] * _tn(_ind(hit, bf16), xwin_ref[...])
            return carry

        lax.fori_loop(1, (n_rows + (w - 1)) // w, extra, 0)

    h2 = acc_ref[...]
    gate = _sigmoid(_mm(_rms(h2, g_ref[...]).astype(bf16), wpg_ref[...]))
    o_ref[...] = h2 + gate * _mm(p_ref[...].astype(bf16), wpp_ref[...])


def _combine(h2d, slot, wts, ye, p2d, off, g_ple, wpg, wpp):
    m = h2d.shape[0]
    rows = ye.shape[1]
    full = lambda shape: pl.BlockSpec(shape, lambda i, o: (0,) * len(shape))
    grid_spec = pltpu.PrefetchScalarGridSpec(
        num_scalar_prefetch=1,
        grid=(m // MOE_TT,),
        in_specs=[pl.BlockSpec((MOE_TT, D_MODEL), lambda i, o: (i, 0)),
                  pl.BlockSpec((N_EXPERTS, MOE_TT), lambda i, o: (0, i)),
                  pl.BlockSpec((N_EXPERTS, MOE_TT), lambda i, o: (0, i)),
                  pl.BlockSpec((MOE_TT, PLE_DIM), lambda i, o: (i, 0)),
                  full((1, D_MODEL)), full((D_MODEL, D_MODEL)), full((PLE_DIM, D_MODEL)),
                  pl.BlockSpec(memory_space=pl.ANY)],
        out_specs=pl.BlockSpec((MOE_TT, D_MODEL), lambda i, o: (i, 0)),
        scratch_shapes=[pltpu.VMEM((2, N_EXPERTS, MOE_W, D_MODEL), bf16), pltpu.VMEM((MOE_W, D_MODEL), bf16),
                        pltpu.VMEM((MOE_TT, D_MODEL), f32), pltpu.SemaphoreType.DMA((3, N_EXPERTS))])
    return pl.pallas_call(
        functools.partial(_combine_body, rows=rows), name="combine_ple",
        out_shape=jax.ShapeDtypeStruct((m, D_MODEL), f32),
        grid_spec=grid_spec,
        compiler_params=_cparams("arbitrary"),
    )(off, h2d, slot, wts, p2d, g_ple, wpg, wpp, ye)


def _moe_ple(h2d, p2d, g_ffn, wr_t, wg, wu, wd, g_ple, wpg, wpp):
    n = h2d.shape[0]
    nt = n // MOE_TT
    cap = max(1, EC_FACTOR * n // N_EXPERTS)
    rows = -(-(cap + SUBLANES * nt) // MOE_FB) * MOE_FB
    assert n % MOE_TT == 0 and cap % MOE_PAD == 0 and MOE_W <= MOE_PAD and MOE_W <= cap
    hn, aff_t = _router(h2d, g_ffn, wr_t)
    thr = _threshold(aff_t, cap)
    slot, wts, cnt = _slots(aff_t, thr)
    off = jnp.concatenate([jnp.zeros((1, N_EXPERTS), i32), jnp.cumsum(cnt[:, :, 0], axis=0, dtype=i32)], axis=0)
    off = jnp.transpose(off)
    xe = _dispatch(hn, slot, off, rows + MOE_PAD, cap)
    ye = _experts(xe, off[:, nt], wg, wu, wd, rows)
    return _combine(h2d, slot, wts, ye, p2d, off, g_ple, wpg, wpp)


def _final_norm_body(x_ref, g_ref, o_ref):
    o_ref[...] = _rms(x_ref[...], g_ref[...])


def _final_norm(h2d, gain, tm=1024):
    m = h2d.shape[0]
    return pl.pallas_call(
        _final_norm_body, name="final_norm",
        out_shape=jax.ShapeDtypeStruct((m, D_MODEL), f32),
        grid=(m // tm,),
        in_specs=[pl.BlockSpec((tm, D_MODEL), lambda i: (i, 0)), pl.BlockSpec((1, D_MODEL), lambda i: (0, 0))],
        out_specs=pl.BlockSpec((tm, D_MODEL), lambda i: (i, 0)),
        compiler_params=_cparams("parallel"),
    )(h2d, gain)


def _layer_weights(w):
    lw = {}
    lw['g_mix'] = w['norm_mix'].reshape(1, D_MODEL)
    lw['w_in'] = _reorder_w_in(w['w_in'])
    lw['s5'] = _s5_weights(w['s5_B_re'], w['s5_B_im'], w['s5_C_re'], w['s5_C_im'],
                           w['s5_lam_re'], w['s5_lam_im'], w['s5_log_dt'])
    wgate = w['gla_w_gate']
    lw['gla_wg'] = tuple(jnp.zeros((LANES, GLA_H * GLA_DK), f32).at[d * GLA_RANK:(d + 1) * GLA_RANK].set(wgate[d])
                         for d in range(2))
    lw['gla_bg'] = tuple(w['gla_b_gate'][d].reshape(1, -1) for d in range(2))
    lw['conv_w'] = jnp.pad(jnp.transpose(w['gdn_conv']), ((0, SUBLANES - CONV_W), (0, 0)))
    neg_a = -jnp.exp(w['gdn_A_log']).reshape(-1)
    par = jnp.zeros((SUBLANES, LANES), f32).at[0, :2 * GDN_H].set(neg_a).at[1, :2 * GDN_H].set(
        w['gdn_dt_bias'].reshape(-1))
    lw['gdn_par'] = par
    lw['dsk'] = w['s5_D'].reshape(1, BW)
    lw['wglu'] = w['s5_w_glu'].astype(bf16)
    lw['gng'] = w['gla_norm'].reshape(1, LANES)
    lw['dng'] = w['gdn_norm'].reshape(1, LANES)
    lw['wbr'] = w['w_branch'].astype(bf16)
    lw['wout'] = w['w_out'].astype(bf16)
    lw['g_ffn'] = w['norm_ffn'].reshape(1, D_MODEL)
    lw['wr_t'] = jnp.transpose(w['w_router'])
    lw['wg'] = w['w_exp_gate'].astype(bf16)
    lw['wu'] = w['w_exp_up'].astype(bf16)
    lw['wd'] = w['w_exp_down'].astype(bf16)
    lw['g_ple'] = w['norm_ple'].reshape(1, D_MODEL)
    lw['wpg'] = w['w_ple_gate'].astype(bf16)
    lw['wpp'] = w['w_ple_proj'].astype(bf16)
    return lw


def _mixers(h, lw):
    b, l, _ = h.shape
    h2d = h.reshape(b * l, D_MODEL)
    pa2d, pb2d = _inproj(h2d, lw['g_mix'], lw['w_in'])
    pa, pb = pa2d.reshape(b, l, D_A), pb2d.reshape(b, l, D_B)
    ch = _s5_mixer(pb, lw['s5'])
    gla_f, gla_b = _gla(pa, pb, lw['gla_wg'], lw['gla_bg'])
    qkv, gb = _gdn_prep(pa, pb, lw['conv_w'], lw['gdn_par'])
    gdn_f, gdn_b = _gdn_seq(*_gdn_par(qkv, gb))
    flat = lambda a: a.reshape(b * l, a.shape[-1])
    return _merge(h2d, pa2d, pb2d, flat(ch), flat(gla_f), flat(gla_b), flat(gdn_f), flat(gdn_b),
                  lw['dsk'], lw['wglu'], lw['gng'], lw['dng'], lw['wbr'], lw['wout'])


def _layer(h, p_i, lw):
    b, l, _ = h.shape
    h1 = _mixers(h, lw)
    h3 = _moe_ple(h1, p_i.reshape(b * l, PLE_DIM), lw['g_ffn'], lw['wr_t'], lw['wg'], lw['wu'], lw['wd'],
                  lw['g_ple'], lw['wpg'], lw['wpp'])
    return h3.reshape(b, l, D_MODEL)


def kernel(x_prompt, x_sample, p_prompt, p_sample, norm_mix, w_in, s5_B_re, s5_B_im, s5_C_re, s5_C_im, s5_D, s5_lam_re, s5_lam_im, s5_log_dt, s5_w_glu, gla_w_gate, gla_b_gate, gla_norm, gdn_conv, gdn_A_log, gdn_dt_bias, gdn_norm, w_branch, w_out, norm_ffn, w_router, w_exp_gate, w_exp_up, w_exp_down, norm_ple, w_ple_gate, w_ple_proj, norm_final):
    weights = dict(norm_mix=norm_mix, w_in=w_in, s5_B_re=s5_B_re, s5_B_im=s5_B_im, s5_C_re=s5_C_re, s5_C_im=s5_C_im,
                   s5_D=s5_D, s5_lam_re=s5_lam_re, s5_lam_im=s5_lam_im, s5_log_dt=s5_log_dt, s5_w_glu=s5_w_glu,
                   gla_w_gate=gla_w_gate, gla_b_gate=gla_b_gate, gla_norm=gla_norm, gdn_conv=gdn_conv,
                   gdn_A_log=gdn_A_log, gdn_dt_bias=gdn_dt_bias, gdn_norm=gdn_norm, w_branch=w_branch, w_out=w_out,
                   norm_ffn=norm_ffn, w_router=w_router, w_exp_gate=w_exp_gate, w_exp_up=w_exp_up,
                   w_exp_down=w_exp_down, norm_ple=norm_ple, w_ple_gate=w_ple_gate, w_ple_proj=w_ple_proj)

    def body(carry, xs):
        hp, hs = carry
        w_i, pp, ps = xs
        lw = _layer_weights(w_i)
        return (_layer(hp, pp, lw), _layer(hs, ps, lw)), None

    (hp, hs), _ = lax.scan(body, (x_prompt.astype(f32), x_sample.astype(f32)), (weights, p_prompt, p_sample))
    g_fin = norm_final.reshape(1, D_MODEL)
    yp = _final_norm(hp.reshape(-1, D_MODEL), g_fin).reshape(x_prompt.shape).astype(x_prompt.dtype)
    ys = _final_norm(hs.reshape(-1, D_MODEL), g_fin).reshape(x_sample.shape).astype(x_sample.dtype)
    return (yp, ys)
```

```python
import functools
import math

import jax
import jax.numpy as jnp
import numpy as np
from jax import lax
from jax.experimental import pallas as pl
from jax.experimental.pallas import tpu as pltpu

f32 = jnp.float32
bf16 = jnp.bfloat16
i32 = jnp.int32
HIGHEST = lax.Precision.HIGHEST

D_MODEL = 1024
DEPTH = 4
PLE_DIM = 256
BW = 512
EPS = 1e-6
CHUNK = 64
S5_GROUPS, S5_GC, S5_STATE = 32, 16, 64
GLA_H, GLA_DK, GLA_DV, GLA_RANK, GLA_TAU = 4, 64, 128, 16, 16.0
GDN_H, GDN_DK, GDN_DV, CONV_W = 4, 128, 128, 5
N_EXPERTS, EXPERT_FF, EC_FACTOR = 16, 2048, 2

LANES = 128
SUBLANES = 8
VMEM_LIMIT_BYTES = 56 * 1024 * 1024

A_GATE, A_GDN_QKV, A_GDN_Z, A_GLA_V, A_GLA_R, A_GLA_Q, A_GLA_K = 0, 3072, 4608, 5120, 5632, 6144, 6400
D_A = 6656
B_S5, B_GLA_LR, B_GDN_AB = 0, 512, 640
D_B = 768
D_INP = D_A + D_B

S5_T = 8
S5_LB = BW // LANES
S5_SW = 8 * S5_STATE * 2
S5_SCAN_CB = 2

GDN_GROUP = 4
GLA_GROUP = 4

MOE_TT = 512
MOE_W = 96
MOE_PAD = 128
MOE_FB = 512


def _cparams(*sem):
    return pltpu.CompilerParams(dimension_semantics=sem, vmem_limit_bytes=VMEM_LIMIT_BYTES)


def _nt(a, b):
    return lax.dot_general(a, b, (((1,), (1,)), ((), ())), preferred_element_type=f32)


def _tn(a, b):
    return lax.dot_general(a, b, (((0,), (0,)), ((), ())), preferred_element_type=f32)


def _mm(a, b):
    return jnp.dot(a, b, preferred_element_type=f32)


def _mm_hi(a, b):
    return jnp.dot(a, b, preferred_element_type=f32, precision=HIGHEST)


def _mmb(a, b):
    return _mm(a.astype(bf16), b.astype(bf16))


def _split3(x):
    hi = x.astype(bf16)
    r = x - hi.astype(f32)
    mid = r.astype(bf16)
    return hi, mid, (r - mid.astype(f32)).astype(bf16)


def _mm_mask(mask01, x):
    m = mask01.astype(bf16)
    hi, mid, lo = _split3(x)
    return _mm(m, hi) + (_mm(m, mid) + _mm(m, lo))


def _ind(mask, dtype=f32):
    return jnp.where(mask, 1.0, 0.0).astype(dtype)


def _sigmoid(x):
    return 0.5 * jnp.tanh(0.5 * x) + 0.5


def _silu(x):
    return x * _sigmoid(x)


def _gelu_tanh(x):
    return 0.5 * x * (1.0 + jnp.tanh(math.sqrt(2.0 / math.pi) * (x + 0.044715 * (x * x * x))))


def _rms(x, g):
    return x * lax.rsqrt(jnp.mean(x * x, axis=-1, keepdims=True) + EPS) * g


def _inproj_body(x_ref, g_ref, w_ref, oa_ref, ob_ref):
    xn = _rms(x_ref[...], g_ref[...]).astype(bf16)
    oa_ref[...] = _mm(xn, w_ref[:, :D_A]).astype(bf16)
    ob_ref[...] = _mm(xn, w_ref[:, D_A:])


def _inproj(x2d, gain, w_p, tm=256):
    m = x2d.shape[0]
    return pl.pallas_call(
        _inproj_body, name="inproj",
        out_shape=(jax.ShapeDtypeStruct((m, D_A), bf16), jax.ShapeDtypeStruct((m, D_B), f32)),
        grid=(m // tm,),
        in_specs=[pl.BlockSpec((tm, D_MODEL), lambda i: (i, 0)),
                  pl.BlockSpec((1, D_MODEL), lambda i: (0, 0)),
                  pl.BlockSpec((D_MODEL, D_INP), lambda i: (0, 0), pipeline_mode=pl.Buffered(1))],
        out_specs=(pl.BlockSpec((tm, D_A), lambda i: (i, 0)), pl.BlockSpec((tm, D_B), lambda i: (i, 0))),
        compiler_params=_cparams("parallel"),
    )(x2d, gain, w_p)


def _reorder_w_in(w_in):
    o = np.cumsum((0, 512, 256, 256, 512, 512, 32, 512, 512, 512, 512, 8, 8, 3072))
    seg = lambda k: w_in[:, o[k]:o[k + 1]]
    zpad = lambda a: jnp.pad(a, ((0, 0), (0, LANES - a.shape[1])))
    parts = [seg(12), seg(6), seg(7), seg(8), seg(9), seg(3), seg(4), seg(1), seg(2),
             seg(0), zpad(seg(5)), zpad(jnp.concatenate([seg(10), seg(11)], axis=1))]
    return jnp.concatenate(parts, axis=1).astype(bf16)


def _s5_weights(b_re, b_im, c_re, c_im, lam_re, lam_im, log_dt):
    T = S5_T
    dt = jnp.exp(log_dt)[:, :, None]
    lr, li = lam_re, lam_im
    mag = jnp.exp(lr * dt)
    ab_re, ab_im = mag * jnp.cos(li * dt), mag * jnp.sin(li * dt)
    den = lr * lr + li * li
    num_re = ab_re - 1.0
    coef_re = (num_re * lr + ab_im * li) / den
    coef_im = (ab_im * lr - num_re * li) / den
    xb_re = coef_re[..., None] * b_re[None] - coef_im[..., None] * b_im[None]
    xb_im = coef_re[..., None] * b_im[None] + coef_im[..., None] * b_re[None]

    def powers(taus):
        tau = jnp.asarray(taus, lr.dtype)
        pm = jnp.exp((lr * dt)[..., None] * tau)
        ang = (li * dt)[..., None] * tau
        return pm * jnp.cos(ang), pm * jnp.sin(ang)

    p_re, p_im = powers(np.arange(T + 1))
    cp_re = c_re[None, :, :, :, None] * p_re[:, :, None] - c_im[None, :, :, :, None] * p_im[:, :, None]
    cp_im = c_re[None, :, :, :, None] * p_im[:, :, None] + c_im[None, :, :, :, None] * p_re[:, :, None]
    kern = (jnp.einsum('dgknt,dgnc->dgtkc', cp_re, xb_re, precision=HIGHEST)
            - jnp.einsum('dgknt,dgnc->dgtkc', cp_im, xb_im, precision=HIGHEST))
    s_idx = np.arange(T)[:, None]
    t_idx = np.arange(T)[None, :]
    lag_f = np.clip(t_idx - s_idx, 0, T)
    lag_b = np.clip(s_idx - t_idx, 0, T)
    m_f = jnp.asarray((t_idx >= s_idx), kern.dtype)[None, :, :, None, None]
    m_b = jnp.asarray((s_idx >= t_idx), kern.dtype)[None, :, :, None, None]
    a_g = kern[0][:, lag_f] * m_f + kern[1][:, lag_b] * m_b
    dtype = kern.dtype
    wide = T * LANES
    col = np.arange(wide)
    row_grp_lane = lax.broadcasted_iota(i32, (wide, wide), 0) // S5_GC % 8
    col_grp_lane = lax.broadcasted_iota(i32, (wide, wide), 1) // S5_GC % 8
    st_idx = lambda ax: (2 * (lax.broadcasted_iota(i32, (wide, wide), ax) // (2 * LANES))
                         + lax.broadcasted_iota(i32, (wide, wide), ax) % LANES // S5_STATE)
    exp_tk = jnp.asarray((np.arange(LANES)[:, None] // S5_GC == col[None, :] // LANES)
                         & (np.arange(LANES)[:, None] % S5_GC == col[None, :] % S5_GC), dtype)
    exp_pn = jnp.asarray((np.arange(LANES)[:, None] // S5_STATE == col[None, :] % (2 * LANES) // LANES)
                         & (np.arange(LANES)[:, None] % S5_STATE == col[None, :] % S5_STATE), dtype)

    def expand(compact, expansion, mask):
        return jnp.where(mask, jnp.einsum('jrm,mn->jrn', compact, expansion, precision=HIGHEST), 0.0)

    a_c = jnp.transpose(a_g.reshape(S5_LB, 8, T, T, S5_GC, S5_GC), (0, 2, 1, 5, 3, 4))
    a_blk = expand(a_c.reshape(S5_LB, wide, LANES), exp_tk, row_grp_lane == col_grp_lane)

    def state_in(d, taus):
        e_re = p_re[d][:, :, taus][..., None] * xb_re[d][:, :, None, :] - p_im[d][:, :, taus][..., None] * xb_im[d][:, :, None, :]
        e_im = p_re[d][:, :, taus][..., None] * xb_im[d][:, :, None, :] + p_im[d][:, :, taus][..., None] * xb_re[d][:, :, None, :]
        e = jnp.stack([e_re, e_im], axis=0)
        e = e.reshape(2, S5_LB, 8, S5_STATE, T, S5_GC)
        e = jnp.transpose(e, (1, 4, 2, 5, 0, 3))
        return expand(e.reshape(S5_LB, wide, LANES), exp_pn, row_grp_lane == st_idx(1))

    m_f_w = state_in(0, np.arange(T - 1, -1, -1))
    m_b_w = state_in(1, np.arange(T))

    def state_out(d, taus):
        r = cp_re[d][..., taus]
        im = -cp_im[d][..., taus]
        w = jnp.stack([r, im], axis=0).reshape(2, S5_LB, 4, 2, S5_GC, S5_STATE, T)
        w = jnp.transpose(w, (1, 2, 0, 3, 5, 6, 4))
        return expand(w.reshape(S5_LB, S5_SW, LANES), exp_tk, st_idx(0) == col_grp_lane)

    n_f_w = state_out(0, np.arange(1, T + 1))
    n_b_w = state_out(1, np.arange(T, 0, -1))

    q_re, q_im = powers(T * np.arange(8))

    def table(arr_re, arr_im, d, order):
        t = jnp.stack([arr_re[d][..., order], arr_im[d][..., order]], axis=0)
        t = t.reshape(2, 16, 2, S5_STATE, len(order))
        return jnp.transpose(t, (4, 1, 0, 2, 3)).reshape(len(order), 16 * 2 * LANES)

    asc = np.arange(8)
    dbl = np.array([1, 2, 4, 0, 0, 0, 0, 0])
    tabs = (table(q_re, q_im, 0, asc), table(q_re, q_im, 0, dbl),
            table(q_re, q_im, 1, asc[::-1]), table(q_re, q_im, 1, dbl))
    return (a_blk.astype(bf16), m_f_w.astype(bf16), m_b_w.astype(bf16), n_f_w.astype(bf16), n_b_w.astype(bf16),
            tuple(t.astype(f32) for t in tabs))


def _s5_load_chunks(u_ref):
    rows = u_ref.shape[0] // S5_T
    parts = [u_ref[pl.ds(s, rows, stride=S5_T), :] for s in range(S5_T)]
    return jnp.concatenate(parts, axis=1).astype(bf16)


def _s5_in_body(u_ref, mf_ref, mb_ref, xf_ref, xb_ref):
    lhs = _s5_load_chunks(u_ref)
    xf_ref[...] = _mm(lhs, mf_ref[...])
    xb_ref[...] = _mm(lhs, mb_ref[...])


def _s5_in(proj, m_f_w, m_b_w, tl):
    b, l, _ = proj.shape
    nc = l // S5_T
    out = jax.ShapeDtypeStruct((b, nc, S5_LB * S5_SW), f32)
    wspec = pl.BlockSpec((None, S5_T * LANES, S5_SW), lambda bi, j, t: (j, 0, 0))
    ospec = pl.BlockSpec((None, tl // S5_T, S5_SW), lambda bi, j, t: (bi, t, j))
    return pl.pallas_call(
        _s5_in_body, name="s5_in",
        out_shape=(out, out),
        grid=(b, S5_LB, l // tl),
        in_specs=[pl.BlockSpec((None, tl, LANES), lambda bi, j, t: (bi, t, B_S5 // LANES + j)), wspec, wspec],
        out_specs=(ospec, ospec),
        compiler_params=_cparams("parallel", "parallel", "parallel"),
    )(proj, m_f_w, m_b_w)


def _s5_scan_body(xf_ref, xb_ref, cf_ref, hf_ref, cb_ref, hb_ref, sf_ref, sb_ref):
    n8 = xf_ref.shape[0] // SUBLANES
    ncb = xf_ref.shape[1] // (2 * LANES)
    rows = lax.broadcasted_iota(i32, (SUBLANES, LANES), 0)

    def shifted(x, d, fwd):
        if fwd:
            return jnp.where(rows >= d, pltpu.roll(x, d, 0), 0.0)
        return jnp.where(rows < SUBLANES - d, pltpu.roll(x, SUBLANES - d, 0), 0.0)

    def local_scan(xr, xi, h_ref, re, im, fwd):
        er, ei = shifted(xr, 1, fwd), shifted(xi, 1, fwd)
        for k, d in enumerate((1, 2, 4)):
            ar, ai = h_ref[k:k + 1, re], h_ref[k:k + 1, im]
            sr, si = shifted(er, d, fwd), shifted(ei, d, fwd)
            er, ei = er + ar * sr - ai * si, ei + ar * si + ai * sr
        return er, ei

    def tile(x_ref, c_ref, h_ref, o_ref, i, c, sr, si, fwd):
        re = slice(2 * c * LANES, (2 * c + 1) * LANES)
        im = slice((2 * c + 1) * LANES, (2 * c + 2) * LANES)
        r0 = pl.multiple_of(i * SUBLANES, SUBLANES)
        xr, xi = x_ref[pl.ds(r0, SUBLANES), re], x_ref[pl.ds(r0, SUBLANES), im]
        er, ei = local_scan(xr, xi, h_ref, re, im, fwd)
        cr, ci = c_ref[:, re], c_ref[:, im]
        outr = er + cr * sr - ci * si
        outi = ei + cr * si + ci * sr
        o_ref[pl.ds(r0, SUBLANES), re] = outr
        o_ref[pl.ds(r0, SUBLANES), im] = outi
        e = SUBLANES - 1 if fwd else 0
        ar, ai = h_ref[0:1, re], h_ref[0:1, im]
        nr = ar * outr[e:e + 1] - ai * outi[e:e + 1] + xr[e:e + 1]
        ni = ar * outi[e:e + 1] + ai * outr[e:e + 1] + xi[e:e + 1]
        return nr, ni

    def step(i, carry):
        out = []
        for c in range(ncb):
            fr, fi, br, bi = carry[4 * c:4 * c + 4]
            fr, fi = tile(xf_ref, cf_ref, hf_ref, sf_ref, i, c, fr, fi, True)
            br, bi = tile(xb_ref, cb_ref, hb_ref, sb_ref, n8 - 1 - i, c, br, bi, False)
            out += [fr, fi, br, bi]
        return tuple(out)

    z = jnp.zeros((1, LANES), f32)
    lax.fori_loop(0, n8, step, (z,) * (4 * ncb))


def _s5_scan(xf, xb, tabs):
    b, nc, w = xf.shape
    bw = S5_SCAN_CB * 2 * LANES
    xspec = pl.BlockSpec((None, nc, bw), lambda bi, c: (bi, 0, c))
    tspec = pl.BlockSpec((SUBLANES, bw), lambda bi, c: (0, c))
    out = jax.ShapeDtypeStruct((b, nc, w), f32)
    return pl.pallas_call(
        _s5_scan_body, name="s5_scan",
        out_shape=(out, out),
        grid=(b, w // bw),
        in_specs=[xspec, xspec, tspec, tspec, tspec, tspec],
        out_specs=(xspec, xspec),
        compiler_params=_cparams("parallel", "parallel"),
    )(xf, xb, *tabs)


def _s5_out_body(u_ref, sf_ref, sb_ref, a_ref, nf_ref, nb_ref, y_ref):
    lhs = _s5_load_chunks(u_ref)
    y = (_mm(lhs, a_ref[...]) + _mm(sf_ref[...].astype(bf16), nf_ref[...])
         + _mm(sb_ref[...].astype(bf16), nb_ref[...]))
    rows = y.shape[0]
    for t in range(S5_T):
        y_ref[pl.ds(t, rows, stride=S5_T), :] = y[:, t * LANES:(t + 1) * LANES]


def _s5_out(proj, sf, sb, a_blk, n_f_w, n_b_w, tl):
    b, l, _ = proj.shape
    sspec = pl.BlockSpec((None, tl // S5_T, S5_SW), lambda bi, j, t: (bi, t, j))
    return pl.pallas_call(
        _s5_out_body, name="s5_out",
        out_shape=jax.ShapeDtypeStruct((b, l, BW), f32),
        grid=(b, S5_LB, l // tl),
        in_specs=[pl.BlockSpec((None, tl, LANES), lambda bi, j, t: (bi, t, B_S5 // LANES + j)), sspec, sspec,
                  pl.BlockSpec((None, S5_T * LANES, S5_T * LANES), lambda bi, j, t: (j, 0, 0)),
                  pl.BlockSpec((None, S5_SW, S5_T * LANES), lambda bi, j, t: (j, 0, 0)),
                  pl.BlockSpec((None, S5_SW, S5_T * LANES), lambda bi, j, t: (j, 0, 0))],
        out_specs=pl.BlockSpec((None, tl, LANES), lambda bi, j, t: (bi, t, j)),
        compiler_params=_cparams("parallel", "parallel", "parallel"),
    )(proj, sf, sb, a_blk, n_f_w, n_b_w)


def _s5_mixer(proj, s5w, tl=2048):
    a_blk, m_f_w, m_b_w, n_f_w, n_b_w, tabs = s5w
    tl = min(tl, proj.shape[1])
    xf, xb = _s5_in(proj, m_f_w, m_b_w, tl)
    sf, sb = _s5_scan(xf, xb, tabs)
    return _s5_out(proj, sf, sb, a_blk, n_f_w, n_b_w, tl)


def _gla_body(qf_ref, kf_ref, vf_ref, lrf_ref, qb_ref, kb_ref, vb_ref, lrb_ref, wgf_ref, bgf_ref, wgb_ref, bgb_ref,
              of_ref, ob_ref, st_ref):
    ins = ((qf_ref, kf_ref, vf_ref, lrf_ref, wgf_ref, bgf_ref, of_ref),
           (qb_ref, kb_ref, vb_ref, lrb_ref, wgb_ref, bgb_ref, ob_ref))
    nch = qf_ref.shape[0] // CHUNK
    hk = GLA_H * GLA_DK
    hv = GLA_H * GLA_DV

    @pl.when(pl.program_id(1) == 0)
    def _():
        st_ref[...] = jnp.zeros_like(st_ref)

    r64 = lax.broadcasted_iota(i32, (CHUNK, CHUNK), 0)
    c64 = lax.broadcasted_iota(i32, (CHUNK, CHUNK), 1)
    tris = (_ind(r64 >= c64), _ind(r64 <= c64))
    rr = lax.broadcasted_iota(i32, (CHUNK, hk), 0)
    cc = lax.broadcasted_iota(i32, (CHUNK, hk), 1) % CHUNK
    causals = (rr >= cc, rr <= cc)
    kmask = (lax.broadcasted_iota(i32, (hk, hk), 0) // CHUNK
             == lax.broadcasted_iota(i32, (hk, hk), 1) // GLA_DK)
    vmask = (lax.broadcasted_iota(i32, (hk, hv), 0) // CHUNK
             == lax.broadcasted_iota(i32, (hk, hv), 1) // GLA_DV)
    smask = (lax.broadcasted_iota(i32, (hv, hk), 0) // GLA_DV
             == lax.broadcasted_iota(i32, (hv, hk), 1) // GLA_DK)
    scale = GLA_DK ** -0.5
    zero = jnp.zeros((), bf16)

    def group(gi, carry):
        span = GLA_GROUP * CHUNK
        bases = (gi * span, (nch - (gi + 1) * GLA_GROUP) * CHUNK)
        gls = [_mm_hi(ins[d][3][pl.ds(pl.multiple_of(bases[d], span), span), :], ins[d][4][...]) + ins[d][5][...]
               for d in range(2)]
        chains = []
        for j in range(GLA_GROUP):
            for d in range(2):
                cj = gi * GLA_GROUP + j
                c = cj if d == 0 else nch - 1 - cj
                lo = (j if d == 0 else GLA_GROUP - 1 - j) * CHUNK
                chains.append(dict(d=d, rows=pl.ds(pl.multiple_of(c * CHUNK, CHUNK), CHUNK),
                                   gl=gls[d][lo:lo + CHUNK, :]))
        for ch in chains:
            gl = ch['gl']
            g = (jnp.minimum(gl, 0.0) - jnp.log(1.0 + jnp.exp(-jnp.abs(gl)))) * (1.0 / GLA_TAU)
            ch['gc'] = _mm_mask(tris[ch['d']], g)
        for ch in chains:
            r, rows, gc = ins[ch['d']], ch['rows'], ch['gc']
            gtot = gc[CHUNK - 1:CHUNK, :] if ch['d'] == 0 else gc[0:1, :]
            k = r[1][rows, :].astype(f32)
            ch['qd'] = (r[0][rows, :].astype(f32) * scale * jnp.exp(gc)).astype(bf16)
            ki = (k * jnp.exp(-gc)).astype(bf16)
            ch['kt'] = (k * jnp.exp(gtot - gc)).astype(bf16)
            ch['dec'] = jnp.exp(gtot)
            kstack = jnp.where(kmask, jnp.concatenate([ki] * GLA_H, axis=0), zero)
            ch['sc'] = jnp.where(causals[ch['d']], _nt(ch['qd'], kstack), 0.0).astype(bf16)
        for ch in chains:
            v = ins[ch['d']][2][ch['rows'], :].astype(bf16)
            vbd = jnp.where(vmask, jnp.concatenate([v] * GLA_H, axis=0), zero)
            ch['oi'] = _mm(ch['sc'], vbd)
            ch['kv'] = _tn(v, ch['kt'])
        for ch in chains:
            d = ch['d']
            st = st_ref[d]
            ins[d][6][ch['rows'], :] = (ch['oi'] + _nt(ch['qd'], st.astype(bf16))).astype(bf16)
            st_ref[d] = st * ch['dec'] + jnp.where(smask, ch['kv'], 0.0)
        return carry

    lax.fori_loop(0, nch // GLA_GROUP, group, 0)


def _gla(pa, pb, wgs, bgs, blk=512):
    b, l, _ = pa.shape
    blk = min(blk, l)
    nb = l // blk
    assert l % blk == 0 and (blk // CHUNK) % GLA_GROUP == 0
    hk, hv = GLA_H * GLA_DK, GLA_H * GLA_DV
    up, down = (lambda i: i), (lambda i: nb - 1 - i)

    def cols(bidx):
        col = lambda width, off: pl.BlockSpec((None, blk, width), lambda bi, i: (bi, bidx(i), off // width))
        return [col(hk, A_GLA_Q), col(hk, A_GLA_K), col(hv, A_GLA_V), col(LANES, B_GLA_LR)]

    wspec = pl.BlockSpec((LANES, hk), lambda bi, i: (0, 0))
    bspec = pl.BlockSpec((1, hk), lambda bi, i: (0, 0))
    out = jax.ShapeDtypeStruct((b, l, hv), bf16)
    return pl.pallas_call(
        _gla_body, name="gla",
        out_shape=(out, out),
        grid=(b, nb),
        in_specs=cols(up) + cols(down) + [wspec, bspec, wspec, bspec],
        out_specs=(pl.BlockSpec((None, blk, hv), lambda bi, i: (bi, up(i), 0)),
                   pl.BlockSpec((None, blk, hv), lambda bi, i: (bi, down(i), 0))),
        scratch_shapes=[pltpu.VMEM((2, hv, hk), f32)],
        compiler_params=_cparams("parallel", "arbitrary"),
    )(pa, pa, pa, pb, pa, pa, pa, pb, wgs[0], bgs[0], wgs[1], bgs[1])


def _gdn_prep_body(x_ref, xp_ref, xn_ref, ab_ref, cw_ref, par_ref, qkv_ref, gb_ref):
    i, n = pl.program_id(1), pl.num_programs(1)
    blk = x_ref.shape[0]
    halo = xp_ref.shape[0]
    prev = jnp.where(i > 0, xp_ref[...].astype(f32), 0.0)
    nxt = jnp.where(i < n - 1, xn_ref[...].astype(f32), 0.0)
    ext = jnp.concatenate([prev, x_ref[...].astype(f32), nxt], axis=0)
    tot = blk + 2 * halo
    acc = None
    for t in range(CONV_W):
        sh = (CONV_W // 2 - t) % tot
        xs = ext if sh == 0 else pltpu.roll(ext, sh, 0)
        term = xs[halo:halo + blk, :] * cw_ref[t:t + 1, :]
        acc = term if acc is None else acc + term
    y = _silu(acc)
    nqk = 2 * GDN_H
    for h in range(3 * GDN_H):
        sl = slice(h * LANES, (h + 1) * LANES)
        yh = y[:, sl]
        if h < nqk:
            yh = yh * lax.rsqrt(jnp.sum(yh * yh, axis=-1, keepdims=True) + EPS)
            if h < GDN_H:
                yh = yh * GDN_DK ** -0.5
        qkv_ref[:, sl] = yh.astype(qkv_ref.dtype)
    x = ab_ref[...]
    lane = lax.broadcasted_iota(i32, x.shape, 1)
    xa = x + par_ref[1:2, :]
    softplus = jnp.maximum(xa, 0.0) + jnp.log(1.0 + jnp.exp(-jnp.abs(xa)))
    gb_ref[...] = jnp.where(lane < nqk, par_ref[0:1, :] * softplus, _sigmoid(x))


def _gdn_prep(pa, pb, conv_w, par, blk=256):
    b, l, _ = pa.shape
    blk = min(blk, l)
    nb = l // blk
    w = 3 * BW
    halo = 2 * SUBLANES
    rh = blk // halo
    last = l // halo - 1
    return pl.pallas_call(
        _gdn_prep_body, name="gdn_prep",
        out_shape=(jax.ShapeDtypeStruct((b, l, w), bf16), jax.ShapeDtypeStruct((b, l, LANES), f32)),
        grid=(b, nb),
        in_specs=[pl.BlockSpec((None, blk, w), lambda bi, i: (bi, i, A_GDN_QKV // w)),
                  pl.BlockSpec((None, halo, w), lambda bi, i: (bi, jnp.maximum(i * rh - 1, 0), A_GDN_QKV // w)),
                  pl.BlockSpec((None, halo, w), lambda bi, i: (bi, jnp.minimum((i + 1) * rh, last), A_GDN_QKV // w)),
                  pl.BlockSpec((None, blk, LANES), lambda bi, i: (bi, i, B_GDN_AB // LANES)),
                  pl.BlockSpec((SUBLANES, w), lambda bi, i: (0, 0)),
                  pl.BlockSpec((SUBLANES, LANES), lambda bi, i: (0, 0))],
        out_specs=(pl.BlockSpec((None, blk, w), lambda bi, i: (bi, i, 0)),
                   pl.BlockSpec((None, blk, LANES), lambda bi, i: (bi, i, 0))),
        compiler_params=_cparams("parallel", "parallel"),
    )(pa, pa, pa, pb, conv_w, par)


def _unit_tri_inverses(lws, eye, bd16):
    lds = [jnp.where(bd16, lw, 0.0) for lw in lws]
    los = [lw - ld for lw, ld in zip(lws, lds)]
    ps = [eye - ld for ld in lds]
    pw = lds
    for _ in range(3):
        pw = [_mmb(x, x) for x in pw]
        ps = [p + _mmb(p, x) for p, x in zip(ps, pw)]
    ms = [_mmb(p, lo) for p, lo in zip(ps, los)]
    m2s = [_mmb(m, m) for m in ms]
    qs = [eye - m for m in ms]
    qs = [q + _mmb(q, m2) for q, m2 in zip(qs, m2s)]
    return [_mmb(q, p) for q, p in zip(qs, ps)]


def _gdn_par_body(q_ref, k_ref, v_ref, gb_ref, *out_refs):
    nch = q_ref.shape[0] // CHUNK
    r64 = lax.broadcasted_iota(i32, (CHUNK, CHUNK), 0)
    c64 = lax.broadcasted_iota(i32, (CHUNK, CHUNK), 1)
    eye = _ind(r64 == c64)
    bd16 = (r64 // 16) == (c64 // 16)
    incls = (r64 >= c64, r64 <= c64)
    stricts = (r64 > c64, r64 < c64)
    tris = tuple(_ind(m) for m in incls)

    def chunk_group(ci, carry):
        chains = []
        for j in range(GDN_GROUP):
            c = ci * GDN_GROUP + j
            rows = pl.ds(pl.multiple_of(c * CHUNK, CHUNK), CHUNK)
            gcols = gb_ref[rows, :]
            gams = tuple(_mm_hi(t, gcols) for t in tris)
            gam_ts = tuple(g.T for g in gams)
            for d in range(2):
                out_refs[6 * d + 5][c] = jnp.exp(gams[d][CHUNK - 1:CHUNK, :] if d == 0 else gams[d][0:1, :])
            for h in range(GDN_H):
                sl = slice(h * LANES, (h + 1) * LANES)
                qb, kb16 = q_ref[rows, sl], k_ref[rows, sl]
                raw = _nt(jnp.concatenate([qb, kb16], axis=0), kb16)
                qh, kh, vh = qb.astype(f32), kb16.astype(f32), v_ref[rows, sl].astype(f32)
                for d in range(2):
                    lg = d * GDN_H + h
                    chains.append(dict(rows=rows, h=h, d=d, qh=qh, kh=kh, vh=vh, raw=raw,
                                       gcol=gams[d][:, lg:lg + 1], grow=gam_ts[d][lg:lg + 1, :],
                                       beta=gcols[:, 2 * GDN_H + lg:2 * GDN_H + lg + 1]))
        lws = []
        for ch in chains:
            d, rows, h = ch['d'], ch['rows'], ch['h']
            dec = jnp.where(incls[d], jnp.exp(jnp.where(incls[d], ch['gcol'] - ch['grow'], 0.0)), 0.0)
            out_refs[6 * d + 4][rows, h * CHUNK:(h + 1) * CHUNK] = (ch['raw'][:CHUNK] * dec).astype(bf16)
            lws.append(jnp.where(stricts[d], ch['raw'][CHUNK:] * dec * ch['beta'], 0.0))
        tinvs = _unit_tri_inverses(lws, eye, bd16)
        sols = []
        for ch, tinv in zip(chains, tinvs):
            ch['eg'] = jnp.exp(ch['gcol'])
            kb = ch['kh'] * ch['beta']
            sols.append(_mmb(tinv, jnp.concatenate([ch['vh'] * ch['beta'], kb * ch['eg']], axis=1)))
        for ch, sol in zip(chains, sols):
            d, rows = ch['d'], ch['rows']
            sl = slice(ch['h'] * LANES, (ch['h'] + 1) * LANES)
            u_ref, w_ref, qd_ref, kt_ref = out_refs[6 * d:6 * d + 4]
            gcol = ch['gcol']
            gtot = gcol[CHUNK - 1:CHUNK, :] if d == 0 else gcol[0:1, :]
            u_ref[rows, sl] = sol[:, :GDN_DV]
            w_ref[rows, sl] = sol[:, GDN_DV:].astype(bf16)
            qd_ref[rows, sl] = (ch['qh'] * ch['eg']).astype(bf16)
            kt_ref[rows, sl] = (ch['kh'] * jnp.exp(gtot - gcol)).astype(bf16)
        return carry

    lax.fori_loop(0, nch // GDN_GROUP, chunk_group, 0)


def _gdn_par(qkv, gb, blk=512):
    b, l, _ = qkv.shape
    blk = min(blk, l)
    nch = blk // CHUNK
    assert l % blk == 0 and nch % GDN_GROUP == 0
    col = lambda j: pl.BlockSpec((None, blk, BW), lambda bi, i: (bi, i, j))
    wide = pl.BlockSpec((None, blk, BW), lambda bi, i: (bi, i, 0))
    one_dir_shapes = (jax.ShapeDtypeStruct((b, l, BW), f32), jax.ShapeDtypeStruct((b, l, BW), bf16),
                      jax.ShapeDtypeStruct((b, l, BW), bf16), jax.ShapeDtypeStruct((b, l, BW), bf16),
                      jax.ShapeDtypeStruct((b, l, GDN_H * CHUNK), bf16),
                      jax.ShapeDtypeStruct((b, l // CHUNK, 1, LANES), f32))
    one_dir_specs = (wide, wide, wide, wide,
                     pl.BlockSpec((None, blk, GDN_H * CHUNK), lambda bi, i: (bi, i, 0)),
                     pl.BlockSpec((None, nch, 1, LANES), lambda bi, i: (bi, i, 0, 0)))
    outs = pl.pallas_call(
        _gdn_par_body, name="gdn_par",
        out_shape=one_dir_shapes * 2,
        grid=(b, l // blk),
        in_specs=[col(0), col(1), col(2), pl.BlockSpec((None, blk, LANES), lambda bi, i: (bi, i, 0))],
        out_specs=one_dir_specs * 2,
        compiler_params=_cparams("parallel", "parallel"),
    )(qkv, qkv, qkv, gb)
    return outs[:6], outs[6:]


def _gdn_seq_body(*refs):
    ins = (refs[0:6], refs[6:12])
    o_refs = refs[12:14]
    s_ref = refs[14]
    nseq = o_refs[0].shape[0]
    nch = o_refs[0].shape[1] // CHUNK

    @pl.when(pl.program_id(0) == 0)
    def _():
        s_ref[...] = jnp.zeros_like(s_ref)

    def chunk(ci, carry):
        rows, cidx = [], []
        for d in range(2):
            c = ci if d == 0 else nch - 1 - ci
            cidx.append(c)
            rows.append(pl.ds(pl.multiple_of(c * CHUNK, CHUNK), CHUNK))
        ch = [(bi, d, h) for bi in range(nseq) for d in range(2) for h in range(GDN_H)]
        sl = lambda h: slice(h * LANES, (h + 1) * LANES)
        ss = [s_ref[bi, d, sl(h), :] for bi, d, h in ch]
        sbs = [s.astype(bf16) for s in ss]
        wss = [_mm(ins[d][1][bi, rows[d], sl(h)], sb) for (bi, d, h), sb in zip(ch, sbs)]
        qss = [_mm(ins[d][2][bi, rows[d], sl(h)], sb) for (bi, d, h), sb in zip(ch, sbs)]
        vns = [(ins[d][0][bi, rows[d], sl(h)] - ws).astype(bf16) for (bi, d, h), ws in zip(ch, wss)]
        avs = [_mm(ins[d][4][bi, rows[d], h * CHUNK:(h + 1) * CHUNK], vn) for (bi, d, h), vn in zip(ch, vns)]
        kvs = [_tn(ins[d][3][bi, rows[d], sl(h)], vn) for (bi, d, h), vn in zip(ch, vns)]
        for (bi, d, h), s, kv in zip(ch, ss, kvs):
            lg = d * GDN_H + h
            cd = ins[d][5][bi, cidx[d]]
            s_ref[bi, d, sl(h), :] = s * cd[:, lg:lg + 1] + kv
        for bi in range(nseq):
            for d in range(2):
                o_refs[d][bi, rows[d], :] = jnp.concatenate(
                    [qs + av for (b2, d2, _), qs, av in zip(ch, qss, avs) if (b2, d2) == (bi, d)],
                    axis=1).astype(bf16)
        return carry

    lax.fori_loop(0, nch, chunk, 0)


def _gdn_seq(fwd_in, bwd_in, blk=512):
    b, l, _ = fwd_in[0].shape
    blk = min(blk, l)
    nb = l // blk
    nch = blk // CHUNK

    def specs(bidx):
        wide = pl.BlockSpec((b, blk, BW), lambda i: (0, bidx(i), 0))
        return [wide, wide, wide, wide,
                pl.BlockSpec((b, blk, GDN_H * CHUNK), lambda i: (0, bidx(i), 0)),
                pl.BlockSpec((b, nch, 1, LANES), lambda i: (0, bidx(i), 0, 0))]

    up, down = (lambda i: i), (lambda i: nb - 1 - i)
    out = jax.ShapeDtypeStruct((b, l, BW), bf16)
    return pl.pallas_call(
        _gdn_seq_body, name="gdn_seq",
        out_shape=(out, out),
        grid=(nb,),
        in_specs=specs(up) + specs(down),
        out_specs=(pl.BlockSpec((b, blk, BW), lambda i: (0, up(i), 0)),
                   pl.BlockSpec((b, blk, BW), lambda i: (0, down(i), 0))),
        scratch_shapes=[pltpu.VMEM((b, 2, GDN_H * GDN_DK, GDN_DV), f32)],
        compiler_params=_cparams("arbitrary"),
    )(*fwd_in, *bwd_in)


def _head_norm_gate(o, gain, z):
    outs = []
    for h in range(BW // LANES):
        oh = o[:, h * LANES:(h + 1) * LANES]
        outs.append(oh * lax.rsqrt(jnp.mean(oh * oh, axis=-1, keepdims=True) + EPS) * gain)
    return jnp.concatenate(outs, axis=1) * _silu(z)


def _merge_body(h_ref, gate_ref, ch_ref, u_ref, glaf_ref, glab_ref, r_ref, gdnf_ref, gdnb_ref, z_ref,
                dsk_ref, wglu_ref, gng_ref, dng_ref, wbr_ref, wout_ref, o_ref):
    up = lambda ref: ref[...].astype(f32)
    y0 = _gelu_tanh(ch_ref[...] + dsk_ref[...] * u_ref[...])
    y_s5 = y0 * _sigmoid(_mm(y0.astype(bf16), wglu_ref[...]))
    y_gla = _head_norm_gate(up(glaf_ref) + up(glab_ref), gng_ref[...], up(r_ref))
    y_gdn = _head_norm_gate(up(gdnf_ref) + up(gdnb_ref), dng_ref[...], up(z_ref))
    merged = None
    for r, y in enumerate((y_s5, y_gla, y_gdn)):
        gate = _sigmoid(gate_ref[:, r * D_MODEL:(r + 1) * D_MODEL].astype(f32))
        term = gate * _mm(y.astype(bf16), wbr_ref[r])
        merged = term if merged is None else merged + term
    o_ref[...] = h_ref[...] + _mm(merged.astype(bf16), wout_ref[...])


def _merge(h2d, pa2d, pb2d, ch, gla_f, gla_b, gdn_f, gdn_b, dsk, wglu, gng, dng, wbr, wout, tm=256):
    m = h2d.shape[0]
    row = lambda width, off=0: pl.BlockSpec((tm, width), lambda i: (i, off // width))
    full = lambda shape: pl.BlockSpec(shape, lambda i: (0,) * len(shape))
    return pl.pallas_call(
        _merge_body, name="merge",
        out_shape=jax.ShapeDtypeStruct((m, D_MODEL), f32),
        grid=(m // tm,),
        in_specs=[row(D_MODEL), row(3 * D_MODEL, A_GATE), row(BW), row(BW, B_S5), row(BW), row(BW),
                  row(BW, A_GLA_R), row(BW), row(BW), row(BW, A_GDN_Z),
                  full((1, BW)), full((BW, BW)), full((1, LANES)), full((1, LANES)),
                  full((3, BW, D_MODEL)), full((D_MODEL, D_MODEL))],
        out_specs=row(D_MODEL),
        compiler_params=_cparams("parallel"),
    )(h2d, pa2d, ch, pb2d, gla_f, gla_b, pa2d, gdn_f, gdn_b, pa2d, dsk, wglu, gng, dng, wbr, wout)


def _router_body(h_ref, g_ref, wr_ref, hn_ref, aff_ref):
    hn = _rms(h_ref[...], g_ref[...])
    hn_ref[...] = hn.astype(bf16)
    logits = lax.dot_general(wr_ref[...], hn, (((1,), (1,)), ((), ())), preferred_element_type=f32,
                             precision=HIGHEST)
    e = jnp.exp(logits - jnp.max(logits, axis=0, keepdims=True))
    aff_ref[...] = e / jnp.sum(e, axis=0, keepdims=True)


def _router(h2d, gain, wr_t, tm=512):
    m = h2d.shape[0]
    return pl.pallas_call(
        _router_body, name="router",
        out_shape=(jax.ShapeDtypeStruct((m, D_MODEL), bf16), jax.ShapeDtypeStruct((N_EXPERTS, m), f32)),
        grid=(m // tm,),
        in_specs=[pl.BlockSpec((tm, D_MODEL), lambda i: (i, 0)),
                  pl.BlockSpec((1, D_MODEL), lambda i: (0, 0)),
                  pl.BlockSpec((N_EXPERTS, D_MODEL), lambda i: (0, 0))],
        out_specs=(pl.BlockSpec((tm, D_MODEL), lambda i: (i, 0)), pl.BlockSpec((N_EXPERTS, tm), lambda i: (0, i))),
        compiler_params=_cparams("parallel"),
    )(h2d, gain, wr_t)


def _threshold_body(aff_ref, thr_ref, *, cap):
    keys = pltpu.bitcast(aff_ref[...], i32)

    def count(mask):
        return jnp.sum(jnp.where(mask, 1.0, 0.0), axis=1, keepdims=True).astype(i32)

    def bit(bi, t):
        cand = t | (1 << (30 - bi))
        return jnp.where(count(keys >= cand) >= cap, cand, t)

    t = lax.fori_loop(0, 31, bit, jnp.zeros((N_EXPERTS, 1), i32))
    budget = cap - count(keys > t)
    lane = lax.broadcasted_iota(i32, (N_EXPERTS, LANES), 1)
    thr_ref[...] = jnp.where(lane == 0, t, jnp.where(lane == 1, budget, 0))


def _threshold(aff_t, cap):
    n = aff_t.shape[1]
    return pl.pallas_call(
        functools.partial(_threshold_body, cap=cap), name="topc_threshold",
        out_shape=jax.ShapeDtypeStruct((N_EXPERTS, LANES), i32),
        in_specs=[pl.BlockSpec((N_EXPERTS, n), lambda: (0, 0))],
        out_specs=pl.BlockSpec((N_EXPERTS, LANES), lambda: (0, 0)),
        compiler_params=pltpu.CompilerParams(vmem_limit_bytes=VMEM_LIMIT_BYTES),
    )(aff_t)


def _slots_body(aff_ref, thr_ref, slot_ref, wts_ref, cnt_ref, run_ref):
    @pl.when(pl.program_id(0) == 0)
    def _():
        run_ref[...] = jnp.zeros_like(run_ref)

    tt = aff_ref.shape[1]
    aff = aff_ref[...]
    keys = pltpu.bitcast(aff, i32)
    t = thr_ref[:, 0:1]
    budget = thr_ref[:, 1:2]
    upper = _ind(lax.broadcasted_iota(i32, (tt, tt), 0) <= lax.broadcasted_iota(i32, (tt, tt), 1), bf16)
    eq = keys == t
    sel_run = run_ref[:, 0:1]
    tie_run = run_ref[:, 1:2]
    cs_eq = _mm(_ind(eq, bf16), upper).astype(i32)
    tie_rank = tie_run + cs_eq - 1
    sel = (keys > t) | (eq & (tie_rank < budget))
    cs_sel = _mm(_ind(sel, bf16), upper).astype(i32)
    slot_ref[...] = jnp.where(sel, sel_run + cs_sel - 1, -1)
    wts_ref[...] = jnp.where(sel, aff, 0.0)
    n_sel = cs_sel[:, tt - 1:tt]
    n_eq = cs_eq[:, tt - 1:tt]
    n_rows = ((n_sel + (SUBLANES - 1)) // SUBLANES) * SUBLANES
    cnt_ref[...] = jnp.broadcast_to(n_rows, cnt_ref.shape)
    lane = lax.broadcasted_iota(i32, run_ref.shape, 1)
    run_ref[...] = run_ref[...] + jnp.where(lane == 0, n_rows, jnp.where(lane == 1, n_eq, 0))


def _slots(aff_t, thr):
    n = aff_t.shape[1]
    nt = n // MOE_TT
    return pl.pallas_call(
        _slots_body, name="topc_slots",
        out_shape=(jax.ShapeDtypeStruct((N_EXPERTS, n), i32), jax.ShapeDtypeStruct((N_EXPERTS, n), f32),
                   jax.ShapeDtypeStruct((nt, N_EXPERTS, LANES), i32)),
        grid=(nt,),
        in_specs=[pl.BlockSpec((N_EXPERTS, MOE_TT), lambda i: (0, i)),
                  pl.BlockSpec((N_EXPERTS, LANES), lambda i: (0, 0))],
        out_specs=(pl.BlockSpec((N_EXPERTS, MOE_TT), lambda i: (0, i)),
                   pl.BlockSpec((N_EXPERTS, MOE_TT), lambda i: (0, i)),
                   pl.BlockSpec((None, N_EXPERTS, LANES), lambda i: (i, 0, 0))),
        scratch_shapes=[pltpu.VMEM((N_EXPERTS, LANES), i32)],
        compiler_params=_cparams("arbitrary"),
    )(aff_t, thr)


def _window_hits(slot_rows, starts, width):
    tt = slot_rows.shape[1]
    r = lax.broadcasted_iota(i32, (width, tt), 0)
    return [r == (slot_rows[e:e + 1, :] - starts[e]) for e in range(len(starts))]


def _dispatch_body(off_ref, x_ref, slot_ref, xe_hbm, buf_ref, xbuf_ref, sem, *, cap):
    i = pl.program_id(0)
    w = MOE_W
    x = x_ref[...]
    slot_rows = slot_ref[...]
    starts = [off_ref[e, i] for e in range(N_EXPERTS)]

    def window_copy(e, start, src):
        return pltpu.make_async_copy(src, xe_hbm.at[e, pl.ds(pl.multiple_of(start, SUBLANES), w)], sem.at[e])

    def window_wait(e):
        pltpu.make_async_copy(buf_ref.at[e], xe_hbm.at[e, pl.ds(0, w)], sem.at[e]).wait()

    @pl.when(i > 0)
    def _():
        for e in range(N_EXPERTS):
            window_wait(e)

    @pl.when(i == 0)
    def _():
        xbuf_ref[...] = jnp.zeros_like(xbuf_ref)
        tails = [pltpu.make_async_copy(xbuf_ref, xe_hbm.at[e, pl.ds(r0, MOE_PAD)], sem.at[N_EXPERTS])
                 for e in range(N_EXPERTS) for r0 in range(cap, xe_hbm.shape[1], MOE_PAD)]
        for cp in tails:
            cp.start()
        for cp in tails:
            cp.wait()

    hits = _window_hits(slot_rows, starts, w)
    lhs = jnp.concatenate([_ind(h, bf16) for h in hits], axis=0)
    rows = _mm(lhs, x).astype(bf16)
    for e in range(N_EXPERTS):
        buf_ref[e] = rows[e * w:(e + 1) * w, :]
        window_copy(e, starts[e], buf_ref.at[e]).start()

    for e in range(N_EXPERTS):
        n_rows = off_ref[e, i + 1] - starts[e]

        def extra(k, carry, e=e):
            start = starts[e] + k * w
            hit = _window_hits(slot_rows[e:e + 1, :], [start], w)[0]
            xbuf_ref[0:w, :] = _mm(_ind(hit, bf16), x).astype(bf16)
            cp = pltpu.make_async_copy(xbuf_ref.at[pl.ds(0, w)],
                                       xe_hbm.at[e, pl.ds(pl.multiple_of(start, SUBLANES), w)], sem.at[N_EXPERTS])
            cp.start()
            cp.wait()
            return carry

        lax.fori_loop(1, (n_rows + (w - 1)) // w, extra, 0)

    @pl.when(i == pl.num_programs(0) - 1)
    def _():
        for e in range(N_EXPERTS):
            window_wait(e)


def _dispatch(hn, slot, off, rows_alloc, cap):
    n = hn.shape[0]
    grid_spec = pltpu.PrefetchScalarGridSpec(
        num_scalar_prefetch=1,
        grid=(n // MOE_TT,),
        in_specs=[pl.BlockSpec((MOE_TT, D_MODEL), lambda i, o: (i, 0)),
                  pl.BlockSpec((N_EXPERTS, MOE_TT), lambda i, o: (0, i))],
        out_specs=pl.BlockSpec(memory_space=pl.ANY),
        scratch_shapes=[pltpu.VMEM((N_EXPERTS, MOE_W, D_MODEL), bf16), pltpu.VMEM((MOE_PAD, D_MODEL), bf16),
                        pltpu.SemaphoreType.DMA((N_EXPERTS + 1,))])
    return pl.pallas_call(
        functools.partial(_dispatch_body, cap=cap), name="dispatch",
        out_shape=jax.ShapeDtypeStruct((N_EXPERTS, rows_alloc, D_MODEL), bf16),
        grid_spec=grid_spec,
        compiler_params=_cparams("arbitrary"),
    )(off, hn, slot)


def _experts_body(tot_ref, x_ref, wg_ref, wu_ref, wd_ref, y_ref):
    e, j = pl.program_id(0), pl.program_id(1)
    fb = x_ref.shape[0]
    n_valid = tot_ref[e] - j * fb

    def ffn(x):
        hid = (_silu(_mm(x, wg_ref[...])) * _mm(x, wu_ref[...])).astype(bf16)
        y_ref[...] = _mm(hid, wd_ref[...]).astype(y_ref.dtype)

    @pl.when(n_valid >= fb)
    def _():
        ffn(x_ref[...])

    @pl.when((n_valid > 0) & (n_valid < fb))
    def _():
        row = lax.broadcasted_iota(i32, (fb, 1), 0)
        ffn(jnp.where(row < n_valid, x_ref[...], jnp.zeros((), bf16)))

    @pl.when(n_valid <= 0)
    def _():
        y_ref[...] = jnp.zeros_like(y_ref)


def _experts(xe, total, wg, wu, wd, rows):
    grid_spec = pltpu.PrefetchScalarGridSpec(
        num_scalar_prefetch=1,
        grid=(N_EXPERTS, rows // MOE_FB),
        in_specs=[pl.BlockSpec((None, MOE_FB, D_MODEL), lambda e, j, t: (e, j, 0)),
                  pl.BlockSpec((None, D_MODEL, EXPERT_FF), lambda e, j, t: (e, 0, 0)),
                  pl.BlockSpec((None, D_MODEL, EXPERT_FF), lambda e, j, t: (e, 0, 0)),
                  pl.BlockSpec((None, EXPERT_FF, D_MODEL), lambda e, j, t: (e, 0, 0))],
        out_specs=pl.BlockSpec((None, MOE_FB, D_MODEL), lambda e, j, t: (e, j, 0)))
    return pl.pallas_call(
        _experts_body, name="experts",
        out_shape=jax.ShapeDtypeStruct((N_EXPERTS, rows, D_MODEL), bf16),
        grid_spec=grid_spec,
        compiler_params=_cparams("parallel", "arbitrary"),
    )(total, xe, wg, wu, wd)


def _combine_body(off_ref, h_ref, slot_ref, wts_ref, p_ref, g_ref, wpg_ref, wpp_ref, ye_hbm, o_ref,
                  win_ref, xwin_ref, acc_ref, sem, *, rows):
    i, n = pl.program_id(0), pl.num_programs(0)
    w = MOE_W
    tt = h_ref.shape[0]

    def wstart(e, tile, k=0):
        return pl.multiple_of(jnp.minimum(off_ref[e, tile] + k * w, rows - w), SUBLANES)

    def window_copy(e, tile, par):
        return pltpu.make_async_copy(ye_hbm.at[e, pl.ds(wstart(e, tile), w)], win_ref.at[par, e], sem.at[par, e])

    @pl.when(i == 0)
    def _():
        for e in range(N_EXPERTS):
            window_copy(e, 0, 0).start()

    @pl.when(i + 1 < n)
    def _():
        for e in range(N_EXPERTS):
            window_copy(e, i + 1, (i + 1) % 2).start()

    par = i % 2
    slot_rows = slot_ref[...]
    wts_rows = wts_ref[...]
    r = lax.broadcasted_iota(i32, (w, tt), 0)
    hits, wins = [], []
    for e in range(N_EXPERTS):
        window_copy(e, i, par).wait()
        s_row = slot_rows[e:e + 1, :]
        lo = off_ref[e, i]
        hit = (r == s_row - wstart(e, i)) & (s_row < lo + w)
        gsel = jnp.where(hit, wts_rows[e:e + 1, :], 0.0)
        part = gsel[:, 0:LANES]
        for c in range(1, tt // LANES):
            part = part + gsel[:, c * LANES:(c + 1) * LANES]
        slot_gate = jnp.sum(part, axis=1, keepdims=True)
        hits.append(_ind(hit, bf16))
        wins.append((win_ref[par, e].astype(f32) * slot_gate).astype(bf16))
    acc_ref[...] = h_ref[...] + _tn(jnp.concatenate(hits, axis=0), jnp.concatenate(wins, axis=0))
    gates = jnp.transpose(wts_rows)

    for e in range(N_EXPERTS):
        lo = off_ref[e, i]
        n_rows = off_ref[e, i + 1] - lo

        def extra(k, carry, e=e, lo=lo):
            cp = pltpu.make_async_copy(ye_hbm.at[e, pl.ds(wstart(e, i, k), w)], xwin_ref, sem.at[2, 0])
            cp.start()
            cp.wait()
            s_row = slot_rows[e:e + 1, :]
            hit = (r == s_row - wstart(e, i, k)) & (s_row >= lo + k * w) & (s_row < lo + (k + 1) * w)
            acc_ref[...] += gates[:, e:e + 1] * _tn(_ind(hit, bf16), xwin_ref[...])
            return carry

        lax.fori_loop(1, (n_rows + (w - 1)) // w, extra, 0)

    h2 = acc_ref[...]
    gate = _sigmoid(_mm(_rms(h2, g_ref[...]).astype(bf16), wpg_ref[...]))
    o_ref[...] = h2 + gate * _mm(p_ref[...].astype(bf16), wpp_ref[...])


def _combine(h2d, slot, wts, ye, p2d, off, g_ple, wpg, wpp):
    m = h2d.shape[0]
    rows = ye.shape[1]
    full = lambda shape: pl.BlockSpec(shape, lambda i, o: (0,) * len(shape))
    grid_spec = pltpu.PrefetchScalarGridSpec(
        num_scalar_prefetch=1,
        grid=(m // MOE_TT,),
        in_specs=[pl.BlockSpec((MOE_TT, D_MODEL), lambda i, o: (i, 0)),
                  pl.BlockSpec((N_EXPERTS, MOE_TT), lambda i, o: (0, i)),
                  pl.BlockSpec((N_EXPERTS, MOE_TT), lambda i, o: (0, i)),
                  pl.BlockSpec((MOE_TT, PLE_DIM), lambda i, o: (i, 0)),
                  full((1, D_MODEL)), full((D_MODEL, D_MODEL)), full((PLE_DIM, D_MODEL)),
                  pl.BlockSpec(memory_space=pl.ANY)],
        out_specs=pl.BlockSpec((MOE_TT, D_MODEL), lambda i, o: (i, 0)),
        scratch_shapes=[pltpu.VMEM((2, N_EXPERTS, MOE_W, D_MODEL), bf16), pltpu.VMEM((MOE_W, D_MODEL), bf16),
                        pltpu.VMEM((MOE_TT, D_MODEL), f32), pltpu.SemaphoreType.DMA((3, N_EXPERTS))])
    return pl.pallas_call(
        functools.partial(_combine_body, rows=rows), name="combine_ple",
        out_shape=jax.ShapeDtypeStruct((m, D_MODEL), f32),
        grid_spec=grid_spec,
        compiler_params=_cparams("arbitrary"),
    )(off, h2d, slot, wts, p2d, g_ple, wpg, wpp, ye)


def _moe_ple(h2d, p2d, g_ffn, wr_t, wg, wu, wd, g_ple, wpg, wpp):
    n = h2d.shape[0]
    nt = n // MOE_TT
    cap = max(1, EC_FACTOR * n // N_EXPERTS)
    rows = -(-(cap + SUBLANES * nt) // MOE_FB) * MOE_FB
    assert n % MOE_TT == 0 and cap % MOE_PAD == 0 and MOE_W <= MOE_PAD and MOE_W <= cap
    hn, aff_t = _router(h2d, g_ffn, wr_t)
    thr = _threshold(aff_t, cap)
    slot, wts, cnt = _slots(aff_t, thr)
    off = jnp.concatenate([jnp.zeros((1, N_EXPERTS), i32), jnp.cumsum(cnt[:, :, 0], axis=0, dtype=i32)], axis=0)
    off = jnp.transpose(off)
    xe = _dispatch(hn, slot, off, rows + MOE_PAD, cap)
    ye = _experts(xe, off[:, nt], wg, wu, wd, rows)
    return _combine(h2d, slot, wts, ye, p2d, off, g_ple, wpg, wpp)


def _final_norm_body(x_ref, g_ref, o_ref):
    o_ref[...] = _rms(x_ref[...], g_ref[...])


def _final_norm(h2d, gain, tm=1024):
    m = h2d.shape[0]
    return pl.pallas_call(
        _final_norm_body, name="final_norm",
        out_shape=jax.ShapeDtypeStruct((m, D_MODEL), f32),
        grid=(m // tm,),
        in_specs=[pl.BlockSpec((tm, D_MODEL), lambda i: (i, 0)), pl.BlockSpec((1, D_MODEL), lambda i: (0, 0))],
        out_specs=pl.BlockSpec((tm, D_MODEL), lambda i: (i, 0)),
        compiler_params=_cparams("parallel"),
    )(h2d, gain)


def _layer_weights(w):
    lw = {}
    lw['g_mix'] = w['norm_mix'].reshape(1, D_MODEL)
    lw['w_in'] = _reorder_w_in(w['w_in'])
    lw['s5'] = _s5_weights(w['s5_B_re'], w['s5_B_im'], w['s5_C_re'], w['s5_C_im'],
                           w['s5_lam_re'], w['s5_lam_im'], w['s5_log_dt'])
    wgate = w['gla_w_gate']
    lw['gla_wg'] = tuple(jnp.zeros((LANES, GLA_H * GLA_DK), f32).at[d * GLA_RANK:(d + 1) * GLA_RANK].set(wgate[d])
                         for d in range(2))
    lw['gla_bg'] = tuple(w['gla_b_gate'][d].reshape(1, -1) for d in range(2))
    lw['conv_w'] = jnp.pad(jnp.transpose(w['gdn_conv']), ((0, SUBLANES - CONV_W), (0, 0)))
    neg_a = -jnp.exp(w['gdn_A_log']).reshape(-1)
    par = jnp.zeros((SUBLANES, LANES), f32).at[0, :2 * GDN_H].set(neg_a).at[1, :2 * GDN_H].set(
        w['gdn_dt_bias'].reshape(-1))
    lw['gdn_par'] = par
    lw['dsk'] = w['s5_D'].reshape(1, BW)
    lw['wglu'] = w['s5_w_glu'].astype(bf16)
    lw['gng'] = w['gla_norm'].reshape(1, LANES)
    lw['dng'] = w['gdn_norm'].reshape(1, LANES)
    lw['wbr'] = w['w_branch'].astype(bf16)
    lw['wout'] = w['w_out'].astype(bf16)
    lw['g_ffn'] = w['norm_ffn'].reshape(1, D_MODEL)
    lw['wr_t'] = jnp.transpose(w['w_router'])
    lw['wg'] = w['w_exp_gate'].astype(bf16)
    lw['wu'] = w['w_exp_up'].astype(bf16)
    lw['wd'] = w['w_exp_down'].astype(bf16)
    lw['g_ple'] = w['norm_ple'].reshape(1, D_MODEL)
    lw['wpg'] = w['w_ple_gate'].astype(bf16)
    lw['wpp'] = w['w_ple_proj'].astype(bf16)
    return lw


def _mixers(h, lw):
    b, l, _ = h.shape
    h2d = h.reshape(b * l, D_MODEL)
    pa2d, pb2d = _inproj(h2d, lw['g_mix'], lw['w_in'])
    pa, pb = pa2d.reshape(b, l, D_A), pb2d.reshape(b, l, D_B)
    ch = _s5_mixer(pb, lw['s5'])
    gla_f, gla_b = _gla(pa, pb, lw['gla_wg'], lw['gla_bg'])
    qkv, gb = _gdn_prep(pa, pb, lw['conv_w'], lw['gdn_par'])
    gdn_f, gdn_b = _gdn_seq(*_gdn_par(qkv, gb))
    flat = lambda a: a.reshape(b * l, a.shape[-1])
    return _merge(h2d, pa2d, pb2d, flat(ch), flat(gla_f), flat(gla_b), flat(gdn_f), flat(gdn_b),
                  lw['dsk'], lw['wglu'], lw['gng'], lw['dng'], lw['wbr'], lw['wout'])


def _layer(h, p_i, lw):
    b, l, _ = h.shape
    h1 = _mixers(h, lw)
    h3 = _moe_ple(h1, p_i.reshape(b * l, PLE_DIM), lw['g_ffn'], lw['wr_t'], lw['wg'], lw['wu'], lw['wd'],
                  lw['g_ple'], lw['wpg'], lw['wpp'])
    return h3.reshape(b, l, D_MODEL)


def kernel(x_prompt, x_sample, p_prompt, p_sample, norm_mix, w_in, s5_B_re, s5_B_im, s5_C_re, s5_C_im, s5_D, s5_lam_re, s5_lam_im, s5_log_dt, s5_w_glu, gla_w_gate, gla_b_gate, gla_norm, gdn_conv, gdn_A_log, gdn_dt_bias, gdn_norm, w_branch, w_out, norm_ffn, w_router, w_exp_gate, w_exp_up, w_exp_down, norm_ple, w_ple_gate, w_ple_proj, norm_final):
    weights = dict(norm_mix=norm_mix, w_in=w_in, s5_B_re=s5_B_re, s5_B_im=s5_B_im, s5_C_re=s5_C_re, s5_C_im=s5_C_im,
                   s5_D=s5_D, s5_lam_re=s5_lam_re, s5_lam_im=s5_lam_im, s5_log_dt=s5_log_dt, s5_w_glu=s5_w_glu,
                   gla_w_gate=gla_w_gate, gla_b_gate=gla_b_gate, gla_norm=gla_norm, gdn_conv=gdn_conv,
                   gdn_A_log=gdn_A_log, gdn_dt_bias=gdn_dt_bias, gdn_norm=gdn_norm, w_branch=w_branch, w_out=w_out,
                   norm_ffn=norm_ffn, w_router=w_router, w_exp_gate=w_exp_gate, w_exp_up=w_exp_up,
                   w_exp_down=w_exp_down, norm_ple=norm_ple, w_ple_gate=w_ple_gate, w_ple_proj=w_ple_proj)

    def body(carry, xs):
        hp, hs = carry
        w_i, pp, ps = xs
        lw = _layer_weights(w_i)
        return (_layer(hp, pp, lw), _layer(hs, ps, lw)), None

    (hp, hs), _ = lax.scan(body, (x_prompt.astype(f32), x_sample.astype(f32)), (weights, p_prompt, p_sample))
    g_fin = norm_final.reshape(1, D_MODEL)
    yp = _final_norm(hp.reshape(-1, D_MODEL), g_fin).reshape(x_prompt.shape).astype(x_prompt.dtype)
    ys = _final_norm(hs.reshape(-1, D_MODEL), g_fin).reshape(x_sample.shape).astype(x_sample.dtype)
    return (yp, ys)
```

```python
import functools
import math

import jax
import jax.numpy as jnp
import numpy as np
from jax import lax
from jax.experimental import pallas as pl
from jax.experimental.pallas import tpu as pltpu

f32 = jnp.float32
bf16 = jnp.bfloat16
i32 = jnp.int32
HIGHEST = lax.Precision.HIGHEST

D_MODEL = 1024
DEPTH = 4
PLE_DIM = 256
BW = 512
EPS = 1e-6
CHUNK = 64
S5_GROUPS, S5_GC, S5_STATE = 32, 16, 64
GLA_H, GLA_DK, GLA_DV, GLA_RANK, GLA_TAU = 4, 64, 128, 16, 16.0
GDN_H, GDN_DK, GDN_DV, CONV_W = 4, 128, 128, 5
N_EXPERTS, EXPERT_FF, EC_FACTOR = 16, 2048, 2

LANES = 128
SUBLANES = 8
VMEM_LIMIT_BYTES = 56 * 1024 * 1024

A_GATE, A_GDN_QKV, A_GDN_Z, A_GLA_V, A_GLA_R, A_GLA_Q, A_GLA_K = 0, 3072, 4608, 5120, 5632, 6144, 6400
D_A = 6656
B_S5, B_GLA_LR, B_GDN_AB = 0, 512, 640
D_B = 768
D_INP = D_A + D_B

S5_T = 8
S5_LB = BW // LANES
S5_SW = 8 * S5_STATE * 2
S5_SCAN_CB = 2

GDN_GROUP = 4
GLA_GROUP = 4

MOE_TT = 512
MOE_W = 96
MOE_PAD = 128
MOE_FB = 512


def _cparams(*sem):
    return pltpu.CompilerParams(dimension_semantics=sem, vmem_limit_bytes=VMEM_LIMIT_BYTES)


def _nt(a, b):
    return lax.dot_general(a, b, (((1,), (1,)), ((), ())), preferred_element_type=f32)


def _tn(a, b):
    return lax.dot_general(a, b, (((0,), (0,)), ((), ())), preferred_element_type=f32)


def _mm(a, b):
    return jnp.dot(a, b, preferred_element_type=f32)


def _mm_hi(a, b):
    return jnp.dot(a, b, preferred_element_type=f32, precision=HIGHEST)


def _mmb(a, b):
    return _mm(a.astype(bf16), b.astype(bf16))


def _split3(x):
    hi = x.astype(bf16)
    r = x - hi.astype(f32)
    mid = r.astype(bf16)
    return hi, mid, (r - mid.astype(f32)).astype(bf16)


def _mm_mask(mask01, x):
    m = mask01.astype(bf16)
    hi, mid, lo = _split3(x)
    return _mm(m, hi) + (_mm(m, mid) + _mm(m, lo))


def _ind(mask, dtype=f32):
    return jnp.where(mask, 1.0, 0.0).astype(dtype)


def _sigmoid(x):
    return 0.5 * jnp.tanh(0.5 * x) + 0.5


def _silu(x):
    return x * _sigmoid(x)


def _gelu_tanh(x):
    return 0.5 * x * (1.0 + jnp.tanh(math.sqrt(2.0 / math.pi) * (x + 0.044715 * (x * x * x))))


def _rms(x, g):
    return x * lax.rsqrt(jnp.mean(x * x, axis=-1, keepdims=True) + EPS) * g


def _inproj_body(x_ref, g_ref, w_ref, oa_ref, ob_ref):
    xn = _rms(x_ref[...], g_ref[...]).astype(bf16)
    oa_ref[...] = _mm(xn, w_ref[:, :D_A]).astype(bf16)
    ob_ref[...] = _mm(xn, w_ref[:, D_A:])


def _inproj(x2d, gain, w_p, tm=256):
    m = x2d.shape[0]
    return pl.pallas_call(
        _inproj_body, name="inproj",
        out_shape=(jax.ShapeDtypeStruct((m, D_A), bf16), jax.ShapeDtypeStruct((m, D_B), f32)),
        grid=(m // tm,),
        in_specs=[pl.BlockSpec((tm, D_MODEL), lambda i: (i, 0)),
                  pl.BlockSpec((1, D_MODEL), lambda i: (0, 0)),
                  pl.BlockSpec((D_MODEL, D_INP), lambda i: (0, 0), pipeline_mode=pl.Buffered(1))],
        out_specs=(pl.BlockSpec((tm, D_A), lambda i: (i, 0)), pl.BlockSpec((tm, D_B), lambda i: (i, 0))),
        compiler_params=_cparams("parallel"),
    )(x2d, gain, w_p)


def _reorder_w_in(w_in):
    o = np.cumsum((0, 512, 256, 256, 512, 512, 32, 512, 512, 512, 512, 8, 8, 3072))
    seg = lambda k: w_in[:, o[k]:o[k + 1]]
    zpad = lambda a: jnp.pad(a, ((0, 0), (0, LANES - a.shape[1])))
    parts = [seg(12), seg(6), seg(7), seg(8), seg(9), seg(3), seg(4), seg(1), seg(2),
             seg(0), zpad(seg(5)), zpad(jnp.concatenate([seg(10), seg(11)], axis=1))]
    return jnp.concatenate(parts, axis=1).astype(bf16)


def _s5_weights(b_re, b_im, c_re, c_im, lam_re, lam_im, log_dt):
    T = S5_T
    dt = jnp.exp(log_dt)[:, :, None]
    lr, li = lam_re, lam_im
    mag = jnp.exp(lr * dt)
    ab_re, ab_im = mag * jnp.cos(li * dt), mag * jnp.sin(li * dt)
    den = lr * lr + li * li
    num_re = ab_re - 1.0
    coef_re = (num_re * lr + ab_im * li) / den
    coef_im = (ab_im * lr - num_re * li) / den
    xb_re = coef_re[..., None] * b_re[None] - coef_im[..., None] * b_im[None]
    xb_im = coef_re[..., None] * b_im[None] + coef_im[..., None] * b_re[None]

    def powers(taus):
        tau = jnp.asarray(taus, lr.dtype)
        pm = jnp.exp((lr * dt)[..., None] * tau)
        ang = (li * dt)[..., None] * tau
        return pm * jnp.cos(ang), pm * jnp.sin(ang)

    p_re, p_im = powers(np.arange(T + 1))
    cp_re = c_re[None, :, :, :, None] * p_re[:, :, None] - c_im[None, :, :, :, None] * p_im[:, :, None]
    cp_im = c_re[None, :, :, :, None] * p_im[:, :, None] + c_im[None, :, :, :, None] * p_re[:, :, None]
    kern = (jnp.einsum('dgknt,dgnc->dgtkc', cp_re, xb_re, precision=HIGHEST)
            - jnp.einsum('dgknt,dgnc->dgtkc', cp_im, xb_im, precision=HIGHEST))
    s_idx = np.arange(T)[:, None]
    t_idx = np.arange(T)[None, :]
    lag_f = np.clip(t_idx - s_idx, 0, T)
    lag_b = np.clip(s_idx - t_idx, 0, T)
    m_f = jnp.asarray((t_idx >= s_idx), kern.dtype)[None, :, :, None, None]
    m_b = jnp.asarray((s_idx >= t_idx), kern.dtype)[None, :, :, None, None]
    a_g = kern[0][:, lag_f] * m_f + kern[1][:, lag_b] * m_b
    dtype = kern.dtype
    wide = T * LANES
    col = np.arange(wide)
    row_grp_lane = lax.broadcasted_iota(i32, (wide, wide), 0) // S5_GC % 8
    col_grp_lane = lax.broadcasted_iota(i32, (wide, wide), 1) // S5_GC % 8
    st_idx = lambda ax: (2 * (lax.broadcasted_iota(i32, (wide, wide), ax) // (2 * LANES))
                         + lax.broadcasted_iota(i32, (wide, wide), ax) % LANES // S5_STATE)
    exp_tk = jnp.asarray((np.arange(LANES)[:, None] // S5_GC == col[None, :] // LANES)
                         & (np.arange(LANES)[:, None] % S5_GC == col[None, :] % S5_GC), dtype)
    exp_pn = jnp.asarray((np.arange(LANES)[:, None] // S5_STATE == col[None, :] % (2 * LANES) // LANES)
                         & (np.arange(LANES)[:, None] % S5_STATE == col[None, :] % S5_STATE), dtype)

    def expand(compact, expansion, mask):
        return jnp.where(mask, jnp.einsum('jrm,mn->jrn', compact, expansion, precision=HIGHEST), 0.0)

    a_c = jnp.transpose(a_g.reshape(S5_LB, 8, T, T, S5_GC, S5_GC), (0, 2, 1, 5, 3, 4))
    a_blk = expand(a_c.reshape(S5_LB, wide, LANES), exp_tk, row_grp_lane == col_grp_lane)

    def state_in(d, taus):
        e_re = p_re[d][:, :, taus][..., None] * xb_re[d][:, :, None, :] - p_im[d][:, :, taus][..., None] * xb_im[d][:, :, None, :]
        e_im = p_re[d][:, :, taus][..., None] * xb_im[d][:, :, None, :] + p_im[d][:, :, taus][..., None] * xb_re[d][:, :, None, :]
        e = jnp.stack([e_re, e_im], axis=0)
        e = e.reshape(2, S5_LB, 8, S5_STATE, T, S5_GC)
        e = jnp.transpose(e, (1, 4, 2, 5, 0, 3))
        return expand(e.reshape(S5_LB, wide, LANES), exp_pn, row_grp_lane == st_idx(1))

    m_f_w = state_in(0, np.arange(T - 1, -1, -1))
    m_b_w = state_in(1, np.arange(T))

    def state_out(d, taus):
        r = cp_re[d][..., taus]
        im = -cp_im[d][..., taus]
        w = jnp.stack([r, im], axis=0).reshape(2, S5_LB, 4, 2, S5_GC, S5_STATE, T)
        w = jnp.transpose(w, (1, 2, 0, 3, 5, 6, 4))
        return expand(w.reshape(S5_LB, S5_SW, LANES), exp_tk, st_idx(0) == col_grp_lane)

    n_f_w = state_out(0, np.arange(1, T + 1))
    n_b_w = state_out(1, np.arange(T, 0, -1))

    q_re, q_im = powers(T * np.arange(8))

    def table(arr_re, arr_im, d, order):
        t = jnp.stack([arr_re[d][..., order], arr_im[d][..., order]], axis=0)
        t = t.reshape(2, 16, 2, S5_STATE, len(order))
        return jnp.transpose(t, (4, 1, 0, 2, 3)).reshape(len(order), 16 * 2 * LANES)

    asc = np.arange(8)
    dbl = np.array([1, 2, 4, 0, 0, 0, 0, 0])
    tabs = (table(q_re, q_im, 0, asc), table(q_re, q_im, 0, dbl),
            table(q_re, q_im, 1, asc[::-1]), table(q_re, q_im, 1, dbl))
    return (a_blk.astype(bf16), m_f_w.astype(bf16), m_b_w.astype(bf16), n_f_w.astype(bf16), n_b_w.astype(bf16),
            tuple(t.astype(f32) for t in tabs))


def _s5_load_chunks(u_ref):
    rows = u_ref.shape[0] // S5_T
    parts = [u_ref[pl.ds(s, rows, stride=S5_T), :] for s in range(S5_T)]
    return jnp.concatenate(parts, axis=1).astype(bf16)


def _s5_in_body(u_ref, mf_ref, mb_ref, xf_ref, xb_ref):
    lhs = _s5_load_chunks(u_ref)
    xf_ref[...] = _mm(lhs, mf_ref[...])
    xb_ref[...] = _mm(lhs, mb_ref[...])


def _s5_in(proj, m_f_w, m_b_w, tl):
    b, l, _ = proj.shape
    nc = l // S5_T
    out = jax.ShapeDtypeStruct((b, nc, S5_LB * S5_SW), f32)
    wspec = pl.BlockSpec((None, S5_T * LANES, S5_SW), lambda bi, j, t: (j, 0, 0))
    ospec = pl.BlockSpec((None, tl // S5_T, S5_SW), lambda bi, j, t: (bi, t, j))
    return pl.pallas_call(
        _s5_in_body, name="s5_in",
        out_shape=(out, out),
        grid=(b, S5_LB, l // tl),
        in_specs=[pl.BlockSpec((None, tl, LANES), lambda bi, j, t: (bi, t, B_S5 // LANES + j)), wspec, wspec],
        out_specs=(ospec, ospec),
        compiler_params=_cparams("parallel", "parallel", "parallel"),
    )(proj, m_f_w, m_b_w)


def _s5_scan_body(xf_ref, xb_ref, cf_ref, hf_ref, cb_ref, hb_ref, sf_ref, sb_ref):
    n8 = xf_ref.shape[0] // SUBLANES
    ncb = xf_ref.shape[1] // (2 * LANES)
    rows = lax.broadcasted_iota(i32, (SUBLANES, LANES), 0)

    def shifted(x, d, fwd):
        if fwd:
            return jnp.where(rows >= d, pltpu.roll(x, d, 0), 0.0)
        return jnp.where(rows < SUBLANES - d, pltpu.roll(x, SUBLANES - d, 0), 0.0)

    def local_scan(xr, xi, h_ref, re, im, fwd):
        er, ei = shifted(xr, 1, fwd), shifted(xi, 1, fwd)
        for k, d in enumerate((1, 2, 4)):
            ar, ai = h_ref[k:k + 1, re], h_ref[k:k + 1, im]
            sr, si = shifted(er, d, fwd), shifted(ei, d, fwd)
            er, ei = er + ar * sr - ai * si, ei + ar * si + ai * sr
        return er, ei

    def tile(x_ref, c_ref, h_ref, o_ref, i, c, sr, si, fwd):
        re = slice(2 * c * LANES, (2 * c + 1) * LANES)
        im = slice((2 * c + 1) * LANES, (2 * c + 2) * LANES)
        r0 = pl.multiple_of(i * SUBLANES, SUBLANES)
        xr, xi = x_ref[pl.ds(r0, SUBLANES), re], x_ref[pl.ds(r0, SUBLANES), im]
        er, ei = local_scan(xr, xi, h_ref, re, im, fwd)
        cr, ci = c_ref[:, re], c_ref[:, im]
        outr = er + cr * sr - ci * si
        outi = ei + cr * si + ci * sr
        o_ref[pl.ds(r0, SUBLANES), re] = outr
        o_ref[pl.ds(r0, SUBLANES), im] = outi
        e = SUBLANES - 1 if fwd else 0
        ar, ai = h_ref[0:1, re], h_ref[0:1, im]
        nr = ar * outr[e:e + 1] - ai * outi[e:e + 1] + xr[e:e + 1]
        ni = ar * outi[e:e + 1] + ai * outr[e:e + 1] + xi[e:e + 1]
        return nr, ni

    def step(i, carry):
        out = []
        for c in range(ncb):
            fr, fi, br, bi = carry[4 * c:4 * c + 4]
            fr, fi = tile(xf_ref, cf_ref, hf_ref, sf_ref, i, c, fr, fi, True)
            br, bi = tile(xb_ref, cb_ref, hb_ref, sb_ref, n8 - 1 - i, c, br, bi, False)
            out += [fr, fi, br, bi]
        return tuple(out)

    z = jnp.zeros((1, LANES), f32)
    lax.fori_loop(0, n8, step, (z,) * (4 * ncb))


def _s5_scan(xf, xb, tabs):
    b, nc, w = xf.shape
    bw = S5_SCAN_CB * 2 * LANES
    xspec = pl.BlockSpec((None, nc, bw), lambda bi, c: (bi, 0, c))
    tspec = pl.BlockSpec((SUBLANES, bw), lambda bi, c: (0, c))
    out = jax.ShapeDtypeStruct((b, nc, w), f32)
    return pl.pallas_call(
        _s5_scan_body, name="s5_scan",
        out_shape=(out, out),
        grid=(b, w // bw),
        in_specs=[xspec, xspec, tspec, tspec, tspec, tspec],
        out_specs=(xspec, xspec),
        compiler_params=_cparams("parallel", "parallel"),
    )(xf, xb, *tabs)


def _s5_out_body(u_ref, sf_ref, sb_ref, a_ref, nf_ref, nb_ref, y_ref):
    lhs = _s5_load_chunks(u_ref)
    y = (_mm(lhs, a_ref[...]) + _mm(sf_ref[...].astype(bf16), nf_ref[...])
         + _mm(sb_ref[...].astype(bf16), nb_ref[...]))
    rows = y.shape[0]
    for t in range(S5_T):
        y_ref[pl.ds(t, rows, stride=S5_T), :] = y[:, t * LANES:(t + 1) * LANES]


def _s5_out(proj, sf, sb, a_blk, n_f_w, n_b_w, tl):
    b, l, _ = proj.shape
    sspec = pl.BlockSpec((None, tl // S5_T, S5_SW), lambda bi, j, t: (bi, t, j))
    return pl.pallas_call(
        _s5_out_body, name="s5_out",
        out_shape=jax.ShapeDtypeStruct((b, l, BW), f32),
        grid=(b, S5_LB, l // tl),
        in_specs=[pl.BlockSpec((None, tl, LANES), lambda bi, j, t: (bi, t, B_S5 // LANES + j)), sspec, sspec,
                  pl.BlockSpec((None, S5_T * LANES, S5_T * LANES), lambda bi, j, t: (j, 0, 0)),
                  pl.BlockSpec((None, S5_SW, S5_T * LANES), lambda bi, j, t: (j, 0, 0)),
                  pl.BlockSpec((None, S5_SW, S5_T * LANES), lambda bi, j, t: (j, 0, 0))],
        out_specs=pl.BlockSpec((None, tl, LANES), lambda bi, j, t: (bi, t, j)),
        compiler_params=_cparams("parallel", "parallel", "parallel"),
    )(proj, sf, sb, a_blk, n_f_w, n_b_w)


def _s5_mixer(proj, s5w, tl=2048):
    a_blk, m_f_w, m_b_w, n_f_w, n_b_w, tabs = s5w
    tl = min(tl, proj.shape[1])
    xf, xb = _s5_in(proj, m_f_w, m_b_w, tl)
    sf, sb = _s5_scan(xf, xb, tabs)
    return _s5_out(proj, sf, sb, a_blk, n_f_w, n_b_w, tl)


def _gla_body(qf_ref, kf_ref, vf_ref, lrf_ref, qb_ref, kb_ref, vb_ref, lrb_ref, wgf_ref, bgf_ref, wgb_ref, bgb_ref,
              of_ref, ob_ref, st_ref):
    ins = ((qf_ref, kf_ref, vf_ref, lrf_ref, wgf_ref, bgf_ref, of_ref),
           (qb_ref, kb_ref, vb_ref, lrb_ref, wgb_ref, bgb_ref, ob_ref))
    nch = qf_ref.shape[0] // CHUNK
    hk = GLA_H * GLA_DK
    hv = GLA_H * GLA_DV

    @pl.when(pl.program_id(1) == 0)
    def _():
        st_ref[...] = jnp.zeros_like(st_ref)

    r64 = lax.broadcasted_iota(i32, (CHUNK, CHUNK), 0)
    c64 = lax.broadcasted_iota(i32, (CHUNK, CHUNK), 1)
    tris = (_ind(r64 >= c64), _ind(r64 <= c64))
    rr = lax.broadcasted_iota(i32, (CHUNK, hk), 0)
    cc = lax.broadcasted_iota(i32, (CHUNK, hk), 1) % CHUNK
    causals = (rr >= cc, rr <= cc)
    kmask = (lax.broadcasted_iota(i32, (hk, hk), 0) // CHUNK
             == lax.broadcasted_iota(i32, (hk, hk), 1) // GLA_DK)
    vmask = (lax.broadcasted_iota(i32, (hk, hv), 0) // CHUNK
             == lax.broadcasted_iota(i32, (hk, hv), 1) // GLA_DV)
    smask = (lax.broadcasted_iota(i32, (hv, hk), 0) // GLA_DV
             == lax.broadcasted_iota(i32, (hv, hk), 1) // GLA_DK)
    scale = GLA_DK ** -0.5
    zero = jnp.zeros((), bf16)

    def group(gi, carry):
        span = GLA_GROUP * CHUNK
        bases = (gi * span, (nch - (gi + 1) * GLA_GROUP) * CHUNK)
        gls = [_mm_hi(ins[d][3][pl.ds(pl.multiple_of(bases[d], span), span), :], ins[d][4][...]) + ins[d][5][...]
               for d in range(2)]
        chains = []
        for j in range(GLA_GROUP):
            for d in range(2):
                cj = gi * GLA_GROUP + j
                c = cj if d == 0 else nch - 1 - cj
                lo = (j if d == 0 else GLA_GROUP - 1 - j) * CHUNK
                chains.append(dict(d=d, rows=pl.ds(pl.multiple_of(c * CHUNK, CHUNK), CHUNK),
                                   gl=gls[d][lo:lo + CHUNK, :]))
        for ch in chains:
            gl = ch['gl']
            g = (jnp.minimum(gl, 0.0) - jnp.log(1.0 + jnp.exp(-jnp.abs(gl)))) * (1.0 / GLA_TAU)
            ch['gc'] = _mm_mask(tris[ch['d']], g)
        for ch in chains:
            r, rows, gc = ins[ch['d']], ch['rows'], ch['gc']
            gtot = gc[CHUNK - 1:CHUNK, :] if ch['d'] == 0 else gc[0:1, :]
            k = r[1][rows, :].astype(f32)
            ch['qd'] = (r[0][rows, :].astype(f32) * scale * jnp.exp(gc)).astype(bf16)
            ki = (k * jnp.exp(-gc)).astype(bf16)
            ch['kt'] = (k * jnp.exp(gtot - gc)).astype(bf16)
            ch['dec'] = jnp.exp(gtot)
            kstack = jnp.where(kmask, jnp.concatenate([ki] * GLA_H, axis=0), zero)
            ch['sc'] = jnp.where(causals[ch['d']], _nt(ch['qd'], kstack), 0.0).astype(bf16)
        for ch in chains:
            v = ins[ch['d']][2][ch['rows'], :].astype(bf16)
            vbd = jnp.where(vmask, jnp.concatenate([v] * GLA_H, axis=0), zero)
            ch['oi'] = _mm(ch['sc'], vbd)
            ch['kv'] = _tn(v, ch['kt'])
        for ch in chains:
            d = ch['d']
            st = st_ref[d]
            ins[d][6][ch['rows'], :] = (ch['oi'] + _nt(ch['qd'], st.astype(bf16))).astype(bf16)
            st_ref[d] = st * ch['dec'] + jnp.where(smask, ch['kv'], 0.0)
        return carry

    lax.fori_loop(0, nch // GLA_GROUP, group, 0)


def _gla(pa, pb, wgs, bgs, blk=512):
    b, l, _ = pa.shape
    blk = min(blk, l)
    nb = l // blk
    assert l % blk == 0 and (blk // CHUNK) % GLA_GROUP == 0
    hk, hv = GLA_H * GLA_DK, GLA_H * GLA_DV
    up, down = (lambda i: i), (lambda i: nb - 1 - i)

    def cols(bidx):
        col = lambda width, off: pl.BlockSpec((None, blk, width), lambda bi, i: (bi, bidx(i), off // width))
        return [col(hk, A_GLA_Q), col(hk, A_GLA_K), col(hv, A_GLA_V), col(LANES, B_GLA_LR)]

    wspec = pl.BlockSpec((LANES, hk), lambda bi, i: (0, 0))
    bspec = pl.BlockSpec((1, hk), lambda bi, i: (0, 0))
    out = jax.ShapeDtypeStruct((b, l, hv), bf16)
    return pl.pallas_call(
        _gla_body, name="gla",
        out_shape=(out, out),
        grid=(b, nb),
        in_specs=cols(up) + cols(down) + [wspec, bspec, wspec, bspec],
        out_specs=(pl.BlockSpec((None, blk, hv), lambda bi, i: (bi, up(i), 0)),
                   pl.BlockSpec((None, blk, hv), lambda bi, i: (bi, down(i), 0))),
        scratch_shapes=[pltpu.VMEM((2, hv, hk), f32)],
        compiler_params=_cparams("parallel", "arbitrary"),
    )(pa, pa, pa, pb, pa, pa, pa, pb, wgs[0], bgs[0], wgs[1], bgs[1])


def _gdn_prep_body(x_ref, xp_ref, xn_ref, ab_ref, cw_ref, par_ref, qkv_ref, gb_ref):
    i, n = pl.program_id(1), pl.num_programs(1)
    blk = x_ref.shape[0]
    halo = xp_ref.shape[0]
    prev = jnp.where(i > 0, xp_ref[...].astype(f32), 0.0)
    nxt = jnp.where(i < n - 1, xn_ref[...].astype(f32), 0.0)
    ext = jnp.concatenate([prev, x_ref[...].astype(f32), nxt], axis=0)
    tot = blk + 2 * halo
    acc = None
    for t in range(CONV_W):
        sh = (CONV_W // 2 - t) % tot
        xs = ext if sh == 0 else pltpu.roll(ext, sh, 0)
        term = xs[halo:halo + blk, :] * cw_ref[t:t + 1, :]
        acc = term if acc is None else acc + term
    y = _silu(acc)
    nqk = 2 * GDN_H
    for h in range(3 * GDN_H):
        sl = slice(h * LANES, (h + 1) * LANES)
        yh = y[:, sl]
        if h < nqk:
            yh = yh * lax.rsqrt(jnp.sum(yh * yh, axis=-1, keepdims=True) + EPS)
            if h < GDN_H:
                yh = yh * GDN_DK ** -0.5
        qkv_ref[:, sl] = yh.astype(qkv_ref.dtype)
    x = ab_ref[...]
    lane = lax.broadcasted_iota(i32, x.shape, 1)
    xa = x + par_ref[1:2, :]
    softplus = jnp.maximum(xa, 0.0) + jnp.log(1.0 + jnp.exp(-jnp.abs(xa)))
    gb_ref[...] = jnp.where(lane < nqk, par_ref[0:1, :] * softplus, _sigmoid(x))


def _gdn_prep(pa, pb, conv_w, par, blk=256):
    b, l, _ = pa.shape
    blk = min(blk, l)
    nb = l // blk
    w = 3 * BW
    halo = 2 * SUBLANES
    rh = blk // halo
    last = l // halo - 1
    return pl.pallas_call(
        _gdn_prep_body, name="gdn_prep",
        out_shape=(jax.ShapeDtypeStruct((b, l, w), bf16), jax.ShapeDtypeStruct((b, l, LANES), f32)),
        grid=(b, nb),
        in_specs=[pl.BlockSpec((None, blk, w), lambda bi, i: (bi, i, A_GDN_QKV // w)),
                  pl.BlockSpec((None, halo, w), lambda bi, i: (bi, jnp.maximum(i * rh - 1, 0), A_GDN_QKV // w)),
                  pl.BlockSpec((None, halo, w), lambda bi, i: (bi, jnp.minimum((i + 1) * rh, last), A_GDN_QKV // w)),
                  pl.BlockSpec((None, blk, LANES), lambda bi, i: (bi, i, B_GDN_AB // LANES)),
                  pl.BlockSpec((SUBLANES, w), lambda bi, i: (0, 0)),
                  pl.BlockSpec((SUBLANES, LANES), lambda bi, i: (0, 0))],
        out_specs=(pl.BlockSpec((None, blk, w), lambda bi, i: (bi, i, 0)),
                   pl.BlockSpec((None, blk, LANES), lambda bi, i: (bi, i, 0))),
        compiler_params=_cparams("parallel", "parallel"),
    )(pa, pa, pa, pb, conv_w, par)


def _pair_mm(x, y, bdmask):
    yb = y.astype(bf16)
    return _mm(x.astype(bf16), jnp.where(bdmask, jnp.concatenate([yb, yb], axis=0), jnp.zeros((), bf16)))


def _unit_tri_inverses(lws, eye, bd16, bdmask):
    mm = lambda a, b: _pair_mm(a, b, bdmask)
    lds = [jnp.where(bd16, lw, 0.0) for lw in lws]
    los = [lw - ld for lw, ld in zip(lws, lds)]
    ps = [eye - ld for ld in lds]
    pw = lds
    for _ in range(3):
        pw = [mm(x, x) for x in pw]
        ps = [p + mm(p, x) for p, x in zip(ps, pw)]
    ms = [mm(p, lo) for p, lo in zip(ps, los)]
    m2s = [mm(m, m) for m in ms]
    qs = [eye - m for m in ms]
    qs = [q + mm(q, m2) for q, m2 in zip(qs, m2s)]
    return [mm(q, p) for q, p in zip(qs, ps)]


def _gdn_par_body(q_ref, k_ref, v_ref, gb_ref, *out_refs):
    nch = q_ref.shape[0] // CHUNK
    r64 = lax.broadcasted_iota(i32, (CHUNK, CHUNK), 0)
    c64 = lax.broadcasted_iota(i32, (CHUNK, CHUNK), 1)
    tris = (_ind(r64 >= c64), _ind(r64 <= c64))
    rp = lax.broadcasted_iota(i32, (CHUNK, 2 * CHUNK), 0)
    lp = lax.broadcasted_iota(i32, (CHUNK, 2 * CHUNK), 1)
    fwd_half = lp < CHUNK
    cp = jnp.where(fwd_half, lp, lp - CHUNK)
    incl = (fwd_half & (rp >= cp)) | (~fwd_half & (rp <= cp))
    strict = (fwd_half & (rp > cp)) | (~fwd_half & (rp < cp))
    eye = _ind(rp == cp)
    bd16 = (rp // 16) == (cp // 16)
    bdmask = (lax.broadcasted_iota(i32, (2 * CHUNK, 2 * CHUNK), 0) // CHUNK
              == lax.broadcasted_iota(i32, (2 * CHUNK, 2 * CHUNK), 1) // CHUNK)
    zeros_rhs = jnp.zeros((CHUNK, 2 * GDN_DV), bf16)

    def chunk_group(ci, carry):
        chains = []
        for j in range(GDN_GROUP):
            c = ci * GDN_GROUP + j
            rows = pl.ds(pl.multiple_of(c * CHUNK, CHUNK), CHUNK)
            gcols = gb_ref[rows, :]
            gcols_b = pltpu.roll(gcols, LANES - GDN_H, 1)
            gam_f = _mm_hi(tris[0], gcols)
            gam_b = _mm_hi(tris[1], gcols_b)
            grows = jnp.concatenate([gam_f.T[0:SUBLANES, :], gam_b.T[0:SUBLANES, :]], axis=1)
            out_refs[5][c] = jnp.exp(gam_f[CHUNK - 1:CHUNK, :])
            out_refs[11][c] = pltpu.roll(jnp.exp(gam_b[0:1, :]), GDN_H, 1)
            for h in range(GDN_H):
                sl = slice(h * LANES, (h + 1) * LANES)
                qb, kb16 = q_ref[rows, sl], k_ref[rows, sl]
                raw = _nt(jnp.concatenate([qb, kb16], axis=0), jnp.concatenate([kb16, kb16], axis=0))
                chains.append(dict(
                    rows=rows, h=h, raw=raw, qh=qb.astype(f32), kh=kb16.astype(f32), vh=v_ref[rows, sl].astype(f32),
                    gcols=(gam_f[:, h:h + 1], gam_b[:, h:h + 1]), grow=grows[h:h + 1, :],
                    betas=(gcols[:, 2 * GDN_H + h:2 * GDN_H + h + 1], gcols_b[:, 2 * GDN_H + h:2 * GDN_H + h + 1])))
        lws = []
        for ch in chains:
            rows, h, raw = ch['rows'], ch['h'], ch['raw']
            gcol = jnp.where(fwd_half, ch['gcols'][0], ch['gcols'][1])
            beta = jnp.where(fwd_half, ch['betas'][0], ch['betas'][1])
            dec = jnp.where(incl, jnp.exp(jnp.where(incl, gcol - ch['grow'], 0.0)), 0.0)
            attn = (raw[:CHUNK] * dec).astype(bf16)
            for d in range(2):
                out_refs[6 * d + 4][rows, h * CHUNK:(h + 1) * CHUNK] = attn[:, d * CHUNK:(d + 1) * CHUNK]
            lws.append(jnp.where(strict, raw[CHUNK:] * dec * beta, 0.0))
        tinvs = _unit_tri_inverses(lws, eye, bd16, bdmask)
        sols = []
        for ch, tinv in zip(chains, tinvs):
            ch['egs'] = tuple(jnp.exp(g) for g in ch['gcols'])
            rhs = []
            for d in range(2):
                kb = ch['kh'] * ch['betas'][d]
                rhs.append(jnp.concatenate([ch['vh'] * ch['betas'][d], kb * ch['egs'][d]], axis=1).astype(bf16))
            rhs2 = jnp.concatenate([jnp.concatenate([rhs[0], zeros_rhs], axis=1),
                                    jnp.concatenate([zeros_rhs, rhs[1]], axis=1)], axis=0)
            sols.append(_mm(tinv.astype(bf16), rhs2))
        for ch, sol in zip(chains, sols):
            rows = ch['rows']
            sl = slice(ch['h'] * LANES, (ch['h'] + 1) * LANES)
            for d in range(2):
                u_ref, w_ref, qd_ref, kt_ref = out_refs[6 * d:6 * d + 4]
                gcol = ch['gcols'][d]
                gtot = gcol[CHUNK - 1:CHUNK, :] if d == 0 else gcol[0:1, :]
                u_ref[rows, sl] = sol[:, 2 * d * GDN_DV:(2 * d + 1) * GDN_DV]
                w_ref[rows, sl] = sol[:, (2 * d + 1) * GDN_DV:(2 * d + 2) * GDN_DV].astype(bf16)
                qd_ref[rows, sl] = (ch['qh'] * ch['egs'][d]).astype(bf16)
                kt_ref[rows, sl] = (ch['kh'] * jnp.exp(gtot - gcol)).astype(bf16)
        return carry

    lax.fori_loop(0, nch // GDN_GROUP, chunk_group, 0)


def _gdn_par(qkv, gb, blk=512):
    b, l, _ = qkv.shape
    blk = min(blk, l)
    nch = blk // CHUNK
    assert l % blk == 0 and nch % GDN_GROUP == 0
    col = lambda j: pl.BlockSpec((None, blk, BW), lambda bi, i: (bi, i, j))
    wide = pl.BlockSpec((None, blk, BW), lambda bi, i: (bi, i, 0))
    one_dir_shapes = (jax.ShapeDtypeStruct((b, l, BW), f32), jax.ShapeDtypeStruct((b, l, BW), bf16),
                      jax.ShapeDtypeStruct((b, l, BW), bf16), jax.ShapeDtypeStruct((b, l, BW), bf16),
                      jax.ShapeDtypeStruct((b, l, GDN_H * CHUNK), bf16),
                      jax.ShapeDtypeStruct((b, l // CHUNK, 1, LANES), f32))
    one_dir_specs = (wide, wide, wide, wide,
                     pl.BlockSpec((None, blk, GDN_H * CHUNK), lambda bi, i: (bi, i, 0)),
                     pl.BlockSpec((None, nch, 1, LANES), lambda bi, i: (bi, i, 0, 0)))
    outs = pl.pallas_call(
        _gdn_par_body, name="gdn_par",
        out_shape=one_dir_shapes * 2,
        grid=(b, l // blk),
        in_specs=[col(0), col(1), col(2), pl.BlockSpec((None, blk, LANES), lambda bi, i: (bi, i, 0))],
        out_specs=one_dir_specs * 2,
        compiler_params=_cparams("parallel", "parallel"),
    )(qkv, qkv, qkv, gb)
    return outs[:6], outs[6:]


def _gdn_seq_body(*refs):
    ins = (refs[0:6], refs[6:12])
    o_refs = refs[12:14]
    s_ref = refs[14]
    nseq = o_refs[0].shape[0]
    nch = o_refs[0].shape[1] // CHUNK

    @pl.when(pl.program_id(0) == 0)
    def _():
        s_ref[...] = jnp.zeros_like(s_ref)

    def chunk(ci, carry):
        rows, cidx = [], []
        for d in range(2):
            c = ci if d == 0 else nch - 1 - ci
            cidx.append(c)
            rows.append(pl.ds(pl.multiple_of(c * CHUNK, CHUNK), CHUNK))
        ch = [(bi, d, h) for bi in range(nseq) for d in range(2) for h in range(GDN_H)]
        sl = lambda h: slice(h * LANES, (h + 1) * LANES)
        ss = [s_ref[bi, d, sl(h), :] for bi, d, h in ch]
        sbs = [s.astype(bf16) for s in ss]
        wss = [_mm(ins[d][1][bi, rows[d], sl(h)], sb) for (bi, d, h), sb in zip(ch, sbs)]
        qss = [_mm(ins[d][2][bi, rows[d], sl(h)], sb) for (bi, d, h), sb in zip(ch, sbs)]
        vns = [(ins[d][0][bi, rows[d], sl(h)] - ws).astype(bf16) for (bi, d, h), ws in zip(ch, wss)]
        avs = [_mm(ins[d][4][bi, rows[d], h * CHUNK:(h + 1) * CHUNK], vn) for (bi, d, h), vn in zip(ch, vns)]
        kvs = [_tn(ins[d][3][bi, rows[d], sl(h)], vn) for (bi, d, h), vn in zip(ch, vns)]
        for (bi, d, h), s, kv in zip(ch, ss, kvs):
            lg = d * GDN_H + h
            cd = ins[d][5][bi, cidx[d]]
            s_ref[bi, d, sl(h), :] = s * cd[:, lg:lg + 1] + kv
        for bi in range(nseq):
            for d in range(2):
                o_refs[d][bi, rows[d], :] = jnp.concatenate(
                    [qs + av for (b2, d2, _), qs, av in zip(ch, qss, avs) if (b2, d2) == (bi, d)],
                    axis=1).astype(bf16)
        return carry

    lax.fori_loop(0, nch, chunk, 0)


def _gdn_seq(fwd_in, bwd_in, blk=512):
    b, l, _ = fwd_in[0].shape
    blk = min(blk, l)
    nb = l // blk
    nch = blk // CHUNK

    def specs(bidx):
        wide = pl.BlockSpec((b, blk, BW), lambda i: (0, bidx(i), 0))
        return [wide, wide, wide, wide,
                pl.BlockSpec((b, blk, GDN_H * CHUNK), lambda i: (0, bidx(i), 0)),
                pl.BlockSpec((b, nch, 1, LANES), lambda i: (0, bidx(i), 0, 0))]

    up, down = (lambda i: i), (lambda i: nb - 1 - i)
    out = jax.ShapeDtypeStruct((b, l, BW), bf16)
    return pl.pallas_call(
        _gdn_seq_body, name="gdn_seq",
        out_shape=(out, out),
        grid=(nb,),
        in_specs=specs(up) + specs(down),
        out_specs=(pl.BlockSpec((b, blk, BW), lambda i: (0, up(i), 0)),
                   pl.BlockSpec((b, blk, BW), lambda i: (0, down(i), 0))),
        scratch_shapes=[pltpu.VMEM((b, 2, GDN_H * GDN_DK, GDN_DV), f32)],
        compiler_params=_cparams("arbitrary"),
    )(*fwd_in, *bwd_in)


def _head_norm_gate(o, gain, z):
    outs = []
    for h in range(BW // LANES):
        oh = o[:, h * LANES:(h + 1) * LANES]
        outs.append(oh * lax.rsqrt(jnp.mean(oh * oh, axis=-1, keepdims=True) + EPS) * gain)
    return jnp.concatenate(outs, axis=1) * _silu(z)


def _merge_body(h_ref, gate_ref, ch_ref, u_ref, glaf_ref, glab_ref, r_ref, gdnf_ref, gdnb_ref, z_ref,
                dsk_ref, wglu_ref, gng_ref, dng_ref, wbr_ref, wout_ref, o_ref):
    up = lambda ref: ref[...].astype(f32)
    y0 = _gelu_tanh(ch_ref[...] + dsk_ref[...] * u_ref[...])
    y_s5 = y0 * _sigmoid(_mm(y0.astype(bf16), wglu_ref[...]))
    y_gla = _head_norm_gate(up(glaf_ref) + up(glab_ref), gng_ref[...], up(r_ref))
    y_gdn = _head_norm_gate(up(gdnf_ref) + up(gdnb_ref), dng_ref[...], up(z_ref))
    merged = None
    for r, y in enumerate((y_s5, y_gla, y_gdn)):
        gate = _sigmoid(gate_ref[:, r * D_MODEL:(r + 1) * D_MODEL].astype(f32))
        term = gate * _mm(y.astype(bf16), wbr_ref[r])
        merged = term if merged is None else merged + term
    o_ref[...] = h_ref[...] + _mm(merged.astype(bf16), wout_ref[...])


def _merge(h2d, pa2d, pb2d, ch, gla_f, gla_b, gdn_f, gdn_b, dsk, wglu, gng, dng, wbr, wout, tm=256):
    m = h2d.shape[0]
    row = lambda width, off=0: pl.BlockSpec((tm, width), lambda i: (i, off // width))
    full = lambda shape: pl.BlockSpec(shape, lambda i: (0,) * len(shape))
    return pl.pallas_call(
        _merge_body, name="merge",
        out_shape=jax.ShapeDtypeStruct((m, D_MODEL), f32),
        grid=(m // tm,),
        in_specs=[row(D_MODEL), row(3 * D_MODEL, A_GATE), row(BW), row(BW, B_S5), row(BW), row(BW),
                  row(BW, A_GLA_R), row(BW), row(BW), row(BW, A_GDN_Z),
                  full((1, BW)), full((BW, BW)), full((1, LANES)), full((1, LANES)),
                  full((3, BW, D_MODEL)), full((D_MODEL, D_MODEL))],
        out_specs=row(D_MODEL),
        compiler_params=_cparams("parallel"),
    )(h2d, pa2d, ch, pb2d, gla_f, gla_b, pa2d, gdn_f, gdn_b, pa2d, dsk, wglu, gng, dng, wbr, wout)


def _router_body(h_ref, g_ref, wr_ref, hn_ref, aff_ref):
    hn = _rms(h_ref[...], g_ref[...])
    hn_hi = hn.astype(bf16)
    hn_ref[...] = hn_hi
    hn_lo = (hn - hn_hi.astype(f32)).astype(bf16)
    wr = wr_ref[...]
    wr_hi = wr.astype(bf16)
    wr_lo = (wr - wr_hi.astype(f32)).astype(bf16)
    logits = _nt(wr_hi, hn_hi) + (_nt(wr_hi, hn_lo) + _nt(wr_lo, hn_hi))
    e = jnp.exp(logits - jnp.max(logits, axis=0, keepdims=True))
    aff_ref[...] = e / jnp.sum(e, axis=0, keepdims=True)


def _router(h2d, gain, wr_t, tm=512):
    m = h2d.shape[0]
    return pl.pallas_call(
        _router_body, name="router",
        out_shape=(jax.ShapeDtypeStruct((m, D_MODEL), bf16), jax.ShapeDtypeStruct((N_EXPERTS, m), f32)),
        grid=(m // tm,),
        in_specs=[pl.BlockSpec((tm, D_MODEL), lambda i: (i, 0)),
                  pl.BlockSpec((1, D_MODEL), lambda i: (0, 0)),
                  pl.BlockSpec((N_EXPERTS, D_MODEL), lambda i: (0, 0))],
        out_specs=(pl.BlockSpec((tm, D_MODEL), lambda i: (i, 0)), pl.BlockSpec((N_EXPERTS, tm), lambda i: (0, i))),
        compiler_params=_cparams("parallel"),
    )(h2d, gain, wr_t)


def _threshold_body(aff_ref, thr_ref, *, cap):
    keys = pltpu.bitcast(aff_ref[...], i32)

    def count(mask):
        return jnp.sum(jnp.where(mask, 1.0, 0.0), axis=1, keepdims=True).astype(i32)

    def bit(bi, t):
        cand = t | (1 << (30 - bi))
        return jnp.where(count(keys >= cand) >= cap, cand, t)

    t = lax.fori_loop(0, 31, bit, jnp.zeros((N_EXPERTS, 1), i32))
    budget = cap - count(keys > t)
    lane = lax.broadcasted_iota(i32, (N_EXPERTS, LANES), 1)
    thr_ref[...] = jnp.where(lane == 0, t, jnp.where(lane == 1, budget, 0))


def _threshold(aff_t, cap):
    n = aff_t.shape[1]
    return pl.pallas_call(
        functools.partial(_threshold_body, cap=cap), name="topc_threshold",
        out_shape=jax.ShapeDtypeStruct((N_EXPERTS, LANES), i32),
        in_specs=[pl.BlockSpec((N_EXPERTS, n), lambda: (0, 0))],
        out_specs=pl.BlockSpec((N_EXPERTS, LANES), lambda: (0, 0)),
        compiler_params=pltpu.CompilerParams(vmem_limit_bytes=VMEM_LIMIT_BYTES),
    )(aff_t)


def _slots_body(aff_ref, thr_ref, slot_ref, wts_ref, cnt_ref, run_ref):
    @pl.when(pl.program_id(0) == 0)
    def _():
        run_ref[...] = jnp.zeros_like(run_ref)

    tt = aff_ref.shape[1]
    aff = aff_ref[...]
    keys = pltpu.bitcast(aff, i32)
    t = thr_ref[:, 0:1]
    budget = thr_ref[:, 1:2]
    upper = _ind(lax.broadcasted_iota(i32, (tt, tt), 0) <= lax.broadcasted_iota(i32, (tt, tt), 1), bf16)
    eq = keys == t
    sel_run = run_ref[:, 0:1]
    tie_run = run_ref[:, 1:2]
    cs_eq = _mm(_ind(eq, bf16), upper).astype(i32)
    tie_rank = tie_run + cs_eq - 1
    sel = (keys > t) | (eq & (tie_rank < budget))
    cs_sel = _mm(_ind(sel, bf16), upper).astype(i32)
    slot_ref[...] = jnp.where(sel, sel_run + cs_sel - 1, -1)
    wts_ref[...] = jnp.where(sel, aff, 0.0)
    n_sel = cs_sel[:, tt - 1:tt]
    n_eq = cs_eq[:, tt - 1:tt]
    n_rows = ((n_sel + (SUBLANES - 1)) // SUBLANES) * SUBLANES
    cnt_ref[...] = jnp.broadcast_to(n_rows, cnt_ref.shape)
    lane = lax.broadcasted_iota(i32, run_ref.shape, 1)
    run_ref[...] = run_ref[...] + jnp.where(lane == 0, n_rows, jnp.where(lane == 1, n_eq, 0))


def _slots(aff_t, thr):
    n = aff_t.shape[1]
    nt = n // MOE_TT
    return pl.pallas_call(
        _slots_body, name="topc_slots",
        out_shape=(jax.ShapeDtypeStruct((N_EXPERTS, n), i32), jax.ShapeDtypeStruct((N_EXPERTS, n), f32),
                   jax.ShapeDtypeStruct((nt, N_EXPERTS, LANES), i32)),
        grid=(nt,),
        in_specs=[pl.BlockSpec((N_EXPERTS, MOE_TT), lambda i: (0, i)),
                  pl.BlockSpec((N_EXPERTS, LANES), lambda i: (0, 0))],
        out_specs=(pl.BlockSpec((N_EXPERTS, MOE_TT), lambda i: (0, i)),
                   pl.BlockSpec((N_EXPERTS, MOE_TT), lambda i: (0, i)),
                   pl.BlockSpec((None, N_EXPERTS, LANES), lambda i: (i, 0, 0))),
        scratch_shapes=[pltpu.VMEM((N_EXPERTS, LANES), i32)],
        compiler_params=_cparams("arbitrary"),
    )(aff_t, thr)


def _window_hits(slot_rows, starts, width):
    tt = slot_rows.shape[1]
    r = lax.broadcasted_iota(i32, (width, tt), 0)
    return [r == (slot_rows[e:e + 1, :] - starts[e]) for e in range(len(starts))]


def _dispatch_body(off_ref, x_ref, slot_ref, xe_hbm, buf_ref, xbuf_ref, sem, *, cap):
    i = pl.program_id(0)
    w = MOE_W
    x = x_ref[...]
    slot_rows = slot_ref[...]
    starts = [off_ref[e, i] for e in range(N_EXPERTS)]

    def window_copy(e, start, src):
        return pltpu.make_async_copy(src, xe_hbm.at[e, pl.ds(pl.multiple_of(start, SUBLANES), w)], sem.at[e])

    def window_wait(e):
        pltpu.make_async_copy(buf_ref.at[e], xe_hbm.at[e, pl.ds(0, w)], sem.at[e]).wait()

    @pl.when(i > 0)
    def _():
        for e in range(N_EXPERTS):
            window_wait(e)

    @pl.when(i == 0)
    def _():
        xbuf_ref[...] = jnp.zeros_like(xbuf_ref)
        tails = [pltpu.make_async_copy(xbuf_ref, xe_hbm.at[e, pl.ds(r0, MOE_PAD)], sem.at[N_EXPERTS])
                 for e in range(N_EXPERTS) for r0 in range(cap, xe_hbm.shape[1], MOE_PAD)]
        for cp in tails:
            cp.start()
        for cp in tails:
            cp.wait()

    hits = _window_hits(slot_rows, starts, w)
    lhs = jnp.concatenate([_ind(h, bf16) for h in hits], axis=0)
    rows = _mm(lhs, x).astype(bf16)
    for e in range(N_EXPERTS):
        buf_ref[e] = rows[e * w:(e + 1) * w, :]
        window_copy(e, starts[e], buf_ref.at[e]).start()

    for e in range(N_EXPERTS):
        n_rows = off_ref[e, i + 1] - starts[e]

        def extra(k, carry, e=e):
            start = starts[e] + k * w
            hit = _window_hits(slot_rows[e:e + 1, :], [start], w)[0]
            xbuf_ref[0:w, :] = _mm(_ind(hit, bf16), x).astype(bf16)
            cp = pltpu.make_async_copy(xbuf_ref.at[pl.ds(0, w)],
                                       xe_hbm.at[e, pl.ds(pl.multiple_of(start, SUBLANES), w)], sem.at[N_EXPERTS])
            cp.start()
            cp.wait()
            return carry

        lax.fori_loop(1, (n_rows + (w - 1)) // w, extra, 0)

    @pl.when(i == pl.num_programs(0) - 1)
    def _():
        for e in range(N_EXPERTS):
            window_wait(e)


def _dispatch(hn, slot, off, rows_alloc, cap):
    n = hn.shape[0]
    grid_spec = pltpu.PrefetchScalarGridSpec(
        num_scalar_prefetch=1,
        grid=(n // MOE_TT,),
        in_specs=[pl.BlockSpec((MOE_TT, D_MODEL), lambda i, o: (i, 0)),
                  pl.BlockSpec((N_EXPERTS, MOE_TT), lambda i, o: (0, i))],
        out_specs=pl.BlockSpec(memory_space=pl.ANY),
        scratch_shapes=[pltpu.VMEM((N_EXPERTS, MOE_W, D_MODEL), bf16), pltpu.VMEM((MOE_PAD, D_MODEL), bf16),
                        pltpu.SemaphoreType.DMA((N_EXPERTS + 1,))])
    return pl.pallas_call(
        functools.partial(_dispatch_body, cap=cap), name="dispatch",
        out_shape=jax.ShapeDtypeStruct((N_EXPERTS, rows_alloc, D_MODEL), bf16),
        grid_spec=grid_spec,
        compiler_params=_cparams("arbitrary"),
    )(off, hn, slot)


def _experts_body(tot_ref, x_ref, wg_ref, wu_ref, wd_ref, y_ref):
    e, j = pl.program_id(0), pl.program_id(1)
    fb = x_ref.shape[0]
    n_valid = tot_ref[e] - j * fb

    half = fb // 2

    def ffn(x, rows):
        hid = (_silu(_mm(x, wg_ref[...])) * _mm(x, wu_ref[...])).astype(bf16)
        y_ref[0:rows, :] = _mm(hid, wd_ref[...]).astype(y_ref.dtype)

    def masked(rows):
        row = lax.broadcasted_iota(i32, (rows, 1), 0)
        return jnp.where(row < n_valid, x_ref[0:rows, :], jnp.zeros((), bf16))

    @pl.when(n_valid >= fb)
    def _():
        ffn(x_ref[...], fb)

    @pl.when((n_valid > half) & (n_valid < fb))
    def _():
        ffn(masked(fb), fb)

    @pl.when((n_valid > 0) & (n_valid <= half))
    def _():
        ffn(masked(half), half)
        y_ref[half:fb, :] = jnp.zeros((fb - half, y_ref.shape[1]), y_ref.dtype)

    @pl.when(n_valid <= 0)
    def _():
        y_ref[...] = jnp.zeros_like(y_ref)


def _experts(xe, total, wg, wu, wd, rows):
    grid_spec = pltpu.PrefetchScalarGridSpec(
        num_scalar_prefetch=1,
        grid=(N_EXPERTS, rows // MOE_FB),
        in_specs=[pl.BlockSpec((None, MOE_FB, D_MODEL), lambda e, j, t: (e, j, 0)),
                  pl.BlockSpec((None, D_MODEL, EXPERT_FF), lambda e, j, t: (e, 0, 0)),
                  pl.BlockSpec((None, D_MODEL, EXPERT_FF), lambda e, j, t: (e, 0, 0)),
                  pl.BlockSpec((None, EXPERT_FF, D_MODEL), lambda e, j, t: (e, 0, 0))],
        out_specs=pl.BlockSpec((None, MOE_FB, D_MODEL), lambda e, j, t: (e, j, 0)))
    return pl.pallas_call(
        _experts_body, name="experts",
        out_shape=jax.ShapeDtypeStruct((N_EXPERTS, rows, D_MODEL), bf16),
        grid_spec=grid_spec,
        compiler_params=_cparams("parallel", "arbitrary"),
    )(total, xe, wg, wu, wd)


def _combine_body(off_ref, h_ref, slot_ref, wts_ref, p_ref, g_ref, wpg_ref, wpp_ref, ye_hbm, o_ref,
                  win_ref, xwin_ref, acc_ref, sem, *, rows):
    i, n = pl.program_id(0), pl.num_programs(0)
    w = MOE_W
    tt = h_ref.shape[0]

    def wstart(e, tile, k=0):
        return pl.multiple_of(jnp.minimum(off_ref[e, tile] + k * w, rows - w), SUBLANES)

    def window_copy(e, tile, par):
        return pltpu.make_async_copy(ye_hbm.at[e, pl.ds(wstart(e, tile), w)], win_ref.at[par, e], sem.at[par, e])

    @pl.when(i == 0)
    def _():
        for e in range(N_EXPERTS):
            window_copy(e, 0, 0).start()

    @pl.when(i + 1 < n)
    def _():
        for e in range(N_EXPERTS):
            window_copy(e, i + 1, (i + 1) % 2).start()

    par = i % 2
    slot_rows = slot_ref[...]
    wts_rows = wts_ref[...]
    r = lax.broadcasted_iota(i32, (w, tt), 0)
    hits, wins = [], []
    for e in range(N_EXPERTS):
        window_copy(e, i, par).wait()
        s_row = slot_rows[e:e + 1, :]
        lo = off_ref[e, i]
        hit = (r == s_row - wstart(e, i)) & (s_row < lo + w)
        gsel = jnp.where(hit, wts_rows[e:e + 1, :], 0.0)
        part = gsel[:, 0:LANES]
        for c in range(1, tt // LANES):
            part = part + gsel[:, c * LANES:(c + 1) * LANES]
        slot_gate = jnp.sum(part, axis=1, keepdims=True)
        hits.append(_ind(hit, bf16))
        wins.append((win_ref[par, e].astype(f32) * slot_gate).astype(bf16))
    acc_ref[...] = h_ref[...] + _tn(jnp.concatenate(hits, axis=0), jnp.concatenate(wins, axis=0))
    gates = jnp.transpose(wts_rows)

    for e in range(N_EXPERTS):
        lo = off_ref[e, i]
        n_rows = off_ref[e, i + 1] - lo

        def extra(k, carry, e=e, lo=lo):
            cp = pltpu.make_async_copy(ye_hbm.at[e, pl.ds(wstart(e, i, k), w)], xwin_ref, sem.at[2, 0])
            cp.start()
            cp.wait()
            s_row = slot_rows[e:e + 1, :]
            hit = (r == s_row - wstart(e, i, k)) & (s_row >= lo + k * w) & (s_row < lo + (k + 1) * w)
            acc_ref[...] += gates[:, e:e + 1] * _tn(_ind(hit, bf16), xwin_ref[...])
            return carry

        lax.fori_loop(1, (n_rows + (w - 1)) // w, extra, 0)

    h2 = acc_ref[...]
    gate = _sigmoid(_mm(_rms(h2, g_ref[...]).astype(bf16), wpg_ref[...]))
    o_ref[...] = h2 + gate * _mm(p_ref[...].astype(bf16), wpp_ref[...])


def _combine(h2d, slot, wts, ye, p2d, off, g_ple, wpg, wpp):
    m = h2d.shape[0]
    rows = ye.shape[1]
    full = lambda shape: pl.BlockSpec(shape, lambda i, o: (0,) * len(shape))
    grid_spec = pltpu.PrefetchScalarGridSpec(
        num_scalar_prefetch=1,
        grid=(m // MOE_TT,),
        in_specs=[pl.BlockSpec((MOE_TT, D_MODEL), lambda i, o: (i, 0)),
                  pl.BlockSpec((N_EXPERTS, MOE_TT), lambda i, o: (0, i)),
                  pl.BlockSpec((N_EXPERTS, MOE_TT), lambda i, o: (0, i)),
                  pl.BlockSpec((MOE_TT, PLE_DIM), lambda i, o: (i, 0)),
                  full((1, D_MODEL)), full((D_MODEL, D_MODEL)), full((PLE_DIM, D_MODEL)),
                  pl.BlockSpec(memory_space=pl.ANY)],
        out_specs=pl.BlockSpec((MOE_TT, D_MODEL), lambda i, o: (i, 0)),
        scratch_shapes=[pltpu.VMEM((2, N_EXPERTS, MOE_W, D_MODEL), bf16), pltpu.VMEM((MOE_W, D_MODEL), bf16),
                        pltpu.VMEM((MOE_TT, D_MODEL), f32), pltpu.SemaphoreType.DMA((3, N_EXPERTS))])
    return pl.pallas_call(
        functools.partial(_combine_body, rows=rows), name="combine_ple",
        out_shape=jax.ShapeDtypeStruct((m, D_MODEL), f32),
        grid_spec=grid_spec,
        compiler_params=_cparams("arbitrary"),
    )(off, h2d, slot, wts, p2d, g_ple, wpg, wpp, ye)


def _moe_ple(h2d, p2d, g_ffn, wr_t, wg, wu, wd, g_ple, wpg, wpp):
    n = h2d.shape[0]
    nt = n // MOE_TT
    cap = max(1, EC_FACTOR * n // N_EXPERTS)
    rows = -(-(cap + SUBLANES * nt) // MOE_FB) * MOE_FB
    assert n % MOE_TT == 0 and cap % MOE_PAD == 0 and MOE_W <= MOE_PAD and MOE_W <= cap
    hn, aff_t = _router(h2d, g_ffn, wr_t)
    thr = _threshold(aff_t, cap)
    slot, wts, cnt = _slots(aff_t, thr)
    off = jnp.concatenate([jnp.zeros((1, N_EXPERTS), i32), jnp.cumsum(cnt[:, :, 0], axis=0, dtype=i32)], axis=0)
    off = jnp.transpose(off)
    xe = _dispatch(hn, slot, off, rows + MOE_PAD, cap)
    ye = _experts(xe, off[:, nt], wg, wu, wd, rows)
    return _combine(h2d, slot, wts, ye, p2d, off, g_ple, wpg, wpp)


def _final_norm_body(x_ref, g_ref, o_ref):
    o_ref[...] = _rms(x_ref[...], g_ref[...])


def _final_norm(h2d, gain, tm=1024):
    m = h2d.shape[0]
    return pl.pallas_call(
        _final_norm_body, name="final_norm",
        out_shape=jax.ShapeDtypeStruct((m, D_MODEL), f32),
        grid=(m // tm,),
        in_specs=[pl.BlockSpec((tm, D_MODEL), lambda i: (i, 0)), pl.BlockSpec((1, D_MODEL), lambda i: (0, 0))],
        out_specs=pl.BlockSpec((tm, D_MODEL), lambda i: (i, 0)),
        compiler_params=_cparams("parallel"),
    )(h2d, gain)


def _layer_weights(w):
    lw = {}
    lw['g_mix'] = w['norm_mix'].reshape(1, D_MODEL)
    lw['w_in'] = _reorder_w_in(w['w_in'])
    lw['s5'] = _s5_weights(w['s5_B_re'], w['s5_B_im'], w['s5_C_re'], w['s5_C_im'],
                           w['s5_lam_re'], w['s5_lam_im'], w['s5_log_dt'])
    wgate = w['gla_w_gate']
    lw['gla_wg'] = tuple(jnp.zeros((LANES, GLA_H * GLA_DK), f32).at[d * GLA_RANK:(d + 1) * GLA_RANK].set(wgate[d])
                         for d in range(2))
    lw['gla_bg'] = tuple(w['gla_b_gate'][d].reshape(1, -1) for d in range(2))
    lw['conv_w'] = jnp.pad(jnp.transpose(w['gdn_conv']), ((0, SUBLANES - CONV_W), (0, 0)))
    neg_a = -jnp.exp(w['gdn_A_log']).reshape(-1)
    par = jnp.zeros((SUBLANES, LANES), f32).at[0, :2 * GDN_H].set(neg_a).at[1, :2 * GDN_H].set(
        w['gdn_dt_bias'].reshape(-1))
    lw['gdn_par'] = par
    lw['dsk'] = w['s5_D'].reshape(1, BW)
    lw['wglu'] = w['s5_w_glu'].astype(bf16)
    lw['gng'] = w['gla_norm'].reshape(1, LANES)
    lw['dng'] = w['gdn_norm'].reshape(1, LANES)
    lw['wbr'] = w['w_branch'].astype(bf16)
    lw['wout'] = w['w_out'].astype(bf16)
    lw['g_ffn'] = w['norm_ffn'].reshape(1, D_MODEL)
    lw['wr_t'] = jnp.transpose(w['w_router'])
    lw['wg'] = w['w_exp_gate'].astype(bf16)
    lw['wu'] = w['w_exp_up'].astype(bf16)
    lw['wd'] = w['w_exp_down'].astype(bf16)
    lw['g_ple'] = w['norm_ple'].reshape(1, D_MODEL)
    lw['wpg'] = w['w_ple_gate'].astype(bf16)
    lw['wpp'] = w['w_ple_proj'].astype(bf16)
    return lw


def _mixers(h, lw):
    b, l, _ = h.shape
    h2d = h.reshape(b * l, D_MODEL)
    pa2d, pb2d = _inproj(h2d, lw['g_mix'], lw['w_in'])
    pa, pb = pa2d.reshape(b, l, D_A), pb2d.reshape(b, l, D_B)
    ch = _s5_mixer(pb, lw['s5'])
    gla_f, gla_b = _gla(pa, pb, lw['gla_wg'], lw['gla_bg'])
    qkv, gb = _gdn_prep(pa, pb, lw['conv_w'], lw['gdn_par'])
    gdn_f, gdn_b = _gdn_seq(*_gdn_par(qkv, gb))
    flat = lambda a: a.reshape(b * l, a.shape[-1])
    return _merge(h2d, pa2d, pb2d, flat(ch), flat(gla_f), flat(gla_b), flat(gdn_f), flat(gdn_b),
                  lw['dsk'], lw['wglu'], lw['gng'], lw['dng'], lw['wbr'], lw['wout'])


def _layer(h, p_i, lw):
    b, l, _ = h.shape
    h1 = _mixers(h, lw)
    h3 = _moe_ple(h1, p_i.reshape(b * l, PLE_DIM), lw['g_ffn'], lw['wr_t'], lw['wg'], lw['wu'], lw['wd'],
                  lw['g_ple'], lw['wpg'], lw['wpp'])
    return h3.reshape(b, l, D_MODEL)


def kernel(x_prompt, x_sample, p_prompt, p_sample, norm_mix, w_in, s5_B_re, s5_B_im, s5_C_re, s5_C_im, s5_D, s5_lam_re, s5_lam_im, s5_log_dt, s5_w_glu, gla_w_gate, gla_b_gate, gla_norm, gdn_conv, gdn_A_log, gdn_dt_bias, gdn_norm, w_branch, w_out, norm_ffn, w_router, w_exp_gate, w_exp_up, w_exp_down, norm_ple, w_ple_gate, w_ple_proj, norm_final):
    weights = dict(norm_mix=norm_mix, w_in=w_in, s5_B_re=s5_B_re, s5_B_im=s5_B_im, s5_C_re=s5_C_re, s5_C_im=s5_C_im,
                   s5_D=s5_D, s5_lam_re=s5_lam_re, s5_lam_im=s5_lam_im, s5_log_dt=s5_log_dt, s5_w_glu=s5_w_glu,
                   gla_w_gate=gla_w_gate, gla_b_gate=gla_b_gate, gla_norm=gla_norm, gdn_conv=gdn_conv,
                   gdn_A_log=gdn_A_log, gdn_dt_bias=gdn_dt_bias, gdn_norm=gdn_norm, w_branch=w_branch, w_out=w_out,
                   norm_ffn=norm_ffn, w_router=w_router, w_exp_gate=w_exp_gate, w_exp_up=w_exp_up,
                   w_exp_down=w_exp_down, norm_ple=norm_ple, w_ple_gate=w_ple_gate, w_ple_proj=w_ple_proj)

    def body(carry, xs):
        hp, hs = carry
        w_i, pp, ps = xs
        lw = _layer_weights(w_i)
        return (_layer(hp, pp, lw), _layer(hs, ps, lw)), None

    (hp, hs), _ = lax.scan(body, (x_prompt.astype(f32), x_sample.astype(f32)), (weights, p_prompt, p_sample))
    g_fin = norm_final.reshape(1, D_MODEL)
    yp = _final_norm(hp.reshape(-1, D_MODEL), g_fin).reshape(x_prompt.shape).astype(x_prompt.dtype)
    ys = _final_norm(hs.reshape(-1, D_MODEL), g_fin).reshape(x_sample.shape).astype(x_sample.dtype)
    return (yp, ys)
```

```python
import functools
import math

import jax
import jax.numpy as jnp
import numpy as np
from jax import lax
from jax.experimental import pallas as pl
from jax.experimental.pallas import tpu as pltpu

f32 = jnp.float32
bf16 = jnp.bfloat16
i32 = jnp.int32
HIGHEST = lax.Precision.HIGHEST

D_MODEL = 1024
DEPTH = 4
PLE_DIM = 256
BW = 512
EPS = 1e-6
CHUNK = 64
S5_GROUPS, S5_GC, S5_STATE = 32, 16, 64
GLA_H, GLA_DK, GLA_DV, GLA_RANK, GLA_TAU = 4, 64, 128, 16, 16.0
GDN_H, GDN_DK, GDN_DV, CONV_W = 4, 128, 128, 5
N_EXPERTS, EXPERT_FF, EC_FACTOR = 16, 2048, 2

LANES = 128
SUBLANES = 8
VMEM_LIMIT_BYTES = 56 * 1024 * 1024

A_GATE, A_GDN_QKV, A_GDN_Z, A_GLA_V, A_GLA_R, A_GLA_Q, A_GLA_K = 0, 3072, 4608, 5120, 5632, 6144, 6400
D_A = 6656
B_S5, B_GLA_LR, B_GDN_AB = 0, 512, 640
D_B = 768
D_INP = D_A + D_B

S5_T = 8
S5_LB = BW // LANES
S5_SW = 8 * S5_STATE * 2
S5_SCAN_CB = 2

GDN_GROUP = 4
GLA_GROUP = 4

MOE_TT = 512
MOE_W = 96
MOE_PAD = 128
MOE_FB = 512


def _cparams(*sem):
    return pltpu.CompilerParams(dimension_semantics=sem, vmem_limit_bytes=VMEM_LIMIT_BYTES)


def _nt(a, b):
    return lax.dot_general(a, b, (((1,), (1,)), ((), ())), preferred_element_type=f32)


def _tn(a, b):
    return lax.dot_general(a, b, (((0,), (0,)), ((), ())), preferred_element_type=f32)


def _mm(a, b):
    return jnp.dot(a, b, preferred_element_type=f32)


def _mm_hi(a, b):
    return jnp.dot(a, b, preferred_element_type=f32, precision=HIGHEST)


def _mmb(a, b):
    return _mm(a.astype(bf16), b.astype(bf16))


def _split3(x):
    hi = x.astype(bf16)
    r = x - hi.astype(f32)
    mid = r.astype(bf16)
    return hi, mid, (r - mid.astype(f32)).astype(bf16)


def _mm_mask(mask01, x):
    m = mask01.astype(bf16)
    hi, mid, lo = _split3(x)
    return _mm(m, hi) + (_mm(m, mid) + _mm(m, lo))


def _ind(mask, dtype=f32):
    return jnp.where(mask, 1.0, 0.0).astype(dtype)


def _sigmoid(x):
    return 0.5 * jnp.tanh(0.5 * x) + 0.5


def _silu(x):
    return x * _sigmoid(x)


def _gelu_tanh(x):
    return 0.5 * x * (1.0 + jnp.tanh(math.sqrt(2.0 / math.pi) * (x + 0.044715 * (x * x * x))))


def _rms(x, g):
    return x * lax.rsqrt(jnp.mean(x * x, axis=-1, keepdims=True) + EPS) * g


def _inproj_body(x_ref, g_ref, w_ref, oa_ref, ob_ref):
    xn = _rms(x_ref[...], g_ref[...]).astype(bf16)
    oa_ref[...] = _mm(xn, w_ref[:, :D_A]).astype(bf16)
    ob_ref[...] = _mm(xn, w_ref[:, D_A:])


def _inproj(x2d, gain, w_p, tm=256):
    m = x2d.shape[0]
    return pl.pallas_call(
        _inproj_body, name="inproj",
        out_shape=(jax.ShapeDtypeStruct((m, D_A), bf16), jax.ShapeDtypeStruct((m, D_B), f32)),
        grid=(m // tm,),
        in_specs=[pl.BlockSpec((tm, D_MODEL), lambda i: (i, 0)),
                  pl.BlockSpec((1, D_MODEL), lambda i: (0, 0)),
                  pl.BlockSpec((D_MODEL, D_INP), lambda i: (0, 0), pipeline_mode=pl.Buffered(1))],
        out_specs=(pl.BlockSpec((tm, D_A), lambda i: (i, 0)), pl.BlockSpec((tm, D_B), lambda i: (i, 0))),
        compiler_params=_cparams("parallel"),
    )(x2d, gain, w_p)


def _reorder_w_in(w_in):
    o = np.cumsum((0, 512, 256, 256, 512, 512, 32, 512, 512, 512, 512, 8, 8, 3072))
    seg = lambda k: w_in[:, o[k]:o[k + 1]]
    zpad = lambda a: jnp.pad(a, ((0, 0), (0, LANES - a.shape[1])))
    parts = [0.5 * seg(12), seg(6), seg(7), seg(8), seg(9), seg(3), seg(4), seg(1), seg(2),
             seg(0), zpad(seg(5)), zpad(jnp.concatenate([seg(10), seg(11)], axis=1))]
    return jnp.concatenate(parts, axis=1).astype(bf16)


def _s5_weights(b_re, b_im, c_re, c_im, lam_re, lam_im, log_dt):
    T = S5_T
    dt = jnp.exp(log_dt)[:, :, None]
    lr, li = lam_re, lam_im
    mag = jnp.exp(lr * dt)
    ab_re, ab_im = mag * jnp.cos(li * dt), mag * jnp.sin(li * dt)
    den = lr * lr + li * li
    num_re = ab_re - 1.0
    coef_re = (num_re * lr + ab_im * li) / den
    coef_im = (ab_im * lr - num_re * li) / den
    xb_re = coef_re[..., None] * b_re[None] - coef_im[..., None] * b_im[None]
    xb_im = coef_re[..., None] * b_im[None] + coef_im[..., None] * b_re[None]

    def powers(taus):
        tau = jnp.asarray(taus, lr.dtype)
        pm = jnp.exp((lr * dt)[..., None] * tau)
        ang = (li * dt)[..., None] * tau
        return pm * jnp.cos(ang), pm * jnp.sin(ang)

    p_re, p_im = powers(np.arange(T + 1))
    cp_re = c_re[None, :, :, :, None] * p_re[:, :, None] - c_im[None, :, :, :, None] * p_im[:, :, None]
    cp_im = c_re[None, :, :, :, None] * p_im[:, :, None] + c_im[None, :, :, :, None] * p_re[:, :, None]
    kern = (jnp.einsum('dgknt,dgnc->dgtkc', cp_re, xb_re, precision=HIGHEST)
            - jnp.einsum('dgknt,dgnc->dgtkc', cp_im, xb_im, precision=HIGHEST))
    s_idx = np.arange(T)[:, None]
    t_idx = np.arange(T)[None, :]
    lag_f = np.clip(t_idx - s_idx, 0, T)
    lag_b = np.clip(s_idx - t_idx, 0, T)
    m_f = jnp.asarray((t_idx >= s_idx), kern.dtype)[None, :, :, None, None]
    m_b = jnp.asarray((s_idx >= t_idx), kern.dtype)[None, :, :, None, None]
    a_g = kern[0][:, lag_f] * m_f + kern[1][:, lag_b] * m_b
    dtype = kern.dtype
    wide = T * LANES
    col = np.arange(wide)
    row_grp_lane = lax.broadcasted_iota(i32, (wide, wide), 0) // S5_GC % 8
    col_grp_lane = lax.broadcasted_iota(i32, (wide, wide), 1) // S5_GC % 8
    st_idx = lambda ax: (2 * (lax.broadcasted_iota(i32, (wide, wide), ax) // (2 * LANES))
                         + lax.broadcasted_iota(i32, (wide, wide), ax) % LANES // S5_STATE)
    exp_tk = jnp.asarray((np.arange(LANES)[:, None] // S5_GC == col[None, :] // LANES)
                         & (np.arange(LANES)[:, None] % S5_GC == col[None, :] % S5_GC), dtype)
    exp_pn = jnp.asarray((np.arange(LANES)[:, None] // S5_STATE == col[None, :] % (2 * LANES) // LANES)
                         & (np.arange(LANES)[:, None] % S5_STATE == col[None, :] % S5_STATE), dtype)

    def expand(compact, expansion, mask):
        return jnp.where(mask, jnp.einsum('jrm,mn->jrn', compact, expansion, precision=HIGHEST), 0.0)

    a_c = jnp.transpose(a_g.reshape(S5_LB, 8, T, T, S5_GC, S5_GC), (0, 2, 1, 5, 3, 4))
    a_blk = expand(a_c.reshape(S5_LB, wide, LANES), exp_tk, row_grp_lane == col_grp_lane)

    def state_in(d, taus):
        e_re = p_re[d][:, :, taus][..., None] * xb_re[d][:, :, None, :] - p_im[d][:, :, taus][..., None] * xb_im[d][:, :, None, :]
        e_im = p_re[d][:, :, taus][..., None] * xb_im[d][:, :, None, :] + p_im[d][:, :, taus][..., None] * xb_re[d][:, :, None, :]
        e = jnp.stack([e_re, e_im], axis=0)
        e = e.reshape(2, S5_LB, 8, S5_STATE, T, S5_GC)
        e = jnp.transpose(e, (1, 4, 2, 5, 0, 3))
        return expand(e.reshape(S5_LB, wide, LANES), exp_pn, row_grp_lane == st_idx(1))

    m_f_w = state_in(0, np.arange(T - 1, -1, -1))
    m_b_w = state_in(1, np.arange(T))

    def state_out(d, taus):
        r = cp_re[d][..., taus]
        im = -cp_im[d][..., taus]
        w = jnp.stack([r, im], axis=0).reshape(2, S5_LB, 4, 2, S5_GC, S5_STATE, T)
        w = jnp.transpose(w, (1, 2, 0, 3, 5, 6, 4))
        return expand(w.reshape(S5_LB, S5_SW, LANES), exp_tk, st_idx(0) == col_grp_lane)

    n_f_w = state_out(0, np.arange(1, T + 1))
    n_b_w = state_out(1, np.arange(T, 0, -1))

    q_re, q_im = powers(T * np.arange(8))

    def table(arr_re, arr_im, d, order):
        t = jnp.stack([arr_re[d][..., order], arr_im[d][..., order]], axis=0)
        t = t.reshape(2, 16, 2, S5_STATE, len(order))
        return jnp.transpose(t, (4, 1, 0, 2, 3)).reshape(len(order), 16 * 2 * LANES)

    asc = np.arange(8)
    dbl = np.array([1, 2, 4, 0, 0, 0, 0, 0])
    tabs = (table(q_re, q_im, 0, asc), table(q_re, q_im, 0, dbl),
            table(q_re, q_im, 1, asc[::-1]), table(q_re, q_im, 1, dbl))
    return (a_blk.astype(bf16), m_f_w.astype(bf16), m_b_w.astype(bf16), n_f_w.astype(bf16), n_b_w.astype(bf16),
            tuple(t.astype(f32) for t in tabs))


def _s5_load_chunks(u_ref):
    rows = u_ref.shape[0] // S5_T
    parts = [u_ref[pl.ds(s, rows, stride=S5_T), :] for s in range(S5_T)]
    return jnp.concatenate(parts, axis=1).astype(bf16)


def _s5_in_body(u_ref, mf_ref, mb_ref, xf_ref, xb_ref):
    lhs = _s5_load_chunks(u_ref)
    xf_ref[...] = _mm(lhs, mf_ref[...])
    xb_ref[...] = _mm(lhs, mb_ref[...])


def _s5_in(proj, m_f_w, m_b_w, tl):
    b, l, _ = proj.shape
    nc = l // S5_T
    out = jax.ShapeDtypeStruct((b, nc, S5_LB * S5_SW), f32)
    wspec = pl.BlockSpec((None, S5_T * LANES, S5_SW), lambda bi, j, t: (j, 0, 0))
    ospec = pl.BlockSpec((None, tl // S5_T, S5_SW), lambda bi, j, t: (bi, t, j))
    return pl.pallas_call(
        _s5_in_body, name="s5_in",
        out_shape=(out, out),
        grid=(b, S5_LB, l // tl),
        in_specs=[pl.BlockSpec((None, tl, LANES), lambda bi, j, t: (bi, t, B_S5 // LANES + j)), wspec, wspec],
        out_specs=(ospec, ospec),
        compiler_params=_cparams("parallel", "parallel", "parallel"),
    )(proj, m_f_w, m_b_w)


def _s5_scan_body(xf_ref, xb_ref, cf_ref, hf_ref, cb_ref, hb_ref, sf_ref, sb_ref):
    n8 = xf_ref.shape[0] // SUBLANES
    ncb = xf_ref.shape[1] // (2 * LANES)
    rows = lax.broadcasted_iota(i32, (SUBLANES, LANES), 0)

    def shifted(x, d, fwd):
        if fwd:
            return jnp.where(rows >= d, pltpu.roll(x, d, 0), 0.0)
        return jnp.where(rows < SUBLANES - d, pltpu.roll(x, SUBLANES - d, 0), 0.0)

    def local_scan(xr, xi, h_ref, re, im, fwd):
        er, ei = shifted(xr, 1, fwd), shifted(xi, 1, fwd)
        for k, d in enumerate((1, 2, 4)):
            ar, ai = h_ref[k:k + 1, re], h_ref[k:k + 1, im]
            sr, si = shifted(er, d, fwd), shifted(ei, d, fwd)
            er, ei = er + ar * sr - ai * si, ei + ar * si + ai * sr
        return er, ei

    def tile(x_ref, c_ref, h_ref, o_ref, i, c, sr, si, fwd):
        re = slice(2 * c * LANES, (2 * c + 1) * LANES)
        im = slice((2 * c + 1) * LANES, (2 * c + 2) * LANES)
        r0 = pl.multiple_of(i * SUBLANES, SUBLANES)
        xr, xi = x_ref[pl.ds(r0, SUBLANES), re], x_ref[pl.ds(r0, SUBLANES), im]
        er, ei = local_scan(xr, xi, h_ref, re, im, fwd)
        cr, ci = c_ref[:, re], c_ref[:, im]
        outr = er + cr * sr - ci * si
        outi = ei + cr * si + ci * sr
        o_ref[pl.ds(r0, SUBLANES), re] = outr
        o_ref[pl.ds(r0, SUBLANES), im] = outi
        e = SUBLANES - 1 if fwd else 0
        ar, ai = h_ref[0:1, re], h_ref[0:1, im]
        nr = ar * outr[e:e + 1] - ai * outi[e:e + 1] + xr[e:e + 1]
        ni = ar * outi[e:e + 1] + ai * outr[e:e + 1] + xi[e:e + 1]
        return nr, ni

    def step(i, carry):
        out = []
        for c in range(ncb):
            fr, fi, br, bi = carry[4 * c:4 * c + 4]
            fr, fi = tile(xf_ref, cf_ref, hf_ref, sf_ref, i, c, fr, fi, True)
            br, bi = tile(xb_ref, cb_ref, hb_ref, sb_ref, n8 - 1 - i, c, br, bi, False)
            out += [fr, fi, br, bi]
        return tuple(out)

    z = jnp.zeros((1, LANES), f32)
    lax.fori_loop(0, n8, step, (z,) * (4 * ncb))


def _s5_scan(xf, xb, tabs):
    b, nc, w = xf.shape
    bw = S5_SCAN_CB * 2 * LANES
    xspec = pl.BlockSpec((None, nc, bw), lambda bi, c: (bi, 0, c))
    tspec = pl.BlockSpec((SUBLANES, bw), lambda bi, c: (0, c))
    out = jax.ShapeDtypeStruct((b, nc, w), f32)
    return pl.pallas_call(
        _s5_scan_body, name="s5_scan",
        out_shape=(out, out),
        grid=(b, w // bw),
        in_specs=[xspec, xspec, tspec, tspec, tspec, tspec],
        out_specs=(xspec, xspec),
        compiler_params=_cparams("parallel", "parallel"),
    )(xf, xb, *tabs)


def _s5_out_body(u_ref, sf_ref, sb_ref, a_ref, nf_ref, nb_ref, y_ref):
    lhs = _s5_load_chunks(u_ref)
    y = (_mm(lhs, a_ref[...]) + _mm(sf_ref[...].astype(bf16), nf_ref[...])
         + _mm(sb_ref[...].astype(bf16), nb_ref[...]))
    rows = y.shape[0]
    for t in range(S5_T):
        y_ref[pl.ds(t, rows, stride=S5_T), :] = y[:, t * LANES:(t + 1) * LANES]


def _s5_out(proj, sf, sb, a_blk, n_f_w, n_b_w, tl):
    b, l, _ = proj.shape
    sspec = pl.BlockSpec((None, tl // S5_T, S5_SW), lambda bi, j, t: (bi, t, j))
    return pl.pallas_call(
        _s5_out_body, name="s5_out",
        out_shape=jax.ShapeDtypeStruct((b, l, BW), f32),
        grid=(b, S5_LB, l // tl),
        in_specs=[pl.BlockSpec((None, tl, LANES), lambda bi, j, t: (bi, t, B_S5 // LANES + j)), sspec, sspec,
                  pl.BlockSpec((None, S5_T * LANES, S5_T * LANES), lambda bi, j, t: (j, 0, 0)),
                  pl.BlockSpec((None, S5_SW, S5_T * LANES), lambda bi, j, t: (j, 0, 0)),
                  pl.BlockSpec((None, S5_SW, S5_T * LANES), lambda bi, j, t: (j, 0, 0))],
        out_specs=pl.BlockSpec((None, tl, LANES), lambda bi, j, t: (bi, t, j)),
        compiler_params=_cparams("parallel", "parallel", "parallel"),
    )(proj, sf, sb, a_blk, n_f_w, n_b_w)


def _s5_mixer(proj, s5w, tl=2048):
    a_blk, m_f_w, m_b_w, n_f_w, n_b_w, tabs = s5w
    tl = min(tl, proj.shape[1])
    xf, xb = _s5_in(proj, m_f_w, m_b_w, tl)
    sf, sb = _s5_scan(xf, xb, tabs)
    return _s5_out(proj, sf, sb, a_blk, n_f_w, n_b_w, tl)


def _gla_body(qf_ref, kf_ref, vf_ref, lrf_ref, qb_ref, kb_ref, vb_ref, lrb_ref, wgf_ref, bgf_ref, wgb_ref, bgb_ref,
              of_ref, ob_ref, st_ref):
    ins = ((qf_ref, kf_ref, vf_ref, lrf_ref, wgf_ref, bgf_ref, of_ref),
           (qb_ref, kb_ref, vb_ref, lrb_ref, wgb_ref, bgb_ref, ob_ref))
    nch = qf_ref.shape[0] // CHUNK
    hk = GLA_H * GLA_DK
    hv = GLA_H * GLA_DV

    @pl.when(pl.program_id(1) == 0)
    def _():
        st_ref[...] = jnp.zeros_like(st_ref)

    r64 = lax.broadcasted_iota(i32, (CHUNK, CHUNK), 0)
    c64 = lax.broadcasted_iota(i32, (CHUNK, CHUNK), 1)
    tris = (_ind(r64 >= c64), _ind(r64 <= c64))
    rr = lax.broadcasted_iota(i32, (CHUNK, hk), 0)
    cc = lax.broadcasted_iota(i32, (CHUNK, hk), 1) % CHUNK
    causals = (rr >= cc, rr <= cc)
    kmask = (lax.broadcasted_iota(i32, (hk, hk), 0) // CHUNK
             == lax.broadcasted_iota(i32, (hk, hk), 1) // GLA_DK)
    vmask = (lax.broadcasted_iota(i32, (hk, hv), 0) // CHUNK
             == lax.broadcasted_iota(i32, (hk, hv), 1) // GLA_DV)
    smask = (lax.broadcasted_iota(i32, (hv, hk), 0) // GLA_DV
             == lax.broadcasted_iota(i32, (hv, hk), 1) // GLA_DK)
    scale = GLA_DK ** -0.5
    zero = jnp.zeros((), bf16)

    def group(gi, carry):
        span = GLA_GROUP * CHUNK
        bases = (gi * span, (nch - (gi + 1) * GLA_GROUP) * CHUNK)
        gls = [_mm_hi(ins[d][3][pl.ds(pl.multiple_of(bases[d], span), span), :], ins[d][4][...]) + ins[d][5][...]
               for d in range(2)]
        chains = []
        for j in range(GLA_GROUP):
            for d in range(2):
                cj = gi * GLA_GROUP + j
                c = cj if d == 0 else nch - 1 - cj
                lo = (j if d == 0 else GLA_GROUP - 1 - j) * CHUNK
                chains.append(dict(d=d, rows=pl.ds(pl.multiple_of(c * CHUNK, CHUNK), CHUNK),
                                   gl=gls[d][lo:lo + CHUNK, :]))
        for ch in chains:
            gl = ch['gl']
            g = (jnp.minimum(gl, 0.0) - jnp.log(1.0 + jnp.exp(-jnp.abs(gl)))) * (1.0 / GLA_TAU)
            ch['gc'] = _mm_mask(tris[ch['d']], g)
        for ch in chains:
            r, rows, gc = ins[ch['d']], ch['rows'], ch['gc']
            gtot = gc[CHUNK - 1:CHUNK, :] if ch['d'] == 0 else gc[0:1, :]
            k = r[1][rows, :].astype(f32)
            ch['qd'] = (r[0][rows, :].astype(f32) * scale * jnp.exp(gc)).astype(bf16)
            ki = (k * jnp.exp(-gc)).astype(bf16)
            ch['kt'] = (k * jnp.exp(gtot - gc)).astype(bf16)
            ch['dec'] = jnp.exp(gtot)
            kstack = jnp.where(kmask, jnp.concatenate([ki] * GLA_H, axis=0), zero)
            ch['sc'] = jnp.where(causals[ch['d']], _nt(ch['qd'], kstack), 0.0).astype(bf16)
        for ch in chains:
            v = ins[ch['d']][2][ch['rows'], :].astype(bf16)
            vbd = jnp.where(vmask, jnp.concatenate([v] * GLA_H, axis=0), zero)
            ch['oi'] = _mm(ch['sc'], vbd)
            ch['kv'] = _tn(v, ch['kt'])
        for ch in chains:
            d = ch['d']
            st = st_ref[d]
            ins[d][6][ch['rows'], :] = (ch['oi'] + _nt(ch['qd'], st.astype(bf16))).astype(bf16)
            st_ref[d] = st * ch['dec'] + jnp.where(smask, ch['kv'], 0.0)
        return carry

    lax.fori_loop(0, nch // GLA_GROUP, group, 0)


def _gla(pa, pb, wgs, bgs, blk=512):
    b, l, _ = pa.shape
    blk = min(blk, l)
    nb = l // blk
    assert l % blk == 0 and (blk // CHUNK) % GLA_GROUP == 0
    hk, hv = GLA_H * GLA_DK, GLA_H * GLA_DV
    up, down = (lambda i: i), (lambda i: nb - 1 - i)

    def cols(bidx):
        col = lambda width, off: pl.BlockSpec((None, blk, width), lambda bi, i: (bi, bidx(i), off // width))
        return [col(hk, A_GLA_Q), col(hk, A_GLA_K), col(hv, A_GLA_V), col(LANES, B_GLA_LR)]

    wspec = pl.BlockSpec((LANES, hk), lambda bi, i: (0, 0))
    bspec = pl.BlockSpec((1, hk), lambda bi, i: (0, 0))
    out = jax.ShapeDtypeStruct((b, l, hv), bf16)
    return pl.pallas_call(
        _gla_body, name="gla",
        out_shape=(out, out),
        grid=(b, nb),
        in_specs=cols(up) + cols(down) + [wspec, bspec, wspec, bspec],
        out_specs=(pl.BlockSpec((None, blk, hv), lambda bi, i: (bi, up(i), 0)),
                   pl.BlockSpec((None, blk, hv), lambda bi, i: (bi, down(i), 0))),
        scratch_shapes=[pltpu.VMEM((2, hv, hk), f32)],
        compiler_params=_cparams("parallel", "arbitrary"),
    )(pa, pa, pa, pb, pa, pa, pa, pb, wgs[0], bgs[0], wgs[1], bgs[1])


def _gdn_prep_body(x_ref, xp_ref, xn_ref, ab_ref, cw_ref, par_ref, qkv_ref, gb_ref):
    i, n = pl.program_id(1), pl.num_programs(1)
    blk = x_ref.shape[0]
    halo = xp_ref.shape[0]
    prev = jnp.where(i > 0, xp_ref[...].astype(f32), 0.0)
    nxt = jnp.where(i < n - 1, xn_ref[...].astype(f32), 0.0)
    ext = jnp.concatenate([prev, x_ref[...].astype(f32), nxt], axis=0)
    tot = blk + 2 * halo
    acc = None
    for t in range(CONV_W):
        sh = (CONV_W // 2 - t) % tot
        xs = ext if sh == 0 else pltpu.roll(ext, sh, 0)
        term = xs[halo:halo + blk, :] * cw_ref[t:t + 1, :]
        acc = term if acc is None else acc + term
    y = acc + acc * jnp.tanh(acc)
    nqk = 2 * GDN_H
    for h in range(3 * GDN_H):
        sl = slice(h * LANES, (h + 1) * LANES)
        yh = y[:, sl]
        if h < nqk:
            yh = yh * lax.rsqrt(jnp.sum(yh * yh, axis=-1, keepdims=True) + EPS)
            if h < GDN_H:
                yh = yh * GDN_DK ** -0.5
        qkv_ref[:, sl] = yh.astype(qkv_ref.dtype)
    x = ab_ref[...]
    lane = lax.broadcasted_iota(i32, x.shape, 1)
    xa = x + par_ref[1:2, :]
    softplus = jnp.maximum(xa, 0.0) + jnp.log(1.0 + jnp.exp(-jnp.abs(xa)))
    gb_ref[...] = jnp.where(lane < nqk, par_ref[0:1, :] * softplus, _sigmoid(x))


def _gdn_prep(pa, pb, conv_w, par, blk=256):
    b, l, _ = pa.shape
    blk = min(blk, l)
    nb = l // blk
    w = 3 * BW
    halo = 2 * SUBLANES
    rh = blk // halo
    last = l // halo - 1
    return pl.pallas_call(
        _gdn_prep_body, name="gdn_prep",
        out_shape=(jax.ShapeDtypeStruct((b, l, w), bf16), jax.ShapeDtypeStruct((b, l, LANES), f32)),
        grid=(b, nb),
        in_specs=[pl.BlockSpec((None, blk, w), lambda bi, i: (bi, i, A_GDN_QKV // w)),
                  pl.BlockSpec((None, halo, w), lambda bi, i: (bi, jnp.maximum(i * rh - 1, 0), A_GDN_QKV // w)),
                  pl.BlockSpec((None, halo, w), lambda bi, i: (bi, jnp.minimum((i + 1) * rh, last), A_GDN_QKV // w)),
                  pl.BlockSpec((None, blk, LANES), lambda bi, i: (bi, i, B_GDN_AB // LANES)),
                  pl.BlockSpec((SUBLANES, w), lambda bi, i: (0, 0)),
                  pl.BlockSpec((SUBLANES, LANES), lambda bi, i: (0, 0))],
        out_specs=(pl.BlockSpec((None, blk, w), lambda bi, i: (bi, i, 0)),
                   pl.BlockSpec((None, blk, LANES), lambda bi, i: (bi, i, 0))),
        compiler_params=_cparams("parallel", "parallel"),
    )(pa, pa, pa, pb, conv_w, par)


def _pair_mm(x, y, bdmask):
    yb = y.astype(bf16)
    return _mm(x.astype(bf16), jnp.where(bdmask, jnp.concatenate([yb, yb], axis=0), jnp.zeros((), bf16)))


def _unit_tri_inverses(lws, eye, bd16, bdmask):
    mm = lambda a, b: _pair_mm(a, b, bdmask)
    lds = [jnp.where(bd16, lw, 0.0) for lw in lws]
    los = [lw - ld for lw, ld in zip(lws, lds)]
    ps = [eye - ld for ld in lds]
    pw = lds
    for _ in range(3):
        pw = [mm(x, x) for x in pw]
        ps = [p + mm(p, x) for p, x in zip(ps, pw)]
    ms = [mm(p, lo) for p, lo in zip(ps, los)]
    m2s = [mm(m, m) for m in ms]
    qs = [eye - m for m in ms]
    qs = [q + mm(q, m2) for q, m2 in zip(qs, m2s)]
    return [mm(q, p) for q, p in zip(qs, ps)]


def _gdn_par_body(q_ref, k_ref, v_ref, gb_ref, *out_refs):
    nch = q_ref.shape[0] // CHUNK
    r64 = lax.broadcasted_iota(i32, (CHUNK, CHUNK), 0)
    c64 = lax.broadcasted_iota(i32, (CHUNK, CHUNK), 1)
    tris = (_ind(r64 >= c64), _ind(r64 <= c64))
    rp = lax.broadcasted_iota(i32, (CHUNK, 2 * CHUNK), 0)
    lp = lax.broadcasted_iota(i32, (CHUNK, 2 * CHUNK), 1)
    fwd_half = lp < CHUNK
    cp = jnp.where(fwd_half, lp, lp - CHUNK)
    incl = (fwd_half & (rp >= cp)) | (~fwd_half & (rp <= cp))
    strict = (fwd_half & (rp > cp)) | (~fwd_half & (rp < cp))
    eye = _ind(rp == cp)
    bd16 = (rp // 16) == (cp // 16)
    bdmask = (lax.broadcasted_iota(i32, (2 * CHUNK, 2 * CHUNK), 0) // CHUNK
              == lax.broadcasted_iota(i32, (2 * CHUNK, 2 * CHUNK), 1) // CHUNK)
    zeros_rhs = jnp.zeros((CHUNK, 2 * GDN_DV), bf16)

    def chunk_group(ci, carry):
        chains = []
        for j in range(GDN_GROUP):
            c = ci * GDN_GROUP + j
            rows = pl.ds(pl.multiple_of(c * CHUNK, CHUNK), CHUNK)
            gcols = gb_ref[rows, :]
            gcols_b = pltpu.roll(gcols, LANES - GDN_H, 1)
            gam_f = _mm_hi(tris[0], gcols)
            gam_b = pltpu.roll(gam_f[CHUNK - 1:CHUNK, :] - gam_f + gcols, LANES - GDN_H, 1)
            grows = jnp.concatenate([gam_f.T[0:SUBLANES, :], gam_b.T[0:SUBLANES, :]], axis=1)
            out_refs[5][c] = jnp.exp(gam_f[CHUNK - 1:CHUNK, :])
            out_refs[11][c] = pltpu.roll(jnp.exp(gam_b[0:1, :]), GDN_H, 1)
            for h in range(GDN_H):
                sl = slice(h * LANES, (h + 1) * LANES)
                qb, kb16 = q_ref[rows, sl], k_ref[rows, sl]
                raw = _nt(jnp.concatenate([qb, kb16], axis=0), jnp.concatenate([kb16, kb16], axis=0))
                chains.append(dict(
                    rows=rows, h=h, raw=raw, qh=qb.astype(f32), kh=kb16.astype(f32), vh=v_ref[rows, sl].astype(f32),
                    gcols=(gam_f[:, h:h + 1], gam_b[:, h:h + 1]), grow=grows[h:h + 1, :],
                    betas=(gcols[:, 2 * GDN_H + h:2 * GDN_H + h + 1], gcols_b[:, 2 * GDN_H + h:2 * GDN_H + h + 1])))
        lws = []
        for ch in chains:
            rows, h, raw = ch['rows'], ch['h'], ch['raw']
            gcol = jnp.where(fwd_half, ch['gcols'][0], ch['gcols'][1])
            beta = jnp.where(fwd_half, ch['betas'][0], ch['betas'][1])
            dec = jnp.where(incl, jnp.exp(jnp.where(incl, gcol - ch['grow'], 0.0)), 0.0)
            attn = (raw[:CHUNK] * dec).astype(bf16)
            for d in range(2):
                out_refs[6 * d + 4][rows, h * CHUNK:(h + 1) * CHUNK] = attn[:, d * CHUNK:(d + 1) * CHUNK]
            lws.append(jnp.where(strict, raw[CHUNK:] * dec * beta, 0.0))
        tinvs = _unit_tri_inverses(lws, eye, bd16, bdmask)
        sols = []
        for ch, tinv in zip(chains, tinvs):
            ch['egs'] = tuple(jnp.exp(g) for g in ch['gcols'])
            rhs = []
            for d in range(2):
                kb = ch['kh'] * ch['betas'][d]
                rhs.append(jnp.concatenate([ch['vh'] * ch['betas'][d], kb * ch['egs'][d]], axis=1).astype(bf16))
            rhs2 = jnp.concatenate([jnp.concatenate([rhs[0], zeros_rhs], axis=1),
                                    jnp.concatenate([zeros_rhs, rhs[1]], axis=1)], axis=0)
            sols.append(_mm(tinv.astype(bf16), rhs2))
        for ch, sol in zip(chains, sols):
            rows = ch['rows']
            sl = slice(ch['h'] * LANES, (ch['h'] + 1) * LANES)
            for d in range(2):
                u_ref, w_ref, qd_ref, kt_ref = out_refs[6 * d:6 * d + 4]
                gcol = ch['gcols'][d]
                gtot = gcol[CHUNK - 1:CHUNK, :] if d == 0 else gcol[0:1, :]
                u_ref[rows, sl] = sol[:, 2 * d * GDN_DV:(2 * d + 1) * GDN_DV]
                w_ref[rows, sl] = sol[:, (2 * d + 1) * GDN_DV:(2 * d + 2) * GDN_DV].astype(bf16)
                qd_ref[rows, sl] = (ch['qh'] * ch['egs'][d]).astype(bf16)
                kt_ref[rows, sl] = (ch['kh'] * jnp.exp(gtot - gcol)).astype(bf16)
        return carry

    lax.fori_loop(0, nch // GDN_GROUP, chunk_group, 0)


def _gdn_par(qkv, gb, blk=512):
    b, l, _ = qkv.shape
    blk = min(blk, l)
    nch = blk // CHUNK
    assert l % blk == 0 and nch % GDN_GROUP == 0
    col = lambda j: pl.BlockSpec((None, blk, BW), lambda bi, i: (bi, i, j))
    wide = pl.BlockSpec((None, blk, BW), lambda bi, i: (bi, i, 0))
    one_dir_shapes = (jax.ShapeDtypeStruct((b, l, BW), f32), jax.ShapeDtypeStruct((b, l, BW), bf16),
                      jax.ShapeDtypeStruct((b, l, BW), bf16), jax.ShapeDtypeStruct((b, l, BW), bf16),
                      jax.ShapeDtypeStruct((b, l, GDN_H * CHUNK), bf16),
                      jax.ShapeDtypeStruct((b, l // CHUNK, 1, LANES), f32))
    one_dir_specs = (wide, wide, wide, wide,
                     pl.BlockSpec((None, blk, GDN_H * CHUNK), lambda bi, i: (bi, i, 0)),
                     pl.BlockSpec((None, nch, 1, LANES), lambda bi, i: (bi, i, 0, 0)))
    outs = pl.pallas_call(
        _gdn_par_body, name="gdn_par",
        out_shape=one_dir_shapes * 2,
        grid=(b, l // blk),
        in_specs=[col(0), col(1), col(2), pl.BlockSpec((None, blk, LANES), lambda bi, i: (bi, i, 0))],
        out_specs=one_dir_specs * 2,
        compiler_params=_cparams("parallel", "parallel"),
    )(qkv, qkv, qkv, gb)
    return outs[:6], outs[6:]


def _gdn_seq_body(*refs):
    ins = (refs[0:6], refs[6:12])
    o_refs = refs[12:14]
    s_ref = refs[14]
    nseq = o_refs[0].shape[0]
    nch = o_refs[0].shape[1] // CHUNK

    @pl.when(pl.program_id(0) == 0)
    def _():
        s_ref[...] = jnp.zeros_like(s_ref)

    def chunk(ci, carry):
        rows, cidx = [], []
        for d in range(2):
            c = ci if d == 0 else nch - 1 - ci
            cidx.append(c)
            rows.append(pl.ds(pl.multiple_of(c * CHUNK, CHUNK), CHUNK))
        ch = [(bi, d, h) for bi in range(nseq) for d in range(2) for h in range(GDN_H)]
        sl = lambda h: slice(h * LANES, (h + 1) * LANES)
        ss = [s_ref[bi, d, sl(h), :] for bi, d, h in ch]
        sbs = [s.astype(bf16) for s in ss]
        wss = [_mm(ins[d][1][bi, rows[d], sl(h)], sb) for (bi, d, h), sb in zip(ch, sbs)]
        qss = [_mm(ins[d][2][bi, rows[d], sl(h)], sb) for (bi, d, h), sb in zip(ch, sbs)]
        vns = [(ins[d][0][bi, rows[d], sl(h)] - ws).astype(bf16) for (bi, d, h), ws in zip(ch, wss)]
        avs = [_mm(ins[d][4][bi, rows[d], h * CHUNK:(h + 1) * CHUNK], vn) for (bi, d, h), vn in zip(ch, vns)]
        kvs = [_tn(ins[d][3][bi, rows[d], sl(h)], vn) for (bi, d, h), vn in zip(ch, vns)]
        for (bi, d, h), s, kv in zip(ch, ss, kvs):
            lg = d * GDN_H + h
            cd = ins[d][5][bi, cidx[d]]
            s_ref[bi, d, sl(h), :] = s * cd[:, lg:lg + 1] + kv
        for bi in range(nseq):
            for d in range(2):
                o_refs[d][bi, rows[d], :] = jnp.concatenate(
                    [qs + av for (b2, d2, _), qs, av in zip(ch, qss, avs) if (b2, d2) == (bi, d)],
                    axis=1).astype(bf16)
        return carry

    lax.fori_loop(0, nch, chunk, 0)


def _gdn_seq(fwd_in, bwd_in, blk=512):
    b, l, _ = fwd_in[0].shape
    blk = min(blk, l)
    nb = l // blk
    nch = blk // CHUNK

    def specs(bidx):
        wide = pl.BlockSpec((b, blk, BW), lambda i: (0, bidx(i), 0))
        return [wide, wide, wide, wide,
                pl.BlockSpec((b, blk, GDN_H * CHUNK), lambda i: (0, bidx(i), 0)),
                pl.BlockSpec((b, nch, 1, LANES), lambda i: (0, bidx(i), 0, 0))]

    up, down = (lambda i: i), (lambda i: nb - 1 - i)
    out = jax.ShapeDtypeStruct((b, l, BW), bf16)
    return pl.pallas_call(
        _gdn_seq_body, name="gdn_seq",
        out_shape=(out, out),
        grid=(nb,),
        in_specs=specs(up) + specs(down),
        out_specs=(pl.BlockSpec((b, blk, BW), lambda i: (0, up(i), 0)),
                   pl.BlockSpec((b, blk, BW), lambda i: (0, down(i), 0))),
        scratch_shapes=[pltpu.VMEM((b, 2, GDN_H * GDN_DK, GDN_DV), f32)],
        compiler_params=_cparams("arbitrary"),
    )(*fwd_in, *bwd_in)


def _head_norm_gate(o, gain, z):
    outs = []
    for h in range(BW // LANES):
        oh = o[:, h * LANES:(h + 1) * LANES]
        outs.append(oh * lax.rsqrt(jnp.mean(oh * oh, axis=-1, keepdims=True) + EPS) * gain)
    return jnp.concatenate(outs, axis=1) * _silu(z)


def _merge_body(h_ref, gate_ref, ch_ref, u_ref, glaf_ref, glab_ref, r_ref, gdnf_ref, gdnb_ref, z_ref,
                dsk_ref, wglu_ref, gng_ref, dng_ref, wbr_ref, wout_ref, o_ref):
    up = lambda ref: ref[...].astype(f32)
    y0 = _gelu_tanh(ch_ref[...] + dsk_ref[...] * u_ref[...])
    y_s5 = y0 * _sigmoid(_mm(y0.astype(bf16), wglu_ref[...]))
    y_gla = _head_norm_gate(up(glaf_ref) + up(glab_ref), gng_ref[...], up(r_ref))
    y_gdn = _head_norm_gate(up(gdnf_ref) + up(gdnb_ref), dng_ref[...], up(z_ref))
    merged = None
    for r, y in enumerate((y_s5, y_gla, y_gdn)):
        th = jnp.tanh(gate_ref[:, r * D_MODEL:(r + 1) * D_MODEL].astype(f32))
        t_half = _mm(y.astype(bf16), wbr_ref[r])
        term = t_half + t_half * th
        merged = term if merged is None else merged + term
    o_ref[...] = h_ref[...] + _mm(merged.astype(bf16), wout_ref[...])


def _merge(h2d, pa2d, pb2d, ch, gla_f, gla_b, gdn_f, gdn_b, dsk, wglu, gng, dng, wbr, wout, tm=256):
    m = h2d.shape[0]
    row = lambda width, off=0: pl.BlockSpec((tm, width), lambda i: (i, off // width))
    full = lambda shape: pl.BlockSpec(shape, lambda i: (0,) * len(shape))
    return pl.pallas_call(
        _merge_body, name="merge",
        out_shape=jax.ShapeDtypeStruct((m, D_MODEL), f32),
        grid=(m // tm,),
        in_specs=[row(D_MODEL), row(3 * D_MODEL, A_GATE), row(BW), row(BW, B_S5), row(BW), row(BW),
                  row(BW, A_GLA_R), row(BW), row(BW), row(BW, A_GDN_Z),
                  full((1, BW)), full((BW, BW)), full((1, LANES)), full((1, LANES)),
                  full((3, BW, D_MODEL)), full((D_MODEL, D_MODEL))],
        out_specs=row(D_MODEL),
        compiler_params=_cparams("parallel"),
    )(h2d, pa2d, ch, pb2d, gla_f, gla_b, pa2d, gdn_f, gdn_b, pa2d, dsk, wglu, gng, dng, wbr, wout)


def _router_body(h_ref, g_ref, wr_ref, hn_ref, aff_ref):
    hn = _rms(h_ref[...], g_ref[...])
    hn_hi = hn.astype(bf16)
    hn_ref[...] = hn_hi
    hn_lo = (hn - hn_hi.astype(f32)).astype(bf16)
    wr = wr_ref[...]
    wr_hi = wr.astype(bf16)
    wr_lo = (wr - wr_hi.astype(f32)).astype(bf16)
    logits = _nt(wr_hi, hn_hi) + (_nt(wr_hi, hn_lo) + _nt(wr_lo, hn_hi))
    e = jnp.exp(logits - jnp.max(logits, axis=0, keepdims=True))
    aff_ref[...] = e / jnp.sum(e, axis=0, keepdims=True)


def _router(h2d, gain, wr_t, tm=512):
    m = h2d.shape[0]
    return pl.pallas_call(
        _router_body, name="router",
        out_shape=(jax.ShapeDtypeStruct((m, D_MODEL), bf16), jax.ShapeDtypeStruct((N_EXPERTS, m), f32)),
        grid=(m // tm,),
        in_specs=[pl.BlockSpec((tm, D_MODEL), lambda i: (i, 0)),
                  pl.BlockSpec((1, D_MODEL), lambda i: (0, 0)),
                  pl.BlockSpec((N_EXPERTS, D_MODEL), lambda i: (0, 0))],
        out_specs=(pl.BlockSpec((tm, D_MODEL), lambda i: (i, 0)), pl.BlockSpec((N_EXPERTS, tm), lambda i: (0, i))),
        compiler_params=_cparams("parallel"),
    )(h2d, gain, wr_t)


def _threshold_body(aff_ref, thr_ref, *, cap):
    keys = pltpu.bitcast(aff_ref[...], i32)

    def count(mask):
        return jnp.sum(jnp.where(mask, 1.0, 0.0), axis=1, keepdims=True).astype(i32)

    def bit(bi, t):
        cand = t | (1 << (30 - bi))
        return jnp.where(count(keys >= cand) >= cap, cand, t)

    t = lax.fori_loop(0, 31, bit, jnp.zeros((N_EXPERTS, 1), i32))
    budget = cap - count(keys > t)
    lane = lax.broadcasted_iota(i32, (N_EXPERTS, LANES), 1)
    thr_ref[...] = jnp.where(lane == 0, t, jnp.where(lane == 1, budget, 0))


def _threshold(aff_t, cap):
    n = aff_t.shape[1]
    return pl.pallas_call(
        functools.partial(_threshold_body, cap=cap), name="topc_threshold",
        out_shape=jax.ShapeDtypeStruct((N_EXPERTS, LANES), i32),
        in_specs=[pl.BlockSpec((N_EXPERTS, n), lambda: (0, 0))],
        out_specs=pl.BlockSpec((N_EXPERTS, LANES), lambda: (0, 0)),
        compiler_params=pltpu.CompilerParams(vmem_limit_bytes=VMEM_LIMIT_BYTES),
    )(aff_t)


def _slots_body(aff_ref, thr_ref, slot_ref, wts_ref, cnt_ref, run_ref):
    @pl.when(pl.program_id(0) == 0)
    def _():
        run_ref[...] = jnp.zeros_like(run_ref)

    tt = aff_ref.shape[1]
    aff = aff_ref[...]
    keys = pltpu.bitcast(aff, i32)
    t = thr_ref[:, 0:1]
    budget = thr_ref[:, 1:2]
    upper = _ind(lax.broadcasted_iota(i32, (tt, tt), 0) <= lax.broadcasted_iota(i32, (tt, tt), 1), bf16)
    eq = keys == t
    sel_run = run_ref[:, 0:1]
    tie_run = run_ref[:, 1:2]
    cs_eq = _mm(_ind(eq, bf16), upper).astype(i32)
    tie_rank = tie_run + cs_eq - 1
    sel = (keys > t) | (eq & (tie_rank < budget))
    cs_sel = _mm(_ind(sel, bf16), upper).astype(i32)
    slot_ref[...] = jnp.where(sel, sel_run + cs_sel - 1, -1)
    wts_ref[...] = jnp.where(sel, aff, 0.0)
    n_sel = cs_sel[:, tt - 1:tt]
    n_eq = cs_eq[:, tt - 1:tt]
    n_rows = ((n_sel + (SUBLANES - 1)) // SUBLANES) * SUBLANES
    cnt_ref[...] = jnp.broadcast_to(n_rows, cnt_ref.shape)
    lane = lax.broadcasted_iota(i32, run_ref.shape, 1)
    run_ref[...] = run_ref[...] + jnp.where(lane == 0, n_rows, jnp.where(lane == 1, n_eq, 0))


def _slots(aff_t, thr):
    n = aff_t.shape[1]
    nt = n // MOE_TT
    return pl.pallas_call(
        _slots_body, name="topc_slots",
        out_shape=(jax.ShapeDtypeStruct((N_EXPERTS, n), i32), jax.ShapeDtypeStruct((N_EXPERTS, n), f32),
                   jax.ShapeDtypeStruct((nt, N_EXPERTS, LANES), i32)),
        grid=(nt,),
        in_specs=[pl.BlockSpec((N_EXPERTS, MOE_TT), lambda i: (0, i)),
                  pl.BlockSpec((N_EXPERTS, LANES), lambda i: (0, 0))],
        out_specs=(pl.BlockSpec((N_EXPERTS, MOE_TT), lambda i: (0, i)),
                   pl.BlockSpec((N_EXPERTS, MOE_TT), lambda i: (0, i)),
                   pl.BlockSpec((None, N_EXPERTS, LANES), lambda i: (i, 0, 0))),
        scratch_shapes=[pltpu.VMEM((N_EXPERTS, LANES), i32)],
        compiler_params=_cparams("arbitrary"),
    )(aff_t, thr)


def _window_hits(slot_rows, starts, width):
    tt = slot_rows.shape[1]
    r = lax.broadcasted_iota(i32, (width, tt), 0)
    return [r == (slot_rows[e:e + 1, :] - starts[e]) for e in range(len(starts))]


def _dispatch_body(off_ref, x_ref, slot_ref, xe_hbm, buf_ref, xbuf_ref, sem, *, cap):
    i = pl.program_id(0)
    w = MOE_W
    x = x_ref[...]
    slot_rows = slot_ref[...]
    starts = [off_ref[e, i] for e in range(N_EXPERTS)]

    def window_copy(e, start, src):
        return pltpu.make_async_copy(src, xe_hbm.at[e, pl.ds(pl.multiple_of(start, SUBLANES), w)], sem.at[e])

    def window_wait(e):
        pltpu.make_async_copy(buf_ref.at[e], xe_hbm.at[e, pl.ds(0, w)], sem.at[e]).wait()

    @pl.when(i > 0)
    def _():
        for e in range(N_EXPERTS):
            window_wait(e)

    @pl.when(i == 0)
    def _():
        xbuf_ref[...] = jnp.zeros_like(xbuf_ref)
        tails = [pltpu.make_async_copy(xbuf_ref, xe_hbm.at[e, pl.ds(r0, MOE_PAD)], sem.at[N_EXPERTS])
                 for e in range(N_EXPERTS) for r0 in range(cap, xe_hbm.shape[1], MOE_PAD)]
        for cp in tails:
            cp.start()
        for cp in tails:
            cp.wait()

    hits = _window_hits(slot_rows, starts, w)
    lhs = jnp.concatenate([_ind(h, bf16) for h in hits], axis=0)
    rows = _mm(lhs, x).astype(bf16)
    for e in range(N_EXPERTS):
        buf_ref[e] = rows[e * w:(e + 1) * w, :]
        window_copy(e, starts[e], buf_ref.at[e]).start()

    for e in range(N_EXPERTS):
        n_rows = off_ref[e, i + 1] - starts[e]

        def extra(k, carry, e=e):
            start = starts[e] + k * w
            hit = _window_hits(slot_rows[e:e + 1, :], [start], w)[0]
            xbuf_ref[0:w, :] = _mm(_ind(hit, bf16), x).astype(bf16)
            cp = pltpu.make_async_copy(xbuf_ref.at[pl.ds(0, w)],
                                       xe_hbm.at[e, pl.ds(pl.multiple_of(start, SUBLANES), w)], sem.at[N_EXPERTS])
            cp.start()
            cp.wait()
            return carry

        lax.fori_loop(1, (n_rows + (w - 1)) // w, extra, 0)

    @pl.when(i == pl.num_programs(0) - 1)
    def _():
        for e in range(N_EXPERTS):
            window_wait(e)


def _dispatch(hn, slot, off, rows_alloc, cap):
    n = hn.shape[0]
    grid_spec = pltpu.PrefetchScalarGridSpec(
        num_scalar_prefetch=1,
        grid=(n // MOE_TT,),
        in_specs=[pl.BlockSpec((MOE_TT, D_MODEL), lambda i, o: (i, 0)),
                  pl.BlockSpec((N_EXPERTS, MOE_TT), lambda i, o: (0, i))],
        out_specs=pl.BlockSpec(memory_space=pl.ANY),
        scratch_shapes=[pltpu.VMEM((N_EXPERTS, MOE_W, D_MODEL), bf16), pltpu.VMEM((MOE_PAD, D_MODEL), bf16),
                        pltpu.SemaphoreType.DMA((N_EXPERTS + 1,))])
    return pl.pallas_call(
        functools.partial(_dispatch_body, cap=cap), name="dispatch",
        out_shape=jax.ShapeDtypeStruct((N_EXPERTS, rows_alloc, D_MODEL), bf16),
        grid_spec=grid_spec,
        compiler_params=_cparams("arbitrary"),
    )(off, hn, slot)


def _experts_body(tot_ref, x_ref, wg_ref, wu_ref, wd_ref, y_ref):
    e, j = pl.program_id(0), pl.program_id(1)
    fb = x_ref.shape[0]
    n_valid = tot_ref[e] - j * fb

    half = fb // 2

    def ffn(x, rows):
        hid = (_silu(_mm(x, wg_ref[...])) * _mm(x, wu_ref[...])).astype(bf16)
        y_ref[0:rows, :] = _mm(hid, wd_ref[...]).astype(y_ref.dtype)

    def masked(rows):
        row = lax.broadcasted_iota(i32, (rows, 1), 0)
        return jnp.where(row < n_valid, x_ref[0:rows, :], jnp.zeros((), bf16))

    @pl.when(n_valid >= fb)
    def _():
        ffn(x_ref[...], fb)

    @pl.when((n_valid > half) & (n_valid < fb))
    def _():
        ffn(masked(fb), fb)

    @pl.when((n_valid > 0) & (n_valid <= half))
    def _():
        ffn(masked(half), half)
        y_ref[half:fb, :] = jnp.zeros((fb - half, y_ref.shape[1]), y_ref.dtype)

    @pl.when(n_valid <= 0)
    def _():
        y_ref[...] = jnp.zeros_like(y_ref)


def _experts(xe, total, wg, wu, wd, rows):
    grid_spec = pltpu.PrefetchScalarGridSpec(
        num_scalar_prefetch=1,
        grid=(N_EXPERTS, rows // MOE_FB),
        in_specs=[pl.BlockSpec((None, MOE_FB, D_MODEL), lambda e, j, t: (e, j, 0)),
                  pl.BlockSpec((None, D_MODEL, EXPERT_FF), lambda e, j, t: (e, 0, 0)),
                  pl.BlockSpec((None, D_MODEL, EXPERT_FF), lambda e, j, t: (e, 0, 0)),
                  pl.BlockSpec((None, EXPERT_FF, D_MODEL), lambda e, j, t: (e, 0, 0))],
        out_specs=pl.BlockSpec((None, MOE_FB, D_MODEL), lambda e, j, t: (e, j, 0)))
    return pl.pallas_call(
        _experts_body, name="experts",
        out_shape=jax.ShapeDtypeStruct((N_EXPERTS, rows, D_MODEL), bf16),
        grid_spec=grid_spec,
        compiler_params=_cparams("parallel", "arbitrary"),
    )(total, xe, wg, wu, wd)


def _combine_body(off_ref, h_ref, slot_ref, wts_ref, p_ref, g_ref, wpg_ref, wpp_ref, ye_hbm, o_ref,
                  win_ref, xwin_ref, acc_ref, sem, *, rows):
    i, n = pl.program_id(0), pl.num_programs(0)
    w = MOE_W
    tt = h_ref.shape[0]

    def wstart(e, tile, k=0):
        return pl.multiple_of(jnp.minimum(off_ref[e, tile] + k * w, rows - w), SUBLANES)

    def window_copy(e, tile, par):
        return pltpu.make_async_copy(ye_hbm.at[e, pl.ds(wstart(e, tile), w)], win_ref.at[par, e], sem.at[par, e])

    @pl.when(i == 0)
    def _():
        for e in range(N_EXPERTS):
            window_copy(e, 0, 0).start()

    @pl.when(i + 1 < n)
    def _():
        for e in range(N_EXPERTS):
            window_copy(e, i + 1, (i + 1) % 2).start()

    par = i % 2
    slot_rows = slot_ref[...]
    wts_rows = wts_ref[...]
    r = lax.broadcasted_iota(i32, (w, tt), 0)
    hits, wins = [], []
    for e in range(N_EXPERTS):
        window_copy(e, i, par).wait()
        s_row = slot_rows[e:e + 1, :]
        lo = off_ref[e, i]
        hit = (r == s_row - wstart(e, i)) & (s_row < lo + w)
        gsel = jnp.where(hit, wts_rows[e:e + 1, :], 0.0)
        part = gsel[:, 0:LANES]
        for c in range(1, tt // LANES):
            part = part + gsel[:, c * LANES:(c + 1) * LANES]
        slot_gate = jnp.sum(part, axis=1, keepdims=True)
        hits.append(_ind(hit, bf16))
        wins.append((win_ref[par, e].astype(f32) * slot_gate).astype(bf16))
    acc_ref[...] = h_ref[...] + _tn(jnp.concatenate(hits, axis=0), jnp.concatenate(wins, axis=0))
    gates = jnp.transpose(wts_rows)

    for e in range(N_EXPERTS):
        lo = off_ref[e, i]
        n_rows = off_ref[e, i + 1] - lo

        def extra(k, carry, e=e, lo=lo):
            cp = pltpu.make_async_copy(ye_hbm.at[e, pl.ds(wstart(e, i, k), w)], xwin_ref, sem.at[2, 0])
            cp.start()
            cp.wait()
            s_row = slot_rows[e:e + 1, :]
            hit = (r == s_row - wstart(e, i, k)) & (s_row >= lo + k * w) & (s_row < lo + (k + 1) * w)
            acc_ref[...] += gates[:, e:e + 1] * _tn(_ind(hit, bf16), xwin_ref[...])
            return carry

        lax.fori_loop(1, (n_rows + (w - 1)) // w, extra, 0)

    h2 = acc_ref[...]
    gate = _sigmoid(_mm(_rms(h2, g_ref[...]).astype(bf16), wpg_ref[...]))
    o_ref[...] = h2 + gate * _mm(p_ref[...].astype(bf16), wpp_ref[...])


def _combine(h2d, slot, wts, ye, p2d, off, g_ple, wpg, wpp):
    m = h2d.shape[0]
    rows = ye.shape[1]
    full = lambda shape: pl.BlockSpec(shape, lambda i, o: (0,) * len(shape))
    grid_spec = pltpu.PrefetchScalarGridSpec(
        num_scalar_prefetch=1,
        grid=(m // MOE_TT,),
        in_specs=[pl.BlockSpec((MOE_TT, D_MODEL), lambda i, o: (i, 0)),
                  pl.BlockSpec((N_EXPERTS, MOE_TT), lambda i, o: (0, i)),
                  pl.BlockSpec((N_EXPERTS, MOE_TT), lambda i, o: (0, i)),
                  pl.BlockSpec((MOE_TT, PLE_DIM), lambda i, o: (i, 0)),
                  full((1, D_MODEL)), full((D_MODEL, D_MODEL)), full((PLE_DIM, D_MODEL)),
                  pl.BlockSpec(memory_space=pl.ANY)],
        out_specs=pl.BlockSpec((MOE_TT, D_MODEL), lambda i, o: (i, 0)),
        scratch_shapes=[pltpu.VMEM((2, N_EXPERTS, MOE_W, D_MODEL), bf16), pltpu.VMEM((MOE_W, D_MODEL), bf16),
                        pltpu.VMEM((MOE_TT, D_MODEL), f32), pltpu.SemaphoreType.DMA((3, N_EXPERTS))])
    return pl.pallas_call(
        functools.partial(_combine_body, rows=rows), name="combine_ple",
        out_shape=jax.ShapeDtypeStruct((m, D_MODEL), f32),
        grid_spec=grid_spec,
        compiler_params=_cparams("arbitrary"),
    )(off, h2d, slot, wts, p2d, g_ple, wpg, wpp, ye)


def _moe_ple(h2d, p2d, g_ffn, wr_t, wg, wu, wd, g_ple, wpg, wpp):
    n = h2d.shape[0]
    nt = n // MOE_TT
    cap = max(1, EC_FACTOR * n // N_EXPERTS)
    rows = -(-(cap + SUBLANES * nt) // MOE_FB) * MOE_FB
    assert n % MOE_TT == 0 and cap % MOE_PAD == 0 and MOE_W <= MOE_PAD and MOE_W <= cap
    hn, aff_t = _router(h2d, g_ffn, wr_t)
    thr = _threshold(aff_t, cap)
    slot, wts, cnt = _slots(aff_t, thr)
    off = jnp.concatenate([jnp.zeros((1, N_EXPERTS), i32), jnp.cumsum(cnt[:, :, 0], axis=0, dtype=i32)], axis=0)
    off = jnp.transpose(off)
    xe = _dispatch(hn, slot, off, rows + MOE_PAD, cap)
    ye = _experts(xe, off[:, nt], wg, wu, wd, rows)
    return _combine(h2d, slot, wts, ye, p2d, off, g_ple, wpg, wpp)


def _final_norm_body(x_ref, g_ref, o_ref):
    o_ref[...] = _rms(x_ref[...], g_ref[...])


def _final_norm(h2d, gain, tm=1024):
    m = h2d.shape[0]
    return pl.pallas_call(
        _final_norm_body, name="final_norm",
        out_shape=jax.ShapeDtypeStruct((m, D_MODEL), f32),
        grid=(m // tm,),
        in_specs=[pl.BlockSpec((tm, D_MODEL), lambda i: (i, 0)), pl.BlockSpec((1, D_MODEL), lambda i: (0, 0))],
        out_specs=pl.BlockSpec((tm, D_MODEL), lambda i: (i, 0)),
        compiler_params=_cparams("parallel"),
    )(h2d, gain)


def _layer_weights(w):
    lw = {}
    lw['g_mix'] = w['norm_mix'].reshape(1, D_MODEL)
    lw['w_in'] = _reorder_w_in(w['w_in'])
    lw['s5'] = _s5_weights(w['s5_B_re'], w['s5_B_im'], w['s5_C_re'], w['s5_C_im'],
                           w['s5_lam_re'], w['s5_lam_im'], w['s5_log_dt'])
    wgate = w['gla_w_gate']
    lw['gla_wg'] = tuple(jnp.zeros((LANES, GLA_H * GLA_DK), f32).at[d * GLA_RANK:(d + 1) * GLA_RANK].set(wgate[d])
                         for d in range(2))
    lw['gla_bg'] = tuple(w['gla_b_gate'][d].reshape(1, -1) for d in range(2))
    lw['conv_w'] = jnp.pad(0.5 * jnp.transpose(w['gdn_conv']), ((0, SUBLANES - CONV_W), (0, 0)))
    neg_a = -jnp.exp(w['gdn_A_log']).reshape(-1)
    par = jnp.zeros((SUBLANES, LANES), f32).at[0, :2 * GDN_H].set(neg_a).at[1, :2 * GDN_H].set(
        w['gdn_dt_bias'].reshape(-1))
    lw['gdn_par'] = par
    lw['dsk'] = w['s5_D'].reshape(1, BW)
    lw['wglu'] = w['s5_w_glu'].astype(bf16)
    lw['gng'] = w['gla_norm'].reshape(1, LANES)
    lw['dng'] = w['gdn_norm'].reshape(1, LANES)
    lw['wbr'] = (0.5 * w['w_branch']).astype(bf16)
    lw['wout'] = w['w_out'].astype(bf16)
    lw['g_ffn'] = w['norm_ffn'].reshape(1, D_MODEL)
    lw['wr_t'] = jnp.transpose(w['w_router'])
    lw['wg'] = w['w_exp_gate'].astype(bf16)
    lw['wu'] = w['w_exp_up'].astype(bf16)
    lw['wd'] = w['w_exp_down'].astype(bf16)
    lw['g_ple'] = w['norm_ple'].reshape(1, D_MODEL)
    lw['wpg'] = w['w_ple_gate'].astype(bf16)
    lw['wpp'] = w['w_ple_proj'].astype(bf16)
    return lw


def _mixers(h, lw):
    b, l, _ = h.shape
    h2d = h.reshape(b * l, D_MODEL)
    pa2d, pb2d = _inproj(h2d, lw['g_mix'], lw['w_in'])
    pa, pb = pa2d.reshape(b, l, D_A), pb2d.reshape(b, l, D_B)
    ch = _s5_mixer(pb, lw['s5'])
    gla_f, gla_b = _gla(pa, pb, lw['gla_wg'], lw['gla_bg'])
    qkv, gb = _gdn_prep(pa, pb, lw['conv_w'], lw['gdn_par'])
    gdn_f, gdn_b = _gdn_seq(*_gdn_par(qkv, gb))
    flat = lambda a: a.reshape(b * l, a.shape[-1])
    return _merge(h2d, pa2d, pb2d, flat(ch), flat(gla_f), flat(gla_b), flat(gdn_f), flat(gdn_b),
                  lw['dsk'], lw['wglu'], lw['gng'], lw['dng'], lw['wbr'], lw['wout'])


def _layer(h, p_i, lw):
    b, l, _ = h.shape
    h1 = _mixers(h, lw)
    h3 = _moe_ple(h1, p_i.reshape(b * l, PLE_DIM), lw['g_ffn'], lw['wr_t'], lw['wg'], lw['wu'], lw['wd'],
                  lw['g_ple'], lw['wpg'], lw['wpp'])
    return h3.reshape(b, l, D_MODEL)


def kernel(x_prompt, x_sample, p_prompt, p_sample, norm_mix, w_in, s5_B_re, s5_B_im, s5_C_re, s5_C_im, s5_D, s5_lam_re, s5_lam_im, s5_log_dt, s5_w_glu, gla_w_gate, gla_b_gate, gla_norm, gdn_conv, gdn_A_log, gdn_dt_bias, gdn_norm, w_branch, w_out, norm_ffn, w_router, w_exp_gate, w_exp_up, w_exp_down, norm_ple, w_ple_gate, w_ple_proj, norm_final):
    weights = dict(norm_mix=norm_mix, w_in=w_in, s5_B_re=s5_B_re, s5_B_im=s5_B_im, s5_C_re=s5_C_re, s5_C_im=s5_C_im,
                   s5_D=s5_D, s5_lam_re=s5_lam_re, s5_lam_im=s5_lam_im, s5_log_dt=s5_log_dt, s5_w_glu=s5_w_glu,
                   gla_w_gate=gla_w_gate, gla_b_gate=gla_b_gate, gla_norm=gla_norm, gdn_conv=gdn_conv,
                   gdn_A_log=gdn_A_log, gdn_dt_bias=gdn_dt_bias, gdn_norm=gdn_norm, w_branch=w_branch, w_out=w_out,
                   norm_ffn=norm_ffn, w_router=w_router, w_exp_gate=w_exp_gate, w_exp_up=w_exp_up,
                   w_exp_down=w_exp_down, norm_ple=norm_ple, w_ple_gate=w_ple_gate, w_ple_proj=w_ple_proj)

    def body(carry, xs):
        hp, hs = carry
        w_i, pp, ps = xs
        lw = _layer_weights(w_i)
        return (_layer(hp, pp, lw), _layer(hs, ps, lw)), None

    (hp, hs), _ = lax.scan(body, (x_prompt.astype(f32), x_sample.astype(f32)), (weights, p_prompt, p_sample))
    g_fin = norm_final.reshape(1, D_MODEL)
    yp = _final_norm(hp.reshape(-1, D_MODEL), g_fin).reshape(x_prompt.shape).astype(x_prompt.dtype)
    ys = _final_norm(hs.reshape(-1, D_MODEL), g_fin).reshape(x_sample.shape).astype(x_sample.dtype)
    return (yp, ys)
```

```python
import functools
import math

import jax
import jax.numpy as jnp
import numpy as np
from jax import lax
from jax.experimental import pallas as pl
from jax.experimental.pallas import tpu as pltpu

f32 = jnp.float32
bf16 = jnp.bfloat16
i32 = jnp.int32
HIGHEST = lax.Precision.HIGHEST

D_MODEL = 1024
DEPTH = 4
PLE_DIM = 256
BW = 512
EPS = 1e-6
CHUNK = 64
S5_GROUPS, S5_GC, S5_STATE = 32, 16, 64
GLA_H, GLA_DK, GLA_DV, GLA_RANK, GLA_TAU = 4, 64, 128, 16, 16.0
GDN_H, GDN_DK, GDN_DV, CONV_W = 4, 128, 128, 5
N_EXPERTS, EXPERT_FF, EC_FACTOR = 16, 2048, 2

LANES = 128
SUBLANES = 8
VMEM_LIMIT_BYTES = 56 * 1024 * 1024

A_GATE, A_GDN_QKV, A_GDN_Z, A_GLA_V, A_GLA_R, A_GLA_Q, A_GLA_K = 0, 3072, 4608, 5120, 5632, 6144, 6400
D_A = 6656
B_S5, B_GLA_LR, B_GDN_AB = 0, 512, 640
D_B = 768
D_INP = D_A + D_B

S5_T = 8
S5_LB = BW // LANES
S5_SW = 8 * S5_STATE * 2
S5_SCAN_CB = 2

GDN_GROUP = 4
GLA_GROUP = 4

MOE_TT = 512
MOE_W = 96
MOE_PAD = 128
MOE_FB = 512


def _cparams(*sem):
    return pltpu.CompilerParams(dimension_semantics=sem, vmem_limit_bytes=VMEM_LIMIT_BYTES)


def _nt(a, b):
    return lax.dot_general(a, b, (((1,), (1,)), ((), ())), preferred_element_type=f32)


def _tn(a, b):
    return lax.dot_general(a, b, (((0,), (0,)), ((), ())), preferred_element_type=f32)


def _mm(a, b):
    return jnp.dot(a, b, preferred_element_type=f32)


def _mm_hi(a, b):
    return jnp.dot(a, b, preferred_element_type=f32, precision=HIGHEST)


def _mmb(a, b):
    return _mm(a.astype(bf16), b.astype(bf16))


def _split3(x):
    hi = x.astype(bf16)
    r = x - hi.astype(f32)
    mid = r.astype(bf16)
    return hi, mid, (r - mid.astype(f32)).astype(bf16)


def _mm_mask(mask01, x):
    m = mask01.astype(bf16)
    hi, mid, lo = _split3(x)
    return _mm(m, hi) + (_mm(m, mid) + _mm(m, lo))


def _ind(mask, dtype=f32):
    return jnp.where(mask, 1.0, 0.0).astype(dtype)


def _sigmoid(x):
    return 0.5 * jnp.tanh(0.5 * x) + 0.5


def _silu(x):
    return x * _sigmoid(x)


def _gelu_tanh(x):
    return 0.5 * x * (1.0 + jnp.tanh(math.sqrt(2.0 / math.pi) * (x + 0.044715 * (x * x * x))))


def _rms(x, g):
    return x * lax.rsqrt(jnp.mean(x * x, axis=-1, keepdims=True) + EPS) * g


def _inproj_body(x_ref, g_ref, w_ref, oa_ref, ob_ref):
    xn = _rms(x_ref[...], g_ref[...]).astype(bf16)
    oa_ref[...] = _mm(xn, w_ref[:, :D_A]).astype(bf16)
    ob_ref[...] = _mm(xn, w_ref[:, D_A:])


def _inproj(x2d, gain, w_p, tm=512):
    m = x2d.shape[0]
    return pl.pallas_call(
        _inproj_body, name="inproj",
        out_shape=(jax.ShapeDtypeStruct((m, D_A), bf16), jax.ShapeDtypeStruct((m, D_B), f32)),
        grid=(m // tm,),
        in_specs=[pl.BlockSpec((tm, D_MODEL), lambda i: (i, 0)),
                  pl.BlockSpec((1, D_MODEL), lambda i: (0, 0)),
                  pl.BlockSpec((D_MODEL, D_INP), lambda i: (0, 0), pipeline_mode=pl.Buffered(1))],
        out_specs=(pl.BlockSpec((tm, D_A), lambda i: (i, 0)), pl.BlockSpec((tm, D_B), lambda i: (i, 0))),
        compiler_params=_cparams("parallel"),
    )(x2d, gain, w_p)


def _reorder_w_in(w_in):
    o = np.cumsum((0, 512, 256, 256, 512, 512, 32, 512, 512, 512, 512, 8, 8, 3072))
    seg = lambda k: w_in[:, o[k]:o[k + 1]]
    zpad = lambda a: jnp.pad(a, ((0, 0), (0, LANES - a.shape[1])))
    parts = [0.5 * seg(12), seg(6), seg(7), seg(8), seg(9), seg(3), seg(4), seg(1), seg(2),
             seg(0), zpad(seg(5)), zpad(jnp.concatenate([seg(10), seg(11)], axis=1))]
    return jnp.concatenate(parts, axis=1).astype(bf16)


def _s5_weights(b_re, b_im, c_re, c_im, lam_re, lam_im, log_dt):
    T = S5_T
    dt = jnp.exp(log_dt)[:, :, None]
    lr, li = lam_re, lam_im
    mag = jnp.exp(lr * dt)
    ab_re, ab_im = mag * jnp.cos(li * dt), mag * jnp.sin(li * dt)
    den = lr * lr + li * li
    num_re = ab_re - 1.0
    coef_re = (num_re * lr + ab_im * li) / den
    coef_im = (ab_im * lr - num_re * li) / den
    xb_re = coef_re[..., None] * b_re[None] - coef_im[..., None] * b_im[None]
    xb_im = coef_re[..., None] * b_im[None] + coef_im[..., None] * b_re[None]

    def powers(taus):
        tau = jnp.asarray(taus, lr.dtype)
        pm = jnp.exp((lr * dt)[..., None] * tau)
        ang = (li * dt)[..., None] * tau
        return pm * jnp.cos(ang), pm * jnp.sin(ang)

    p_re, p_im = powers(np.arange(T + 1))
    cp_re = c_re[None, :, :, :, None] * p_re[:, :, None] - c_im[None, :, :, :, None] * p_im[:, :, None]
    cp_im = c_re[None, :, :, :, None] * p_im[:, :, None] + c_im[None, :, :, :, None] * p_re[:, :, None]
    kern = (jnp.einsum('dgknt,dgnc->dgtkc', cp_re, xb_re, precision=HIGHEST)
            - jnp.einsum('dgknt,dgnc->dgtkc', cp_im, xb_im, precision=HIGHEST))
    s_idx = np.arange(T)[:, None]
    t_idx = np.arange(T)[None, :]
    lag_f = np.clip(t_idx - s_idx, 0, T)
    lag_b = np.clip(s_idx - t_idx, 0, T)
    m_f = jnp.asarray((t_idx >= s_idx), kern.dtype)[None, :, :, None, None]
    m_b = jnp.asarray((s_idx >= t_idx), kern.dtype)[None, :, :, None, None]
    a_g = kern[0][:, lag_f] * m_f + kern[1][:, lag_b] * m_b
    dtype = kern.dtype
    wide = T * LANES
    col = np.arange(wide)
    row_grp_lane = lax.broadcasted_iota(i32, (wide, wide), 0) // S5_GC % 8
    col_grp_lane = lax.broadcasted_iota(i32, (wide, wide), 1) // S5_GC % 8
    st_idx = lambda ax: (2 * (lax.broadcasted_iota(i32, (wide, wide), ax) // (2 * LANES))
                         + lax.broadcasted_iota(i32, (wide, wide), ax) % LANES // S5_STATE)
    exp_tk = jnp.asarray((np.arange(LANES)[:, None] // S5_GC == col[None, :] // LANES)
                         & (np.arange(LANES)[:, None] % S5_GC == col[None, :] % S5_GC), dtype)
    exp_pn = jnp.asarray((np.arange(LANES)[:, None] // S5_STATE == col[None, :] % (2 * LANES) // LANES)
                         & (np.arange(LANES)[:, None] % S5_STATE == col[None, :] % S5_STATE), dtype)

    def expand(compact, expansion, mask):
        return jnp.where(mask, jnp.einsum('jrm,mn->jrn', compact, expansion, precision=HIGHEST), 0.0)

    a_c = jnp.transpose(a_g.reshape(S5_LB, 8, T, T, S5_GC, S5_GC), (0, 2, 1, 5, 3, 4))
    a_blk = expand(a_c.reshape(S5_LB, wide, LANES), exp_tk, row_grp_lane == col_grp_lane)

    def state_in(d, taus):
        e_re = p_re[d][:, :, taus][..., None] * xb_re[d][:, :, None, :] - p_im[d][:, :, taus][..., None] * xb_im[d][:, :, None, :]
        e_im = p_re[d][:, :, taus][..., None] * xb_im[d][:, :, None, :] + p_im[d][:, :, taus][..., None] * xb_re[d][:, :, None, :]
        e = jnp.stack([e_re, e_im], axis=0)
        e = e.reshape(2, S5_LB, 8, S5_STATE, T, S5_GC)
        e = jnp.transpose(e, (1, 4, 2, 5, 0, 3))
        return expand(e.reshape(S5_LB, wide, LANES), exp_pn, row_grp_lane == st_idx(1))

    m_f_w = state_in(0, np.arange(T - 1, -1, -1))
    m_b_w = state_in(1, np.arange(T))

    def state_out(d, taus):
        r = cp_re[d][..., taus]
        im = -cp_im[d][..., taus]
        w = jnp.stack([r, im], axis=0).reshape(2, S5_LB, 4, 2, S5_GC, S5_STATE, T)
        w = jnp.transpose(w, (1, 2, 0, 3, 5, 6, 4))
        return expand(w.reshape(S5_LB, S5_SW, LANES), exp_tk, st_idx(0) == col_grp_lane)

    n_f_w = state_out(0, np.arange(1, T + 1))
    n_b_w = state_out(1, np.arange(T, 0, -1))

    q_re, q_im = powers(T * np.arange(8))

    def table(arr_re, arr_im, d, order):
        t = jnp.stack([arr_re[d][..., order], arr_im[d][..., order]], axis=0)
        t = t.reshape(2, 16, 2, S5_STATE, len(order))
        return jnp.transpose(t, (4, 1, 0, 2, 3)).reshape(len(order), 16 * 2 * LANES)

    asc = np.arange(8)
    dbl = np.array([1, 2, 4, 0, 0, 0, 0, 0])
    tabs = (table(q_re, q_im, 0, asc), table(q_re, q_im, 0, dbl),
            table(q_re, q_im, 1, asc[::-1]), table(q_re, q_im, 1, dbl))
    return (a_blk.astype(bf16), m_f_w.astype(bf16), m_b_w.astype(bf16), n_f_w.astype(bf16), n_b_w.astype(bf16),
            tuple(t.astype(f32) for t in tabs))


def _s5_load_chunks(u_ref):
    rows = u_ref.shape[0] // S5_T
    parts = [u_ref[pl.ds(s, rows, stride=S5_T), :] for s in range(S5_T)]
    return jnp.concatenate(parts, axis=1).astype(bf16)


def _s5_in_body(u_ref, mf_ref, mb_ref, xf_ref, xb_ref):
    lhs = _s5_load_chunks(u_ref)
    xf_ref[...] = _mm(lhs, mf_ref[...])
    xb_ref[...] = _mm(lhs, mb_ref[...])


def _s5_in(proj, m_f_w, m_b_w, tl):
    b, l, _ = proj.shape
    nc = l // S5_T
    out = jax.ShapeDtypeStruct((b, nc, S5_LB * S5_SW), f32)
    wspec = pl.BlockSpec((None, S5_T * LANES, S5_SW), lambda bi, j, t: (j, 0, 0))
    ospec = pl.BlockSpec((None, tl // S5_T, S5_SW), lambda bi, j, t: (bi, t, j))
    return pl.pallas_call(
        _s5_in_body, name="s5_in",
        out_shape=(out, out),
        grid=(b, S5_LB, l // tl),
        in_specs=[pl.BlockSpec((None, tl, LANES), lambda bi, j, t: (bi, t, B_S5 // LANES + j)), wspec, wspec],
        out_specs=(ospec, ospec),
        compiler_params=_cparams("parallel", "parallel", "parallel"),
    )(proj, m_f_w, m_b_w)


def _s5_scan_body(xf_ref, xb_ref, cf_ref, hf_ref, cb_ref, hb_ref, sf_ref, sb_ref):
    n8 = xf_ref.shape[0] // SUBLANES
    ncb = xf_ref.shape[1] // (2 * LANES)
    rows = lax.broadcasted_iota(i32, (SUBLANES, LANES), 0)

    def shifted(x, d, fwd):
        if fwd:
            return jnp.where(rows >= d, pltpu.roll(x, d, 0), 0.0)
        return jnp.where(rows < SUBLANES - d, pltpu.roll(x, SUBLANES - d, 0), 0.0)

    def local_scan(xr, xi, h_ref, re, im, fwd):
        er, ei = shifted(xr, 1, fwd), shifted(xi, 1, fwd)
        for k, d in enumerate((1, 2, 4)):
            ar, ai = h_ref[k:k + 1, re], h_ref[k:k + 1, im]
            sr, si = shifted(er, d, fwd), shifted(ei, d, fwd)
            er, ei = er + ar * sr - ai * si, ei + ar * si + ai * sr
        return er, ei

    def tile(x_ref, c_ref, h_ref, o_ref, i, c, sr, si, fwd):
        re = slice(2 * c * LANES, (2 * c + 1) * LANES)
        im = slice((2 * c + 1) * LANES, (2 * c + 2) * LANES)
        r0 = pl.multiple_of(i * SUBLANES, SUBLANES)
        xr, xi = x_ref[pl.ds(r0, SUBLANES), re], x_ref[pl.ds(r0, SUBLANES), im]
        er, ei = local_scan(xr, xi, h_ref, re, im, fwd)
        cr, ci = c_ref[:, re], c_ref[:, im]
        outr = er + cr * sr - ci * si
        outi = ei + cr * si + ci * sr
        o_ref[pl.ds(r0, SUBLANES), re] = outr
        o_ref[pl.ds(r0, SUBLANES), im] = outi
        e = SUBLANES - 1 if fwd else 0
        ar, ai = h_ref[0:1, re], h_ref[0:1, im]
        nr = ar * outr[e:e + 1] - ai * outi[e:e + 1] + xr[e:e + 1]
        ni = ar * outi[e:e + 1] + ai * outr[e:e + 1] + xi[e:e + 1]
        return nr, ni

    def step(i, carry):
        out = []
        for c in range(ncb):
            fr, fi, br, bi = carry[4 * c:4 * c + 4]
            fr, fi = tile(xf_ref, cf_ref, hf_ref, sf_ref, i, c, fr, fi, True)
            br, bi = tile(xb_ref, cb_ref, hb_ref, sb_ref, n8 - 1 - i, c, br, bi, False)
            out += [fr, fi, br, bi]
        return tuple(out)

    z = jnp.zeros((1, LANES), f32)
    lax.fori_loop(0, n8, step, (z,) * (4 * ncb))


def _s5_scan(xf, xb, tabs):
    b, nc, w = xf.shape
    bw = S5_SCAN_CB * 2 * LANES
    xspec = pl.BlockSpec((None, nc, bw), lambda bi, c: (bi, 0, c))
    tspec = pl.BlockSpec((SUBLANES, bw), lambda bi, c: (0, c))
    out = jax.ShapeDtypeStruct((b, nc, w), f32)
    return pl.pallas_call(
        _s5_scan_body, name="s5_scan",
        out_shape=(out, out),
        grid=(b, w // bw),
        in_specs=[xspec, xspec, tspec, tspec, tspec, tspec],
        out_specs=(xspec, xspec),
        compiler_params=_cparams("parallel", "parallel"),
    )(xf, xb, *tabs)


def _s5_out_body(u_ref, sf_ref, sb_ref, a_ref, nf_ref, nb_ref, y_ref):
    lhs = _s5_load_chunks(u_ref)
    y = (_mm(lhs, a_ref[...]) + _mm(sf_ref[...].astype(bf16), nf_ref[...])
         + _mm(sb_ref[...].astype(bf16), nb_ref[...]))
    rows = y.shape[0]
    for t in range(S5_T):
        y_ref[pl.ds(t, rows, stride=S5_T), :] = y[:, t * LANES:(t + 1) * LANES]


def _s5_out(proj, sf, sb, a_blk, n_f_w, n_b_w, tl):
    b, l, _ = proj.shape
    sspec = pl.BlockSpec((None, tl // S5_T, S5_SW), lambda bi, j, t: (bi, t, j))
    return pl.pallas_call(
        _s5_out_body, name="s5_out",
        out_shape=jax.ShapeDtypeStruct((b, l, BW), f32),
        grid=(b, S5_LB, l // tl),
        in_specs=[pl.BlockSpec((None, tl, LANES), lambda bi, j, t: (bi, t, B_S5 // LANES + j)), sspec, sspec,
                  pl.BlockSpec((None, S5_T * LANES, S5_T * LANES), lambda bi, j, t: (j, 0, 0)),
                  pl.BlockSpec((None, S5_SW, S5_T * LANES), lambda bi, j, t: (j, 0, 0)),
                  pl.BlockSpec((None, S5_SW, S5_T * LANES), lambda bi, j, t: (j, 0, 0))],
        out_specs=pl.BlockSpec((None, tl, LANES), lambda bi, j, t: (bi, t, j)),
        compiler_params=_cparams("parallel", "parallel", "parallel"),
    )(proj, sf, sb, a_blk, n_f_w, n_b_w)


def _s5_mixer(proj, s5w, tl=2048):
    a_blk, m_f_w, m_b_w, n_f_w, n_b_w, tabs = s5w
    tl = min(tl, proj.shape[1])
    xf, xb = _s5_in(proj, m_f_w, m_b_w, tl)
    sf, sb = _s5_scan(xf, xb, tabs)
    return _s5_out(proj, sf, sb, a_blk, n_f_w, n_b_w, tl)


def _gla_body(qf_ref, kf_ref, vf_ref, lrf_ref, qb_ref, kb_ref, vb_ref, lrb_ref, wgf_ref, bgf_ref, wgb_ref, bgb_ref,
              of_ref, ob_ref, st_ref):
    ins = ((qf_ref, kf_ref, vf_ref, lrf_ref, wgf_ref, bgf_ref, of_ref),
           (qb_ref, kb_ref, vb_ref, lrb_ref, wgb_ref, bgb_ref, ob_ref))
    nch = qf_ref.shape[0] // CHUNK
    hk = GLA_H * GLA_DK
    hv = GLA_H * GLA_DV

    @pl.when(pl.program_id(1) == 0)
    def _():
        st_ref[...] = jnp.zeros_like(st_ref)

    r64 = lax.broadcasted_iota(i32, (CHUNK, CHUNK), 0)
    c64 = lax.broadcasted_iota(i32, (CHUNK, CHUNK), 1)
    tris = (_ind(r64 >= c64), _ind(r64 <= c64))
    rr = lax.broadcasted_iota(i32, (CHUNK, hk), 0)
    cc = lax.broadcasted_iota(i32, (CHUNK, hk), 1) % CHUNK
    causals = (rr >= cc, rr <= cc)
    kmask = (lax.broadcasted_iota(i32, (hk, hk), 0) // CHUNK
             == lax.broadcasted_iota(i32, (hk, hk), 1) // GLA_DK)
    vmask = (lax.broadcasted_iota(i32, (hk, hv), 0) // CHUNK
             == lax.broadcasted_iota(i32, (hk, hv), 1) // GLA_DV)
    smask = (lax.broadcasted_iota(i32, (hv, hk), 0) // GLA_DV
             == lax.broadcasted_iota(i32, (hv, hk), 1) // GLA_DK)
    scale = GLA_DK ** -0.5
    zero = jnp.zeros((), bf16)

    def group(gi, carry):
        span = GLA_GROUP * CHUNK
        bases = (gi * span, (nch - (gi + 1) * GLA_GROUP) * CHUNK)
        gls = [_mm_hi(ins[d][3][pl.ds(pl.multiple_of(bases[d], span), span), :], ins[d][4][...]) + ins[d][5][...]
               for d in range(2)]
        chains = []
        for j in range(GLA_GROUP):
            for d in range(2):
                cj = gi * GLA_GROUP + j
                c = cj if d == 0 else nch - 1 - cj
                lo = (j if d == 0 else GLA_GROUP - 1 - j) * CHUNK
                chains.append(dict(d=d, rows=pl.ds(pl.multiple_of(c * CHUNK, CHUNK), CHUNK),
                                   gl=gls[d][lo:lo + CHUNK, :]))
        for ch in chains:
            gl = ch['gl']
            g = (jnp.minimum(gl, 0.0) - jnp.log(1.0 + jnp.exp(-jnp.abs(gl)))) * (1.0 / GLA_TAU)
            ch['gc'] = _mm_mask(tris[ch['d']], g)
        for ch in chains:
            r, rows, gc = ins[ch['d']], ch['rows'], ch['gc']
            gtot = gc[CHUNK - 1:CHUNK, :] if ch['d'] == 0 else gc[0:1, :]
            k = r[1][rows, :].astype(f32)
            ch['qd'] = (r[0][rows, :].astype(f32) * scale * jnp.exp(gc)).astype(bf16)
            ki = (k * jnp.exp(-gc)).astype(bf16)
            ch['kt'] = (k * jnp.exp(gtot - gc)).astype(bf16)
            ch['dec'] = jnp.exp(gtot)
            kstack = jnp.where(kmask, jnp.concatenate([ki] * GLA_H, axis=0), zero)
            ch['sc'] = jnp.where(causals[ch['d']], _nt(ch['qd'], kstack), 0.0).astype(bf16)
        for ch in chains:
            v = ins[ch['d']][2][ch['rows'], :].astype(bf16)
            vbd = jnp.where(vmask, jnp.concatenate([v] * GLA_H, axis=0), zero)
            ch['oi'] = _mm(ch['sc'], vbd)
            ch['kv'] = _tn(v, ch['kt'])
        for ch in chains:
            d = ch['d']
            st = st_ref[d]
            ins[d][6][ch['rows'], :] = (ch['oi'] + _nt(ch['qd'], st.astype(bf16))).astype(bf16)
            st_ref[d] = st * ch['dec'] + jnp.where(smask, ch['kv'], 0.0)
        return carry

    lax.fori_loop(0, nch // GLA_GROUP, group, 0)


def _gla(pa, pb, wgs, bgs, blk=512):
    b, l, _ = pa.shape
    blk = min(blk, l)
    nb = l // blk
    assert l % blk == 0 and (blk // CHUNK) % GLA_GROUP == 0
    hk, hv = GLA_H * GLA_DK, GLA_H * GLA_DV
    up, down = (lambda i: i), (lambda i: nb - 1 - i)

    def cols(bidx):
        col = lambda width, off: pl.BlockSpec((None, blk, width), lambda bi, i: (bi, bidx(i), off // width))
        return [col(hk, A_GLA_Q), col(hk, A_GLA_K), col(hv, A_GLA_V), col(LANES, B_GLA_LR)]

    wspec = pl.BlockSpec((LANES, hk), lambda bi, i: (0, 0))
    bspec = pl.BlockSpec((1, hk), lambda bi, i: (0, 0))
    out = jax.ShapeDtypeStruct((b, l, hv), bf16)
    return pl.pallas_call(
        _gla_body, name="gla",
        out_shape=(out, out),
        grid=(b, nb),
        in_specs=cols(up) + cols(down) + [wspec, bspec, wspec, bspec],
        out_specs=(pl.BlockSpec((None, blk, hv), lambda bi, i: (bi, up(i), 0)),
                   pl.BlockSpec((None, blk, hv), lambda bi, i: (bi, down(i), 0))),
        scratch_shapes=[pltpu.VMEM((2, hv, hk), f32)],
        compiler_params=_cparams("parallel", "arbitrary"),
    )(pa, pa, pa, pb, pa, pa, pa, pb, wgs[0], bgs[0], wgs[1], bgs[1])


def _gdn_prep_body(x_ref, xp_ref, xn_ref, ab_ref, cw_ref, par_ref, qkv_ref, gb_ref):
    i, n = pl.program_id(1), pl.num_programs(1)
    blk = x_ref.shape[0]
    halo = xp_ref.shape[0]
    prev = jnp.where(i > 0, xp_ref[...].astype(f32), 0.0)
    nxt = jnp.where(i < n - 1, xn_ref[...].astype(f32), 0.0)
    ext = jnp.concatenate([prev, x_ref[...].astype(f32), nxt], axis=0)
    tot = blk + 2 * halo
    acc = None
    for t in range(CONV_W):
        sh = (CONV_W // 2 - t) % tot
        xs = ext if sh == 0 else pltpu.roll(ext, sh, 0)
        term = xs[halo:halo + blk, :] * cw_ref[t:t + 1, :]
        acc = term if acc is None else acc + term
    y = acc + acc * jnp.tanh(acc)
    nqk = 2 * GDN_H
    for h in range(3 * GDN_H):
        sl = slice(h * LANES, (h + 1) * LANES)
        yh = y[:, sl]
        if h < nqk:
            yh = yh * lax.rsqrt(jnp.sum(yh * yh, axis=-1, keepdims=True) + EPS)
            if h < GDN_H:
                yh = yh * GDN_DK ** -0.5
        qkv_ref[:, sl] = yh.astype(qkv_ref.dtype)
    x = ab_ref[...]
    lane = lax.broadcasted_iota(i32, x.shape, 1)
    xa = x + par_ref[1:2, :]
    softplus = jnp.maximum(xa, 0.0) + jnp.log(1.0 + jnp.exp(-jnp.abs(xa)))
    gb_ref[...] = jnp.where(lane < nqk, par_ref[0:1, :] * softplus, _sigmoid(x))


def _gdn_prep(pa, pb, conv_w, par, blk=256):
    b, l, _ = pa.shape
    blk = min(blk, l)
    nb = l // blk
    w = 3 * BW
    halo = 2 * SUBLANES
    rh = blk // halo
    last = l // halo - 1
    return pl.pallas_call(
        _gdn_prep_body, name="gdn_prep",
        out_shape=(jax.ShapeDtypeStruct((b, l, w), bf16), jax.ShapeDtypeStruct((b, l, LANES), f32)),
        grid=(b, nb),
        in_specs=[pl.BlockSpec((None, blk, w), lambda bi, i: (bi, i, A_GDN_QKV // w)),
                  pl.BlockSpec((None, halo, w), lambda bi, i: (bi, jnp.maximum(i * rh - 1, 0), A_GDN_QKV // w)),
                  pl.BlockSpec((None, halo, w), lambda bi, i: (bi, jnp.minimum((i + 1) * rh, last), A_GDN_QKV // w)),
                  pl.BlockSpec((None, blk, LANES), lambda bi, i: (bi, i, B_GDN_AB // LANES)),
                  pl.BlockSpec((SUBLANES, w), lambda bi, i: (0, 0)),
                  pl.BlockSpec((SUBLANES, LANES), lambda bi, i: (0, 0))],
        out_specs=(pl.BlockSpec((None, blk, w), lambda bi, i: (bi, i, 0)),
                   pl.BlockSpec((None, blk, LANES), lambda bi, i: (bi, i, 0))),
        compiler_params=_cparams("parallel", "parallel"),
    )(pa, pa, pa, pb, conv_w, par)


def _pair_mm(x, y, bdmask):
    yb = y.astype(bf16)
    return _mm(x.astype(bf16), jnp.where(bdmask, jnp.concatenate([yb, yb], axis=0), jnp.zeros((), bf16)))


def _unit_tri_inverses(lws, eye, bd16, bdmask):
    mm = lambda a, b: _pair_mm(a, b, bdmask)
    lds = [jnp.where(bd16, lw, 0.0) for lw in lws]
    los = [lw - ld for lw, ld in zip(lws, lds)]
    ps = [eye - ld for ld in lds]
    pw = lds
    for _ in range(3):
        pw = [mm(x, x) for x in pw]
        ps = [p + mm(p, x) for p, x in zip(ps, pw)]
    ms = [mm(p, lo) for p, lo in zip(ps, los)]
    m2s = [mm(m, m) for m in ms]
    qs = [eye - m for m in ms]
    qs = [q + mm(q, m2) for q, m2 in zip(qs, m2s)]
    return [mm(q, p) for q, p in zip(qs, ps)]


def _gdn_par_body(q_ref, k_ref, v_ref, gb_ref, *out_refs):
    nch = q_ref.shape[0] // CHUNK
    r64 = lax.broadcasted_iota(i32, (CHUNK, CHUNK), 0)
    c64 = lax.broadcasted_iota(i32, (CHUNK, CHUNK), 1)
    tris = (_ind(r64 >= c64), _ind(r64 <= c64))
    rp = lax.broadcasted_iota(i32, (CHUNK, 2 * CHUNK), 0)
    lp = lax.broadcasted_iota(i32, (CHUNK, 2 * CHUNK), 1)
    fwd_half = lp < CHUNK
    cp = jnp.where(fwd_half, lp, lp - CHUNK)
    incl = (fwd_half & (rp >= cp)) | (~fwd_half & (rp <= cp))
    strict = (fwd_half & (rp > cp)) | (~fwd_half & (rp < cp))
    eye = _ind(rp == cp)
    bd16 = (rp // 16) == (cp // 16)
    bdmask = (lax.broadcasted_iota(i32, (2 * CHUNK, 2 * CHUNK), 0) // CHUNK
              == lax.broadcasted_iota(i32, (2 * CHUNK, 2 * CHUNK), 1) // CHUNK)
    zeros_rhs = jnp.zeros((CHUNK, 2 * GDN_DV), bf16)

    def chunk_group(ci, carry):
        chains = []
        for j in range(GDN_GROUP):
            c = ci * GDN_GROUP + j
            rows = pl.ds(pl.multiple_of(c * CHUNK, CHUNK), CHUNK)
            gcols = gb_ref[rows, :]
            gcols_b = pltpu.roll(gcols, LANES - GDN_H, 1)
            gam_f = _mm_hi(tris[0], gcols)
            gam_b = pltpu.roll(gam_f[CHUNK - 1:CHUNK, :] - gam_f + gcols, LANES - GDN_H, 1)
            grows = jnp.concatenate([gam_f.T[0:SUBLANES, :], gam_b.T[0:SUBLANES, :]], axis=1)
            out_refs[5][c] = jnp.exp(gam_f[CHUNK - 1:CHUNK, :])
            out_refs[11][c] = pltpu.roll(jnp.exp(gam_b[0:1, :]), GDN_H, 1)
            for h in range(GDN_H):
                sl = slice(h * LANES, (h + 1) * LANES)
                qb, kb16 = q_ref[rows, sl], k_ref[rows, sl]
                raw = _nt(jnp.concatenate([qb, kb16], axis=0), jnp.concatenate([kb16, kb16], axis=0))
                chains.append(dict(
                    rows=rows, h=h, raw=raw, qh=qb.astype(f32), kh=kb16.astype(f32), vh=v_ref[rows, sl].astype(f32),
                    gcols=(gam_f[:, h:h + 1], gam_b[:, h:h + 1]), grow=grows[h:h + 1, :],
                    betas=(gcols[:, 2 * GDN_H + h:2 * GDN_H + h + 1], gcols_b[:, 2 * GDN_H + h:2 * GDN_H + h + 1])))
        lws = []
        for ch in chains:
            rows, h, raw = ch['rows'], ch['h'], ch['raw']
            gcol = jnp.where(fwd_half, ch['gcols'][0], ch['gcols'][1])
            beta = jnp.where(fwd_half, ch['betas'][0], ch['betas'][1])
            dec = jnp.where(incl, jnp.exp(jnp.where(incl, gcol - ch['grow'], 0.0)), 0.0)
            attn = (raw[:CHUNK] * dec).astype(bf16)
            for d in range(2):
                out_refs[6 * d + 4][rows, h * CHUNK:(h + 1) * CHUNK] = attn[:, d * CHUNK:(d + 1) * CHUNK]
            lws.append(jnp.where(strict, raw[CHUNK:] * dec * beta, 0.0))
        tinvs = _unit_tri_inverses(lws, eye, bd16, bdmask)
        sols = []
        for ch, tinv in zip(chains, tinvs):
            ch['egs'] = tuple(jnp.exp(g) for g in ch['gcols'])
            rhs = []
            for d in range(2):
                kb = ch['kh'] * ch['betas'][d]
                rhs.append(jnp.concatenate([ch['vh'] * ch['betas'][d], kb * ch['egs'][d]], axis=1).astype(bf16))
            rhs2 = jnp.concatenate([jnp.concatenate([rhs[0], zeros_rhs], axis=1),
                                    jnp.concatenate([zeros_rhs, rhs[1]], axis=1)], axis=0)
            sols.append(_mm(tinv.astype(bf16), rhs2))
        for ch, sol in zip(chains, sols):
            rows = ch['rows']
            sl = slice(ch['h'] * LANES, (ch['h'] + 1) * LANES)
            for d in range(2):
                u_ref, w_ref, qd_ref, kt_ref = out_refs[6 * d:6 * d + 4]
                gcol = ch['gcols'][d]
                gtot = gcol[CHUNK - 1:CHUNK, :] if d == 0 else gcol[0:1, :]
                u_ref[rows, sl] = sol[:, 2 * d * GDN_DV:(2 * d + 1) * GDN_DV]
                w_ref[rows, sl] = sol[:, (2 * d + 1) * GDN_DV:(2 * d + 2) * GDN_DV].astype(bf16)
                qd_ref[rows, sl] = (ch['qh'] * ch['egs'][d]).astype(bf16)
                kt_ref[rows, sl] = (ch['kh'] * jnp.exp(gtot - gcol)).astype(bf16)
        return carry

    lax.fori_loop(0, nch // GDN_GROUP, chunk_group, 0)


def _gdn_par(qkv, gb, blk=512):
    b, l, _ = qkv.shape
    blk = min(blk, l)
    nch = blk // CHUNK
    assert l % blk == 0 and nch % GDN_GROUP == 0
    col = lambda j: pl.BlockSpec((None, blk, BW), lambda bi, i: (bi, i, j))
    wide = pl.BlockSpec((None, blk, BW), lambda bi, i: (bi, i, 0))
    one_dir_shapes = (jax.ShapeDtypeStruct((b, l, BW), f32), jax.ShapeDtypeStruct((b, l, BW), bf16),
                      jax.ShapeDtypeStruct((b, l, BW), bf16), jax.ShapeDtypeStruct((b, l, BW), bf16),
                      jax.ShapeDtypeStruct((b, l, GDN_H * CHUNK), bf16),
                      jax.ShapeDtypeStruct((b, l // CHUNK, 1, LANES), f32))
    one_dir_specs = (wide, wide, wide, wide,
                     pl.BlockSpec((None, blk, GDN_H * CHUNK), lambda bi, i: (bi, i, 0)),
                     pl.BlockSpec((None, nch, 1, LANES), lambda bi, i: (bi, i, 0, 0)))
    outs = pl.pallas_call(
        _gdn_par_body, name="gdn_par",
        out_shape=one_dir_shapes * 2,
        grid=(b, l // blk),
        in_specs=[col(0), col(1), col(2), pl.BlockSpec((None, blk, LANES), lambda bi, i: (bi, i, 0))],
        out_specs=one_dir_specs * 2,
        compiler_params=_cparams("parallel", "parallel"),
    )(qkv, qkv, qkv, gb)
    return outs[:6], outs[6:]


def _gdn_seq_body(*refs):
    ins = (refs[0:6], refs[6:12])
    o_refs = refs[12:14]
    s_ref = refs[14]
    nseq = o_refs[0].shape[0]
    nch = o_refs[0].shape[1] // CHUNK

    @pl.when(pl.program_id(0) == 0)
    def _():
        s_ref[...] = jnp.zeros_like(s_ref)

    def chunk(ci, carry):
        rows, cidx = [], []
        for d in range(2):
            c = ci if d == 0 else nch - 1 - ci
            cidx.append(c)
            rows.append(pl.ds(pl.multiple_of(c * CHUNK, CHUNK), CHUNK))
        ch = [(bi, d, h) for bi in range(nseq) for d in range(2) for h in range(GDN_H)]
        sl = lambda h: slice(h * LANES, (h + 1) * LANES)
        ss = [s_ref[bi, d, sl(h), :] for bi, d, h in ch]
        sbs = [s.astype(bf16) for s in ss]
        wss = [_mm(ins[d][1][bi, rows[d], sl(h)], sb) for (bi, d, h), sb in zip(ch, sbs)]
        qss = [_mm(ins[d][2][bi, rows[d], sl(h)], sb) for (bi, d, h), sb in zip(ch, sbs)]
        vns = [(ins[d][0][bi, rows[d], sl(h)] - ws).astype(bf16) for (bi, d, h), ws in zip(ch, wss)]
        avs = [_mm(ins[d][4][bi, rows[d], h * CHUNK:(h + 1) * CHUNK], vn) for (bi, d, h), vn in zip(ch, vns)]
        kvs = [_tn(ins[d][3][bi, rows[d], sl(h)], vn) for (bi, d, h), vn in zip(ch, vns)]
        for (bi, d, h), s, kv in zip(ch, ss, kvs):
            lg = d * GDN_H + h
            cd = ins[d][5][bi, cidx[d]]
            s_ref[bi, d, sl(h), :] = s * cd[:, lg:lg + 1] + kv
        for bi in range(nseq):
            for d in range(2):
                o_refs[d][bi, rows[d], :] = jnp.concatenate(
                    [qs + av for (b2, d2, _), qs, av in zip(ch, qss, avs) if (b2, d2) == (bi, d)],
                    axis=1).astype(bf16)
        return carry

    lax.fori_loop(0, nch, chunk, 0)


def _gdn_seq(fwd_in, bwd_in, blk=512):
    b, l, _ = fwd_in[0].shape
    blk = min(blk, l)
    nb = l // blk
    nch = blk // CHUNK

    def specs(bidx):
        wide = pl.BlockSpec((b, blk, BW), lambda i: (0, bidx(i), 0))
        return [wide, wide, wide, wide,
                pl.BlockSpec((b, blk, GDN_H * CHUNK), lambda i: (0, bidx(i), 0)),
                pl.BlockSpec((b, nch, 1, LANES), lambda i: (0, bidx(i), 0, 0))]

    up, down = (lambda i: i), (lambda i: nb - 1 - i)
    out = jax.ShapeDtypeStruct((b, l, BW), bf16)
    return pl.pallas_call(
        _gdn_seq_body, name="gdn_seq",
        out_shape=(out, out),
        grid=(nb,),
        in_specs=specs(up) + specs(down),
        out_specs=(pl.BlockSpec((b, blk, BW), lambda i: (0, up(i), 0)),
                   pl.BlockSpec((b, blk, BW), lambda i: (0, down(i), 0))),
        scratch_shapes=[pltpu.VMEM((b, 2, GDN_H * GDN_DK, GDN_DV), f32)],
        compiler_params=_cparams("arbitrary"),
    )(*fwd_in, *bwd_in)


def _head_norm_gate(o, gain, z):
    outs = []
    for h in range(BW // LANES):
        oh = o[:, h * LANES:(h + 1) * LANES]
        outs.append(oh * lax.rsqrt(jnp.mean(oh * oh, axis=-1, keepdims=True) + EPS) * gain)
    return jnp.concatenate(outs, axis=1) * _silu(z)


def _merge_body(h_ref, gate_ref, ch_ref, u_ref, glaf_ref, glab_ref, r_ref, gdnf_ref, gdnb_ref, z_ref,
                dsk_ref, wglu_ref, gng_ref, dng_ref, wbr_ref, wout_ref, o_ref):
    up = lambda ref: ref[...].astype(f32)
    y0 = _gelu_tanh(ch_ref[...] + dsk_ref[...] * u_ref[...])
    y_s5 = y0 * _sigmoid(_mm(y0.astype(bf16), wglu_ref[...]))
    y_gla = _head_norm_gate(up(glaf_ref) + up(glab_ref), gng_ref[...], up(r_ref))
    y_gdn = _head_norm_gate(up(gdnf_ref) + up(gdnb_ref), dng_ref[...], up(z_ref))
    merged = None
    for r, y in enumerate((y_s5, y_gla, y_gdn)):
        th = jnp.tanh(gate_ref[:, r * D_MODEL:(r + 1) * D_MODEL].astype(f32))
        t_half = _mm(y.astype(bf16), wbr_ref[r])
        term = t_half + t_half * th
        merged = term if merged is None else merged + term
    o_ref[...] = h_ref[...] + _mm(merged.astype(bf16), wout_ref[...])


def _merge(h2d, pa2d, pb2d, ch, gla_f, gla_b, gdn_f, gdn_b, dsk, wglu, gng, dng, wbr, wout, tm=512):
    m = h2d.shape[0]
    row = lambda width, off=0: pl.BlockSpec((tm, width), lambda i: (i, off // width))
    full = lambda shape: pl.BlockSpec(shape, lambda i: (0,) * len(shape))
    return pl.pallas_call(
        _merge_body, name="merge",
        out_shape=jax.ShapeDtypeStruct((m, D_MODEL), f32),
        grid=(m // tm,),
        in_specs=[row(D_MODEL), row(3 * D_MODEL, A_GATE), row(BW), row(BW, B_S5), row(BW), row(BW),
                  row(BW, A_GLA_R), row(BW), row(BW), row(BW, A_GDN_Z),
                  full((1, BW)), full((BW, BW)), full((1, LANES)), full((1, LANES)),
                  full((3, BW, D_MODEL)), full((D_MODEL, D_MODEL))],
        out_specs=row(D_MODEL),
        compiler_params=_cparams("parallel"),
    )(h2d, pa2d, ch, pb2d, gla_f, gla_b, pa2d, gdn_f, gdn_b, pa2d, dsk, wglu, gng, dng, wbr, wout)


def _router_body(h_ref, g_ref, wr_ref, hn_ref, aff_ref):
    hn = _rms(h_ref[...], g_ref[...])
    hn_hi = hn.astype(bf16)
    hn_ref[...] = hn_hi
    hn_lo = (hn - hn_hi.astype(f32)).astype(bf16)
    wr = wr_ref[...]
    wr_hi = wr.astype(bf16)
    wr_lo = (wr - wr_hi.astype(f32)).astype(bf16)
    logits = _nt(wr_hi, hn_hi) + (_nt(wr_hi, hn_lo) + _nt(wr_lo, hn_hi))
    e = jnp.exp(logits - jnp.max(logits, axis=0, keepdims=True))
    aff_ref[...] = e / jnp.sum(e, axis=0, keepdims=True)


def _router(h2d, gain, wr_t, tm=512):
    m = h2d.shape[0]
    return pl.pallas_call(
        _router_body, name="router",
        out_shape=(jax.ShapeDtypeStruct((m, D_MODEL), bf16), jax.ShapeDtypeStruct((N_EXPERTS, m), f32)),
        grid=(m // tm,),
        in_specs=[pl.BlockSpec((tm, D_MODEL), lambda i: (i, 0)),
                  pl.BlockSpec((1, D_MODEL), lambda i: (0, 0)),
                  pl.BlockSpec((N_EXPERTS, D_MODEL), lambda i: (0, 0))],
        out_specs=(pl.BlockSpec((tm, D_MODEL), lambda i: (i, 0)), pl.BlockSpec((N_EXPERTS, tm), lambda i: (0, i))),
        compiler_params=_cparams("parallel"),
    )(h2d, gain, wr_t)


def _threshold_body(aff_ref, thr_ref, *, cap):
    keys = pltpu.bitcast(aff_ref[...], i32)

    def count(mask):
        return jnp.sum(jnp.where(mask, 1.0, 0.0), axis=1, keepdims=True).astype(i32)

    def bit(bi, t):
        cand = t | (1 << (30 - bi))
        return jnp.where(count(keys >= cand) >= cap, cand, t)

    t = lax.fori_loop(0, 31, bit, jnp.zeros((N_EXPERTS, 1), i32))
    budget = cap - count(keys > t)
    lane = lax.broadcasted_iota(i32, (N_EXPERTS, LANES), 1)
    thr_ref[...] = jnp.where(lane == 0, t, jnp.where(lane == 1, budget, 0))


def _threshold(aff_t, cap):
    n = aff_t.shape[1]
    return pl.pallas_call(
        functools.partial(_threshold_body, cap=cap), name="topc_threshold",
        out_shape=jax.ShapeDtypeStruct((N_EXPERTS, LANES), i32),
        in_specs=[pl.BlockSpec((N_EXPERTS, n), lambda: (0, 0))],
        out_specs=pl.BlockSpec((N_EXPERTS, LANES), lambda: (0, 0)),
        compiler_params=pltpu.CompilerParams(vmem_limit_bytes=VMEM_LIMIT_BYTES),
    )(aff_t)


def _slots_body(aff_ref, thr_ref, slot_ref, wts_ref, cnt_ref, run_ref):
    @pl.when(pl.program_id(0) == 0)
    def _():
        run_ref[...] = jnp.zeros_like(run_ref)

    tt = aff_ref.shape[1]
    aff = aff_ref[...]
    keys = pltpu.bitcast(aff, i32)
    t = thr_ref[:, 0:1]
    budget = thr_ref[:, 1:2]
    upper = _ind(lax.broadcasted_iota(i32, (tt, tt), 0) <= lax.broadcasted_iota(i32, (tt, tt), 1), bf16)
    eq = keys == t
    sel_run = run_ref[:, 0:1]
    tie_run = run_ref[:, 1:2]
    cs_eq = _mm(_ind(eq, bf16), upper).astype(i32)
    tie_rank = tie_run + cs_eq - 1
    sel = (keys > t) | (eq & (tie_rank < budget))
    cs_sel = _mm(_ind(sel, bf16), upper).astype(i32)
    slot_ref[...] = jnp.where(sel, sel_run + cs_sel - 1, -1)
    wts_ref[...] = jnp.where(sel, aff, 0.0)
    n_sel = cs_sel[:, tt - 1:tt]
    n_eq = cs_eq[:, tt - 1:tt]
    n_rows = ((n_sel + (SUBLANES - 1)) // SUBLANES) * SUBLANES
    cnt_ref[...] = jnp.broadcast_to(n_rows, cnt_ref.shape)
    lane = lax.broadcasted_iota(i32, run_ref.shape, 1)
    run_ref[...] = run_ref[...] + jnp.where(lane == 0, n_rows, jnp.where(lane == 1, n_eq, 0))


def _slots(aff_t, thr):
    n = aff_t.shape[1]
    nt = n // MOE_TT
    return pl.pallas_call(
        _slots_body, name="topc_slots",
        out_shape=(jax.ShapeDtypeStruct((N_EXPERTS, n), i32), jax.ShapeDtypeStruct((N_EXPERTS, n), f32),
                   jax.ShapeDtypeStruct((nt, N_EXPERTS, LANES), i32)),
        grid=(nt,),
        in_specs=[pl.BlockSpec((N_EXPERTS, MOE_TT), lambda i: (0, i)),
                  pl.BlockSpec((N_EXPERTS, LANES), lambda i: (0, 0))],
        out_specs=(pl.BlockSpec((N_EXPERTS, MOE_TT), lambda i: (0, i)),
                   pl.BlockSpec((N_EXPERTS, MOE_TT), lambda i: (0, i)),
                   pl.BlockSpec((None, N_EXPERTS, LANES), lambda i: (i, 0, 0))),
        scratch_shapes=[pltpu.VMEM((N_EXPERTS, LANES), i32)],
        compiler_params=_cparams("arbitrary"),
    )(aff_t, thr)


def _window_hits(slot_rows, starts, width):
    tt = slot_rows.shape[1]
    r = lax.broadcasted_iota(i32, (width, tt), 0)
    return [r == (slot_rows[e:e + 1, :] - starts[e]) for e in range(len(starts))]


def _dispatch_body(off_ref, x_ref, slot_ref, xe_hbm, buf_ref, xbuf_ref, sem, *, cap):
    i = pl.program_id(0)
    w = MOE_W
    x = x_ref[...]
    slot_rows = slot_ref[...]
    starts = [off_ref[e, i] for e in range(N_EXPERTS)]

    def window_copy(e, start, src):
        return pltpu.make_async_copy(src, xe_hbm.at[e, pl.ds(pl.multiple_of(start, SUBLANES), w)], sem.at[e])

    def window_wait(e):
        pltpu.make_async_copy(buf_ref.at[e], xe_hbm.at[e, pl.ds(0, w)], sem.at[e]).wait()

    @pl.when(i > 0)
    def _():
        for e in range(N_EXPERTS):
            window_wait(e)

    @pl.when(i == 0)
    def _():
        xbuf_ref[...] = jnp.zeros_like(xbuf_ref)
        tails = [pltpu.make_async_copy(xbuf_ref, xe_hbm.at[e, pl.ds(r0, MOE_PAD)], sem.at[N_EXPERTS])
                 for e in range(N_EXPERTS) for r0 in range(cap, xe_hbm.shape[1], MOE_PAD)]
        for cp in tails:
            cp.start()
        for cp in tails:
            cp.wait()

    hits = _window_hits(slot_rows, starts, w)
    lhs = jnp.concatenate([_ind(h, bf16) for h in hits], axis=0)
    rows = _mm(lhs, x).astype(bf16)
    for e in range(N_EXPERTS):
        buf_ref[e] = rows[e * w:(e + 1) * w, :]
        window_copy(e, starts[e], buf_ref.at[e]).start()

    for e in range(N_EXPERTS):
        n_rows = off_ref[e, i + 1] - starts[e]

        def extra(k, carry, e=e):
            start = starts[e] + k * w
            hit = _window_hits(slot_rows[e:e + 1, :], [start], w)[0]
            xbuf_ref[0:w, :] = _mm(_ind(hit, bf16), x).astype(bf16)
            cp = pltpu.make_async_copy(xbuf_ref.at[pl.ds(0, w)],
                                       xe_hbm.at[e, pl.ds(pl.multiple_of(start, SUBLANES), w)], sem.at[N_EXPERTS])
            cp.start()
            cp.wait()
            return carry

        lax.fori_loop(1, (n_rows + (w - 1)) // w, extra, 0)

    @pl.when(i == pl.num_programs(0) - 1)
    def _():
        for e in range(N_EXPERTS):
            window_wait(e)


def _dispatch(hn, slot, off, rows_alloc, cap):
    n = hn.shape[0]
    grid_spec = pltpu.PrefetchScalarGridSpec(
        num_scalar_prefetch=1,
        grid=(n // MOE_TT,),
        in_specs=[pl.BlockSpec((MOE_TT, D_MODEL), lambda i, o: (i, 0)),
                  pl.BlockSpec((N_EXPERTS, MOE_TT), lambda i, o: (0, i))],
        out_specs=pl.BlockSpec(memory_space=pl.ANY),
        scratch_shapes=[pltpu.VMEM((N_EXPERTS, MOE_W, D_MODEL), bf16), pltpu.VMEM((MOE_PAD, D_MODEL), bf16),
                        pltpu.SemaphoreType.DMA((N_EXPERTS + 1,))])
    return pl.pallas_call(
        functools.partial(_dispatch_body, cap=cap), name="dispatch",
        out_shape=jax.ShapeDtypeStruct((N_EXPERTS, rows_alloc, D_MODEL), bf16),
        grid_spec=grid_spec,
        compiler_params=_cparams("arbitrary"),
    )(off, hn, slot)


def _experts_body(tot_ref, x_ref, wg_ref, wu_ref, wd_ref, y_ref):
    e, j = pl.program_id(0), pl.program_id(1)
    fb = x_ref.shape[0]
    n_valid = tot_ref[e] - j * fb

    half = fb // 2

    def ffn(x, rows):
        hid = (_silu(_mm(x, wg_ref[...])) * _mm(x, wu_ref[...])).astype(bf16)
        y_ref[0:rows, :] = _mm(hid, wd_ref[...]).astype(y_ref.dtype)

    def masked(rows):
        row = lax.broadcasted_iota(i32, (rows, 1), 0)
        return jnp.where(row < n_valid, x_ref[0:rows, :], jnp.zeros((), bf16))

    @pl.when(n_valid >= fb)
    def _():
        ffn(x_ref[...], fb)

    @pl.when((n_valid > half) & (n_valid < fb))
    def _():
        ffn(masked(fb), fb)

    @pl.when((n_valid > 0) & (n_valid <= half))
    def _():
        ffn(masked(half), half)
        y_ref[half:fb, :] = jnp.zeros((fb - half, y_ref.shape[1]), y_ref.dtype)

    @pl.when(n_valid <= 0)
    def _():
        y_ref[...] = jnp.zeros_like(y_ref)


def _experts(xe, total, wg, wu, wd, rows):
    grid_spec = pltpu.PrefetchScalarGridSpec(
        num_scalar_prefetch=1,
        grid=(N_EXPERTS, rows // MOE_FB),
        in_specs=[pl.BlockSpec((None, MOE_FB, D_MODEL), lambda e, j, t: (e, j, 0)),
                  pl.BlockSpec((None, D_MODEL, EXPERT_FF), lambda e, j, t: (e, 0, 0)),
                  pl.BlockSpec((None, D_MODEL, EXPERT_FF), lambda e, j, t: (e, 0, 0)),
                  pl.BlockSpec((None, EXPERT_FF, D_MODEL), lambda e, j, t: (e, 0, 0))],
        out_specs=pl.BlockSpec((None, MOE_FB, D_MODEL), lambda e, j, t: (e, j, 0)))
    return pl.pallas_call(
        _experts_body, name="experts",
        out_shape=jax.ShapeDtypeStruct((N_EXPERTS, rows, D_MODEL), bf16),
        grid_spec=grid_spec,
        compiler_params=_cparams("parallel", "arbitrary"),
    )(total, xe, wg, wu, wd)


def _combine_body(off_ref, h_ref, slot_ref, wts_ref, p_ref, g_ref, wpg_ref, wpp_ref, ye_hbm, o_ref,
                  win_ref, xwin_ref, acc_ref, sem, *, rows):
    i, n = pl.program_id(0), pl.num_programs(0)
    w = MOE_W
    tt = h_ref.shape[0]

    def wstart(e, tile, k=0):
        return pl.multiple_of(jnp.minimum(off_ref[e, tile] + k * w, rows - w), SUBLANES)

    def window_copy(e, tile, par):
        return pltpu.make_async_copy(ye_hbm.at[e, pl.ds(wstart(e, tile), w)], win_ref.at[par, e], sem.at[par, e])

    @pl.when(i == 0)
    def _():
        for e in range(N_EXPERTS):
            window_copy(e, 0, 0).start()

    @pl.when(i + 1 < n)
    def _():
        for e in range(N_EXPERTS):
            window_copy(e, i + 1, (i + 1) % 2).start()

    par = i % 2
    slot_rows = slot_ref[...]
    wts_rows = wts_ref[...]
    r = lax.broadcasted_iota(i32, (w, tt), 0)
    hits, wins = [], []
    for e in range(N_EXPERTS):
        window_copy(e, i, par).wait()
        s_row = slot_rows[e:e + 1, :]
        lo = off_ref[e, i]
        hit = (r == s_row - wstart(e, i)) & (s_row < lo + w)
        gsel = jnp.where(hit, wts_rows[e:e + 1, :], 0.0)
        part = gsel[:, 0:LANES]
        for c in range(1, tt // LANES):
            part = part + gsel[:, c * LANES:(c + 1) * LANES]
        slot_gate = jnp.sum(part, axis=1, keepdims=True)
        hits.append(_ind(hit, bf16))
        wins.append((win_ref[par, e].astype(f32) * slot_gate).astype(bf16))
    acc_ref[...] = h_ref[...] + _tn(jnp.concatenate(hits, axis=0), jnp.concatenate(wins, axis=0))
    gates = jnp.transpose(wts_rows)

    for e in range(N_EXPERTS):
        lo = off_ref[e, i]
        n_rows = off_ref[e, i + 1] - lo

        def extra(k, carry, e=e, lo=lo):
            cp = pltpu.make_async_copy(ye_hbm.at[e, pl.ds(wstart(e, i, k), w)], xwin_ref, sem.at[2, 0])
            cp.start()
            cp.wait()
            s_row = slot_rows[e:e + 1, :]
            hit = (r == s_row - wstart(e, i, k)) & (s_row >= lo + k * w) & (s_row < lo + (k + 1) * w)
            acc_ref[...] += gates[:, e:e + 1] * _tn(_ind(hit, bf16), xwin_ref[...])
            return carry

        lax.fori_loop(1, (n_rows + (w - 1)) // w, extra, 0)

    h2 = acc_ref[...]
    gate = _sigmoid(_mm(_rms(h2, g_ref[...]).astype(bf16), wpg_ref[...]))
    o_ref[...] = h2 + gate * _mm(p_ref[...].astype(bf16), wpp_ref[...])


def _combine(h2d, slot, wts, ye, p2d, off, g_ple, wpg, wpp):
    m = h2d.shape[0]
    rows = ye.shape[1]
    full = lambda shape: pl.BlockSpec(shape, lambda i, o: (0,) * len(shape))
    grid_spec = pltpu.PrefetchScalarGridSpec(
        num_scalar_prefetch=1,
        grid=(m // MOE_TT,),
        in_specs=[pl.BlockSpec((MOE_TT, D_MODEL), lambda i, o: (i, 0)),
                  pl.BlockSpec((N_EXPERTS, MOE_TT), lambda i, o: (0, i)),
                  pl.BlockSpec((N_EXPERTS, MOE_TT), lambda i, o: (0, i)),
                  pl.BlockSpec((MOE_TT, PLE_DIM), lambda i, o: (i, 0)),
                  full((1, D_MODEL)), full((D_MODEL, D_MODEL)), full((PLE_DIM, D_MODEL)),
                  pl.BlockSpec(memory_space=pl.ANY)],
        out_specs=pl.BlockSpec((MOE_TT, D_MODEL), lambda i, o: (i, 0)),
        scratch_shapes=[pltpu.VMEM((2, N_EXPERTS, MOE_W, D_MODEL), bf16), pltpu.VMEM((MOE_W, D_MODEL), bf16),
                        pltpu.VMEM((MOE_TT, D_MODEL), f32), pltpu.SemaphoreType.DMA((3, N_EXPERTS))])
    return pl.pallas_call(
        functools.partial(_combine_body, rows=rows), name="combine_ple",
        out_shape=jax.ShapeDtypeStruct((m, D_MODEL), f32),
        grid_spec=grid_spec,
        compiler_params=_cparams("arbitrary"),
    )(off, h2d, slot, wts, p2d, g_ple, wpg, wpp, ye)


def _moe_ple(h2d, p2d, g_ffn, wr_t, wg, wu, wd, g_ple, wpg, wpp):
    n = h2d.shape[0]
    nt = n // MOE_TT
    cap = max(1, EC_FACTOR * n // N_EXPERTS)
    rows = -(-(cap + SUBLANES * nt) // MOE_FB) * MOE_FB
    assert n % MOE_TT == 0 and cap % MOE_PAD == 0 and MOE_W <= MOE_PAD and MOE_W <= cap
    hn, aff_t = _router(h2d, g_ffn, wr_t)
    thr = _threshold(aff_t, cap)
    slot, wts, cnt = _slots(aff_t, thr)
    off = jnp.concatenate([jnp.zeros((1, N_EXPERTS), i32), jnp.cumsum(cnt[:, :, 0], axis=0, dtype=i32)], axis=0)
    off = jnp.transpose(off)
    xe = _dispatch(hn, slot, off, rows + MOE_PAD, cap)
    ye = _experts(xe, off[:, nt], wg, wu, wd, rows)
    return _combine(h2d, slot, wts, ye, p2d, off, g_ple, wpg, wpp)


def _final_norm_body(x_ref, g_ref, o_ref):
    o_ref[...] = _rms(x_ref[...], g_ref[...])


def _final_norm(h2d, gain, tm=1024):
    m = h2d.shape[0]
    return pl.pallas_call(
        _final_norm_body, name="final_norm",
        out_shape=jax.ShapeDtypeStruct((m, D_MODEL), f32),
        grid=(m // tm,),
        in_specs=[pl.BlockSpec((tm, D_MODEL), lambda i: (i, 0)), pl.BlockSpec((1, D_MODEL), lambda i: (0, 0))],
        out_specs=pl.BlockSpec((tm, D_MODEL), lambda i: (i, 0)),
        compiler_params=_cparams("parallel"),
    )(h2d, gain)


def _layer_weights(w):
    lw = {}
    lw['g_mix'] = w['norm_mix'].reshape(1, D_MODEL)
    lw['w_in'] = _reorder_w_in(w['w_in'])
    lw['s5'] = _s5_weights(w['s5_B_re'], w['s5_B_im'], w['s5_C_re'], w['s5_C_im'],
                           w['s5_lam_re'], w['s5_lam_im'], w['s5_log_dt'])
    wgate = w['gla_w_gate']
    lw['gla_wg'] = tuple(jnp.zeros((LANES, GLA_H * GLA_DK), f32).at[d * GLA_RANK:(d + 1) * GLA_RANK].set(wgate[d])
                         for d in range(2))
    lw['gla_bg'] = tuple(w['gla_b_gate'][d].reshape(1, -1) for d in range(2))
    lw['conv_w'] = jnp.pad(0.5 * jnp.transpose(w['gdn_conv']), ((0, SUBLANES - CONV_W), (0, 0)))
    neg_a = -jnp.exp(w['gdn_A_log']).reshape(-1)
    par = jnp.zeros((SUBLANES, LANES), f32).at[0, :2 * GDN_H].set(neg_a).at[1, :2 * GDN_H].set(
        w['gdn_dt_bias'].reshape(-1))
    lw['gdn_par'] = par
    lw['dsk'] = w['s5_D'].reshape(1, BW)
    lw['wglu'] = w['s5_w_glu'].astype(bf16)
    lw['gng'] = w['gla_norm'].reshape(1, LANES)
    lw['dng'] = w['gdn_norm'].reshape(1, LANES)
    lw['wbr'] = (0.5 * w['w_branch']).astype(bf16)
    lw['wout'] = w['w_out'].astype(bf16)
    lw['g_ffn'] = w['norm_ffn'].reshape(1, D_MODEL)
    lw['wr_t'] = jnp.transpose(w['w_router'])
    lw['wg'] = w['w_exp_gate'].astype(bf16)
    lw['wu'] = w['w_exp_up'].astype(bf16)
    lw['wd'] = w['w_exp_down'].astype(bf16)
    lw['g_ple'] = w['norm_ple'].reshape(1, D_MODEL)
    lw['wpg'] = w['w_ple_gate'].astype(bf16)
    lw['wpp'] = w['w_ple_proj'].astype(bf16)
    return lw


def _mixers(h, lw):
    b, l, _ = h.shape
    h2d = h.reshape(b * l, D_MODEL)
    pa2d, pb2d = _inproj(h2d, lw['g_mix'], lw['w_in'])
    pa, pb = pa2d.reshape(b, l, D_A), pb2d.reshape(b, l, D_B)
    ch = _s5_mixer(pb, lw['s5'])
    gla_f, gla_b = _gla(pa, pb, lw['gla_wg'], lw['gla_bg'])
    qkv, gb = _gdn_prep(pa, pb, lw['conv_w'], lw['gdn_par'])
    gdn_f, gdn_b = _gdn_seq(*_gdn_par(qkv, gb))
    flat = lambda a: a.reshape(b * l, a.shape[-1])
    return _merge(h2d, pa2d, pb2d, flat(ch), flat(gla_f), flat(gla_b), flat(gdn_f), flat(gdn_b),
                  lw['dsk'], lw['wglu'], lw['gng'], lw['dng'], lw['wbr'], lw['wout'])


def _layer(h, p_i, lw):
    b, l, _ = h.shape
    h1 = _mixers(h, lw)
    h3 = _moe_ple(h1, p_i.reshape(b * l, PLE_DIM), lw['g_ffn'], lw['wr_t'], lw['wg'], lw['wu'], lw['wd'],
                  lw['g_ple'], lw['wpg'], lw['wpp'])
    return h3.reshape(b, l, D_MODEL)


def kernel(x_prompt, x_sample, p_prompt, p_sample, norm_mix, w_in, s5_B_re, s5_B_im, s5_C_re, s5_C_im, s5_D, s5_lam_re, s5_lam_im, s5_log_dt, s5_w_glu, gla_w_gate, gla_b_gate, gla_norm, gdn_conv, gdn_A_log, gdn_dt_bias, gdn_norm, w_branch, w_out, norm_ffn, w_router, w_exp_gate, w_exp_up, w_exp_down, norm_ple, w_ple_gate, w_ple_proj, norm_final):
    weights = dict(norm_mix=norm_mix, w_in=w_in, s5_B_re=s5_B_re, s5_B_im=s5_B_im, s5_C_re=s5_C_re, s5_C_im=s5_C_im,
                   s5_D=s5_D, s5_lam_re=s5_lam_re, s5_lam_im=s5_lam_im, s5_log_dt=s5_log_dt, s5_w_glu=s5_w_glu,
                   gla_w_gate=gla_w_gate, gla_b_gate=gla_b_gate, gla_norm=gla_norm, gdn_conv=gdn_conv,
                   gdn_A_log=gdn_A_log, gdn_dt_bias=gdn_dt_bias, gdn_norm=gdn_norm, w_branch=w_branch, w_out=w_out,
                   norm_ffn=norm_ffn, w_router=w_router, w_exp_gate=w_exp_gate, w_exp_up=w_exp_up,
                   w_exp_down=w_exp_down, norm_ple=norm_ple, w_ple_gate=w_ple_gate, w_ple_proj=w_ple_proj)

    def body(carry, xs):
        hp, hs = carry
        w_i, pp, ps = xs
        lw = _layer_weights(w_i)
        return (_layer(hp, pp, lw), _layer(hs, ps, lw)), None

    (hp, hs), _ = lax.scan(body, (x_prompt.astype(f32), x_sample.astype(f32)), (weights, p_prompt, p_sample))
    g_fin = norm_final.reshape(1, D_MODEL)
    yp = _final_norm(hp.reshape(-1, D_MODEL), g_fin).reshape(x_prompt.shape).astype(x_prompt.dtype)
    ys = _final_norm(hs.reshape(-1, D_MODEL), g_fin).reshape(x_sample.shape).astype(x_sample.dtype)
    return (yp, ys)
```

```python
import functools
import math

import jax
import jax.numpy as jnp
import numpy as np
from jax import lax
from jax.experimental import pallas as pl
from jax.experimental.pallas import tpu as pltpu

f32 = jnp.float32
bf16 = jnp.bfloat16
i32 = jnp.int32
HIGHEST = lax.Precision.HIGHEST

D_MODEL = 1024
DEPTH = 4
PLE_DIM = 256
BW = 512
EPS = 1e-6
CHUNK = 64
S5_GROUPS, S5_GC, S5_STATE = 32, 16, 64
GLA_H, GLA_DK, GLA_DV, GLA_RANK, GLA_TAU = 4, 64, 128, 16, 16.0
GDN_H, GDN_DK, GDN_DV, CONV_W = 4, 128, 128, 5
N_EXPERTS, EXPERT_FF, EC_FACTOR = 16, 2048, 2

LANES = 128
SUBLANES = 8
VMEM_LIMIT_BYTES = 56 * 1024 * 1024

A_GATE, A_GDN_QKV, A_GDN_Z, A_GLA_V, A_GLA_R, A_GLA_Q, A_GLA_K = 0, 3072, 4608, 5120, 5632, 6144, 6400
D_A = 6656
B_S5, B_GLA_LR, B_GDN_AB = 0, 512, 640
D_B = 768
D_INP = D_A + D_B

S5_T = 8
S5_LB = BW // LANES
S5_SW = 8 * S5_STATE * 2
S5_SCAN_CB = 2

GDN_GROUP = 4
GLA_GROUP = 4

MOE_TT = 512
MOE_W = 96
MOE_PAD = 128
MOE_FB = 512


def _cparams(*sem):
    return pltpu.CompilerParams(dimension_semantics=sem, vmem_limit_bytes=VMEM_LIMIT_BYTES)


def _nt(a, b):
    return lax.dot_general(a, b, (((1,), (1,)), ((), ())), preferred_element_type=f32)


def _tn(a, b):
    return lax.dot_general(a, b, (((0,), (0,)), ((), ())), preferred_element_type=f32)


def _mm(a, b):
    return jnp.dot(a, b, preferred_element_type=f32)


def _mm_hi(a, b):
    return jnp.dot(a, b, preferred_element_type=f32, precision=HIGHEST)


def _mmb(a, b):
    return _mm(a.astype(bf16), b.astype(bf16))


def _split3(x):
    hi = x.astype(bf16)
    r = x - hi.astype(f32)
    mid = r.astype(bf16)
    return hi, mid, (r - mid.astype(f32)).astype(bf16)


def _mm_mask(mask01, x):
    m = mask01.astype(bf16)
    hi, mid, lo = _split3(x)
    return _mm(m, hi) + (_mm(m, mid) + _mm(m, lo))


def _ind(mask, dtype=f32):
    return jnp.where(mask, 1.0, 0.0).astype(dtype)


def _sigmoid(x):
    return 0.5 * jnp.tanh(0.5 * x) + 0.5


def _silu(x):
    return x * _sigmoid(x)


def _gelu_tanh(x):
    return 0.5 * x * (1.0 + jnp.tanh(math.sqrt(2.0 / math.pi) * (x + 0.044715 * (x * x * x))))


def _rms(x, g):
    return x * lax.rsqrt(jnp.mean(x * x, axis=-1, keepdims=True) + EPS) * g


def _inproj_body(x_ref, g_ref, w_ref, oa_ref, ob_ref):
    xn = _rms(x_ref[...], g_ref[...]).astype(bf16)
    oa_ref[...] = _mm(xn, w_ref[:, :D_A]).astype(bf16)
    ob_ref[...] = _mm(xn, w_ref[:, D_A:])


def _inproj(x2d, gain, w_p, tm=512):
    m = x2d.shape[0]
    return pl.pallas_call(
        _inproj_body, name="inproj",
        out_shape=(jax.ShapeDtypeStruct((m, D_A), bf16), jax.ShapeDtypeStruct((m, D_B), f32)),
        grid=(m // tm,),
        in_specs=[pl.BlockSpec((tm, D_MODEL), lambda i: (i, 0)),
                  pl.BlockSpec((1, D_MODEL), lambda i: (0, 0)),
                  pl.BlockSpec((D_MODEL, D_INP), lambda i: (0, 0), pipeline_mode=pl.Buffered(1))],
        out_specs=(pl.BlockSpec((tm, D_A), lambda i: (i, 0)), pl.BlockSpec((tm, D_B), lambda i: (i, 0))),
        compiler_params=_cparams("parallel"),
    )(x2d, gain, w_p)


def _reorder_w_in(w_in):
    o = np.cumsum((0, 512, 256, 256, 512, 512, 32, 512, 512, 512, 512, 8, 8, 3072))
    seg = lambda k: w_in[:, o[k]:o[k + 1]]
    zpad = lambda a: jnp.pad(a, ((0, 0), (0, LANES - a.shape[1])))
    parts = [0.5 * seg(12), seg(6), seg(7), seg(8), seg(9), seg(3), seg(4), seg(1), seg(2),
             seg(0), zpad(seg(5)), zpad(jnp.concatenate([seg(10), seg(11)], axis=1))]
    return jnp.concatenate(parts, axis=1).astype(bf16)


def _s5_weights(b_re, b_im, c_re, c_im, lam_re, lam_im, log_dt):
    T = S5_T
    dt = jnp.exp(log_dt)[:, :, None]
    lr, li = lam_re, lam_im
    mag = jnp.exp(lr * dt)
    ab_re, ab_im = mag * jnp.cos(li * dt), mag * jnp.sin(li * dt)
    den = lr * lr + li * li
    num_re = ab_re - 1.0
    coef_re = (num_re * lr + ab_im * li) / den
    coef_im = (ab_im * lr - num_re * li) / den
    xb_re = coef_re[..., None] * b_re[None] - coef_im[..., None] * b_im[None]
    xb_im = coef_re[..., None] * b_im[None] + coef_im[..., None] * b_re[None]

    def powers(taus):
        tau = jnp.asarray(taus, lr.dtype)
        pm = jnp.exp((lr * dt)[..., None] * tau)
        ang = (li * dt)[..., None] * tau
        return pm * jnp.cos(ang), pm * jnp.sin(ang)

    p_re, p_im = powers(np.arange(T + 1))
    cp_re = c_re[None, :, :, :, None] * p_re[:, :, None] - c_im[None, :, :, :, None] * p_im[:, :, None]
    cp_im = c_re[None, :, :, :, None] * p_im[:, :, None] + c_im[None, :, :, :, None] * p_re[:, :, None]
    kern = (jnp.einsum('dgknt,dgnc->dgtkc', cp_re, xb_re, precision=HIGHEST)
            - jnp.einsum('dgknt,dgnc->dgtkc', cp_im, xb_im, precision=HIGHEST))
    s_idx = np.arange(T)[:, None]
    t_idx = np.arange(T)[None, :]
    lag_f = np.clip(t_idx - s_idx, 0, T)
    lag_b = np.clip(s_idx - t_idx, 0, T)
    m_f = jnp.asarray((t_idx >= s_idx), kern.dtype)[None, :, :, None, None]
    m_b = jnp.asarray((s_idx >= t_idx), kern.dtype)[None, :, :, None, None]
    a_g = kern[0][:, lag_f] * m_f + kern[1][:, lag_b] * m_b
    dtype = kern.dtype
    wide = T * LANES
    col = np.arange(wide)
    row_grp_lane = lax.broadcasted_iota(i32, (wide, wide), 0) // S5_GC % 8
    col_grp_lane = lax.broadcasted_iota(i32, (wide, wide), 1) // S5_GC % 8
    st_idx = lambda ax: (2 * (lax.broadcasted_iota(i32, (wide, wide), ax) // (2 * LANES))
                         + lax.broadcasted_iota(i32, (wide, wide), ax) % LANES // S5_STATE)
    exp_tk = jnp.asarray((np.arange(LANES)[:, None] // S5_GC == col[None, :] // LANES)
                         & (np.arange(LANES)[:, None] % S5_GC == col[None, :] % S5_GC), dtype)
    exp_pn = jnp.asarray((np.arange(LANES)[:, None] // S5_STATE == col[None, :] % (2 * LANES) // LANES)
                         & (np.arange(LANES)[:, None] % S5_STATE == col[None, :] % S5_STATE), dtype)

    def expand(compact, expansion, mask):
        return jnp.where(mask, jnp.einsum('jrm,mn->jrn', compact, expansion, precision=HIGHEST), 0.0)

    a_c = jnp.transpose(a_g.reshape(S5_LB, 8, T, T, S5_GC, S5_GC), (0, 2, 1, 5, 3, 4))
    a_blk = expand(a_c.reshape(S5_LB, wide, LANES), exp_tk, row_grp_lane == col_grp_lane)

    def state_in(d, taus):
        e_re = p_re[d][:, :, taus][..., None] * xb_re[d][:, :, None, :] - p_im[d][:, :, taus][..., None] * xb_im[d][:, :, None, :]
        e_im = p_re[d][:, :, taus][..., None] * xb_im[d][:, :, None, :] + p_im[d][:, :, taus][..., None] * xb_re[d][:, :, None, :]
        e = jnp.stack([e_re, e_im], axis=0)
        e = e.reshape(2, S5_LB, 8, S5_STATE, T, S5_GC)
        e = jnp.transpose(e, (1, 4, 2, 5, 0, 3))
        return expand(e.reshape(S5_LB, wide, LANES), exp_pn, row_grp_lane == st_idx(1))

    m_f_w = state_in(0, np.arange(T - 1, -1, -1))
    m_b_w = state_in(1, np.arange(T))

    def state_out(d, taus):
        r = cp_re[d][..., taus]
        im = -cp_im[d][..., taus]
        w = jnp.stack([r, im], axis=0).reshape(2, S5_LB, 4, 2, S5_GC, S5_STATE, T)
        w = jnp.transpose(w, (1, 2, 0, 3, 5, 6, 4))
        return expand(w.reshape(S5_LB, S5_SW, LANES), exp_tk, st_idx(0) == col_grp_lane)

    n_f_w = state_out(0, np.arange(1, T + 1))
    n_b_w = state_out(1, np.arange(T, 0, -1))

    q_re, q_im = powers(T * np.arange(8))

    def table(arr_re, arr_im, d, order):
        t = jnp.stack([arr_re[d][..., order], arr_im[d][..., order]], axis=0)
        t = t.reshape(2, 16, 2, S5_STATE, len(order))
        return jnp.transpose(t, (4, 1, 0, 2, 3)).reshape(len(order), 16 * 2 * LANES)

    asc = np.arange(8)
    dbl = np.array([1, 2, 4, 0, 0, 0, 0, 0])
    tabs = (table(q_re, q_im, 0, asc), table(q_re, q_im, 0, dbl),
            table(q_re, q_im, 1, asc[::-1]), table(q_re, q_im, 1, dbl))
    return (a_blk.astype(bf16), m_f_w.astype(bf16), m_b_w.astype(bf16), n_f_w.astype(bf16), n_b_w.astype(bf16),
            tuple(t.astype(f32) for t in tabs))


def _s5_load_chunks(u_ref):
    rows = u_ref.shape[0] // S5_T
    parts = [u_ref[pl.ds(s, rows, stride=S5_T), :] for s in range(S5_T)]
    return jnp.concatenate(parts, axis=1).astype(bf16)


def _s5_in_body(u_ref, mf_ref, mb_ref, xf_ref, xb_ref):
    lhs = _s5_load_chunks(u_ref)
    xf_ref[...] = _mm(lhs, mf_ref[...])
    xb_ref[...] = _mm(lhs, mb_ref[...])


def _s5_in(proj, m_f_w, m_b_w, tl):
    b, l, _ = proj.shape
    nc = l // S5_T
    out = jax.ShapeDtypeStruct((b, nc, S5_LB * S5_SW), f32)
    wspec = pl.BlockSpec((None, S5_T * LANES, S5_SW), lambda bi, j, t: (j, 0, 0))
    ospec = pl.BlockSpec((None, tl // S5_T, S5_SW), lambda bi, j, t: (bi, t, j))
    return pl.pallas_call(
        _s5_in_body, name="s5_in",
        out_shape=(out, out),
        grid=(b, S5_LB, l // tl),
        in_specs=[pl.BlockSpec((None, tl, LANES), lambda bi, j, t: (bi, t, B_S5 // LANES + j)), wspec, wspec],
        out_specs=(ospec, ospec),
        compiler_params=_cparams("parallel", "parallel", "parallel"),
    )(proj, m_f_w, m_b_w)


def _s5_scan_body(xf_ref, xb_ref, cf_ref, hf_ref, cb_ref, hb_ref, sf_ref, sb_ref):
    n8 = xf_ref.shape[0] // SUBLANES
    ncb = xf_ref.shape[1] // (2 * LANES)
    rows = lax.broadcasted_iota(i32, (SUBLANES, LANES), 0)

    def shifted(x, d, fwd):
        if fwd:
            return jnp.where(rows >= d, pltpu.roll(x, d, 0), 0.0)
        return jnp.where(rows < SUBLANES - d, pltpu.roll(x, SUBLANES - d, 0), 0.0)

    def local_scan(xr, xi, h_ref, re, im, fwd):
        er, ei = shifted(xr, 1, fwd), shifted(xi, 1, fwd)
        for k, d in enumerate((1, 2, 4)):
            ar, ai = h_ref[k:k + 1, re], h_ref[k:k + 1, im]
            sr, si = shifted(er, d, fwd), shifted(ei, d, fwd)
            er, ei = er + ar * sr - ai * si, ei + ar * si + ai * sr
        return er, ei

    def tile(x_ref, c_ref, h_ref, o_ref, i, c, sr, si, fwd):
        re = slice(2 * c * LANES, (2 * c + 1) * LANES)
        im = slice((2 * c + 1) * LANES, (2 * c + 2) * LANES)
        r0 = pl.multiple_of(i * SUBLANES, SUBLANES)
        xr, xi = x_ref[pl.ds(r0, SUBLANES), re], x_ref[pl.ds(r0, SUBLANES), im]
        er, ei = local_scan(xr, xi, h_ref, re, im, fwd)
        cr, ci = c_ref[:, re], c_ref[:, im]
        outr = er + cr * sr - ci * si
        outi = ei + cr * si + ci * sr
        o_ref[pl.ds(r0, SUBLANES), re] = outr
        o_ref[pl.ds(r0, SUBLANES), im] = outi
        e = SUBLANES - 1 if fwd else 0
        ar, ai = h_ref[0:1, re], h_ref[0:1, im]
        nr = ar * outr[e:e + 1] - ai * outi[e:e + 1] + xr[e:e + 1]
        ni = ar * outi[e:e + 1] + ai * outr[e:e + 1] + xi[e:e + 1]
        return nr, ni

    def step(i, carry):
        out = []
        for c in range(ncb):
            fr, fi, br, bi = carry[4 * c:4 * c + 4]
            fr, fi = tile(xf_ref, cf_ref, hf_ref, sf_ref, i, c, fr, fi, True)
            br, bi = tile(xb_ref, cb_ref, hb_ref, sb_ref, n8 - 1 - i, c, br, bi, False)
            out += [fr, fi, br, bi]
        return tuple(out)

    z = jnp.zeros((1, LANES), f32)
    lax.fori_loop(0, n8, step, (z,) * (4 * ncb))


def _s5_scan(xf, xb, tabs):
    b, nc, w = xf.shape
    bw = S5_SCAN_CB * 2 * LANES
    xspec = pl.BlockSpec((None, nc, bw), lambda bi, c: (bi, 0, c))
    tspec = pl.BlockSpec((SUBLANES, bw), lambda bi, c: (0, c))
    out = jax.ShapeDtypeStruct((b, nc, w), f32)
    return pl.pallas_call(
        _s5_scan_body, name="s5_scan",
        out_shape=(out, out),
        grid=(b, w // bw),
        in_specs=[xspec, xspec, tspec, tspec, tspec, tspec],
        out_specs=(xspec, xspec),
        compiler_params=_cparams("parallel", "parallel"),
    )(xf, xb, *tabs)


def _s5_out_body(u_ref, sf_ref, sb_ref, a_ref, nf_ref, nb_ref, y_ref):
    lhs = _s5_load_chunks(u_ref)
    y = (_mm(lhs, a_ref[...]) + _mm(sf_ref[...].astype(bf16), nf_ref[...])
         + _mm(sb_ref[...].astype(bf16), nb_ref[...]))
    rows = y.shape[0]
    for t in range(S5_T):
        y_ref[pl.ds(t, rows, stride=S5_T), :] = y[:, t * LANES:(t + 1) * LANES]


def _s5_out(proj, sf, sb, a_blk, n_f_w, n_b_w, tl):
    b, l, _ = proj.shape
    sspec = pl.BlockSpec((None, tl // S5_T, S5_SW), lambda bi, j, t: (bi, t, j))
    return pl.pallas_call(
        _s5_out_body, name="s5_out",
        out_shape=jax.ShapeDtypeStruct((b, l, BW), f32),
        grid=(b, S5_LB, l // tl),
        in_specs=[pl.BlockSpec((None, tl, LANES), lambda bi, j, t: (bi, t, B_S5 // LANES + j)), sspec, sspec,
                  pl.BlockSpec((None, S5_T * LANES, S5_T * LANES), lambda bi, j, t: (j, 0, 0)),
                  pl.BlockSpec((None, S5_SW, S5_T * LANES), lambda bi, j, t: (j, 0, 0)),
                  pl.BlockSpec((None, S5_SW, S5_T * LANES), lambda bi, j, t: (j, 0, 0))],
        out_specs=pl.BlockSpec((None, tl, LANES), lambda bi, j, t: (bi, t, j)),
        compiler_params=_cparams("parallel", "parallel", "parallel"),
    )(proj, sf, sb, a_blk, n_f_w, n_b_w)


def _s5_mixer(proj, s5w, tl=2048):
    a_blk, m_f_w, m_b_w, n_f_w, n_b_w, tabs = s5w
    tl = min(tl, proj.shape[1])
    xf, xb = _s5_in(proj, m_f_w, m_b_w, tl)
    sf, sb = _s5_scan(xf, xb, tabs)
    return _s5_out(proj, sf, sb, a_blk, n_f_w, n_b_w, tl)


def _gla_body(qf_ref, kf_ref, vf_ref, lrf_ref, qb_ref, kb_ref, vb_ref, lrb_ref, wgf_ref, bgf_ref, wgb_ref, bgb_ref,
              of_ref, ob_ref, st_ref):
    ins = ((qf_ref, kf_ref, vf_ref, lrf_ref, wgf_ref, bgf_ref, of_ref),
           (qb_ref, kb_ref, vb_ref, lrb_ref, wgb_ref, bgb_ref, ob_ref))
    nch = qf_ref.shape[0] // CHUNK
    hk = GLA_H * GLA_DK
    hv = GLA_H * GLA_DV

    @pl.when(pl.program_id(1) == 0)
    def _():
        st_ref[...] = jnp.zeros_like(st_ref)

    r64 = lax.broadcasted_iota(i32, (CHUNK, CHUNK), 0)
    c64 = lax.broadcasted_iota(i32, (CHUNK, CHUNK), 1)
    tris = (_ind(r64 >= c64), _ind(r64 <= c64))
    rr = lax.broadcasted_iota(i32, (CHUNK, hk), 0)
    cc = lax.broadcasted_iota(i32, (CHUNK, hk), 1) % CHUNK
    causals = (rr >= cc, rr <= cc)
    kmask = (lax.broadcasted_iota(i32, (hk, hk), 0) // CHUNK
             == lax.broadcasted_iota(i32, (hk, hk), 1) // GLA_DK)
    vmask = (lax.broadcasted_iota(i32, (hk, hv), 0) // CHUNK
             == lax.broadcasted_iota(i32, (hk, hv), 1) // GLA_DV)
    smask = (lax.broadcasted_iota(i32, (hv, hk), 0) // GLA_DV
             == lax.broadcasted_iota(i32, (hv, hk), 1) // GLA_DK)
    scale = GLA_DK ** -0.5
    zero = jnp.zeros((), bf16)

    def group(gi, carry):
        span = GLA_GROUP * CHUNK
        bases = (gi * span, (nch - (gi + 1) * GLA_GROUP) * CHUNK)
        gls = [_mm_hi(ins[d][3][pl.ds(pl.multiple_of(bases[d], span), span), :], ins[d][4][...]) + ins[d][5][...]
               for d in range(2)]
        chains = []
        for j in range(GLA_GROUP):
            for d in range(2):
                cj = gi * GLA_GROUP + j
                c = cj if d == 0 else nch - 1 - cj
                lo = (j if d == 0 else GLA_GROUP - 1 - j) * CHUNK
                chains.append(dict(d=d, rows=pl.ds(pl.multiple_of(c * CHUNK, CHUNK), CHUNK),
                                   gl=gls[d][lo:lo + CHUNK, :]))
        for ch in chains:
            gl = ch['gl']
            g = (jnp.minimum(gl, 0.0) - jnp.log(1.0 + jnp.exp(-jnp.abs(gl)))) * (1.0 / GLA_TAU)
            ch['gc'] = _mm_mask(tris[ch['d']], g)
        for ch in chains:
            r, rows, gc = ins[ch['d']], ch['rows'], ch['gc']
            gtot = gc[CHUNK - 1:CHUNK, :] if ch['d'] == 0 else gc[0:1, :]
            k = r[1][rows, :].astype(f32)
            ch['qd'] = (r[0][rows, :].astype(f32) * scale * jnp.exp(gc)).astype(bf16)
            ki = (k * jnp.exp(-gc)).astype(bf16)
            ch['kt'] = (k * jnp.exp(gtot - gc)).astype(bf16)
            ch['dec'] = jnp.exp(gtot)
            kstack = jnp.where(kmask, jnp.concatenate([ki] * GLA_H, axis=0), zero)
            ch['sc'] = jnp.where(causals[ch['d']], _nt(ch['qd'], kstack), 0.0).astype(bf16)
        for ch in chains:
            v = ins[ch['d']][2][ch['rows'], :].astype(bf16)
            vbd = jnp.where(vmask, jnp.concatenate([v] * GLA_H, axis=0), zero)
            ch['oi'] = _mm(ch['sc'], vbd)
            ch['kv'] = _tn(v, ch['kt'])
        for ch in chains:
            d = ch['d']
            st = st_ref[d]
            ins[d][6][ch['rows'], :] = (ch['oi'] + _nt(ch['qd'], st.astype(bf16))).astype(bf16)
            st_ref[d] = st * ch['dec'] + jnp.where(smask, ch['kv'], 0.0)
        return carry

    lax.fori_loop(0, nch // GLA_GROUP, group, 0)


def _gla(pa, pb, wgs, bgs, blk=512):
    b, l, _ = pa.shape
    blk = min(blk, l)
    nb = l // blk
    assert l % blk == 0 and (blk // CHUNK) % GLA_GROUP == 0
    hk, hv = GLA_H * GLA_DK, GLA_H * GLA_DV
    up, down = (lambda i: i), (lambda i: nb - 1 - i)

    def cols(bidx):
        col = lambda width, off: pl.BlockSpec((None, blk, width), lambda bi, i: (bi, bidx(i), off // width))
        return [col(hk, A_GLA_Q), col(hk, A_GLA_K), col(hv, A_GLA_V), col(LANES, B_GLA_LR)]

    wspec = pl.BlockSpec((LANES, hk), lambda bi, i: (0, 0))
    bspec = pl.BlockSpec((1, hk), lambda bi, i: (0, 0))
    out = jax.ShapeDtypeStruct((b, l, hv), bf16)
    return pl.pallas_call(
        _gla_body, name="gla",
        out_shape=(out, out),
        grid=(b, nb),
        in_specs=cols(up) + cols(down) + [wspec, bspec, wspec, bspec],
        out_specs=(pl.BlockSpec((None, blk, hv), lambda bi, i: (bi, up(i), 0)),
                   pl.BlockSpec((None, blk, hv), lambda bi, i: (bi, down(i), 0))),
        scratch_shapes=[pltpu.VMEM((2, hv, hk), f32)],
        compiler_params=_cparams("parallel", "arbitrary"),
    )(pa, pa, pa, pb, pa, pa, pa, pb, wgs[0], bgs[0], wgs[1], bgs[1])


def _gdn_prep_body(x_ref, xp_ref, xn_ref, ab_ref, cw_ref, par_ref, qkv_ref, gb_ref):
    i, n = pl.program_id(1), pl.num_programs(1)
    blk = x_ref.shape[0]
    halo = xp_ref.shape[0]
    prev = jnp.where(i > 0, xp_ref[...].astype(f32), 0.0)
    nxt = jnp.where(i < n - 1, xn_ref[...].astype(f32), 0.0)
    ext = jnp.concatenate([prev, x_ref[...].astype(f32), nxt], axis=0)
    tot = blk + 2 * halo
    acc = None
    for t in range(CONV_W):
        sh = (CONV_W // 2 - t) % tot
        xs = ext if sh == 0 else pltpu.roll(ext, sh, 0)
        term = xs[halo:halo + blk, :] * cw_ref[t:t + 1, :]
        acc = term if acc is None else acc + term
    y = acc + acc * jnp.tanh(acc)
    nqk = 2 * GDN_H
    for h in range(3 * GDN_H):
        sl = slice(h * LANES, (h + 1) * LANES)
        yh = y[:, sl]
        if h < nqk:
            yh = yh * lax.rsqrt(jnp.sum(yh * yh, axis=-1, keepdims=True) + EPS)
            if h < GDN_H:
                yh = yh * GDN_DK ** -0.5
        qkv_ref[:, sl] = yh.astype(qkv_ref.dtype)
    x = ab_ref[...]
    lane = lax.broadcasted_iota(i32, x.shape, 1)
    xa = x + par_ref[1:2, :]
    softplus = jnp.maximum(xa, 0.0) + jnp.log(1.0 + jnp.exp(-jnp.abs(xa)))
    gb_ref[...] = jnp.where(lane < nqk, par_ref[0:1, :] * softplus, _sigmoid(x))


def _gdn_prep(pa, pb, conv_w, par, blk=512):
    b, l, _ = pa.shape
    blk = min(blk, l)
    nb = l // blk
    w = 3 * BW
    halo = 2 * SUBLANES
    rh = blk // halo
    last = l // halo - 1
    return pl.pallas_call(
        _gdn_prep_body, name="gdn_prep",
        out_shape=(jax.ShapeDtypeStruct((b, l, w), bf16), jax.ShapeDtypeStruct((b, l, LANES), f32)),
        grid=(b, nb),
        in_specs=[pl.BlockSpec((None, blk, w), lambda bi, i: (bi, i, A_GDN_QKV // w)),
                  pl.BlockSpec((None, halo, w), lambda bi, i: (bi, jnp.maximum(i * rh - 1, 0), A_GDN_QKV // w)),
                  pl.BlockSpec((None, halo, w), lambda bi, i: (bi, jnp.minimum((i + 1) * rh, last), A_GDN_QKV // w)),
                  pl.BlockSpec((None, blk, LANES), lambda bi, i: (bi, i, B_GDN_AB // LANES)),
                  pl.BlockSpec((SUBLANES, w), lambda bi, i: (0, 0)),
                  pl.BlockSpec((SUBLANES, LANES), lambda bi, i: (0, 0))],
        out_specs=(pl.BlockSpec((None, blk, w), lambda bi, i: (bi, i, 0)),
                   pl.BlockSpec((None, blk, LANES), lambda bi, i: (bi, i, 0))),
        compiler_params=_cparams("parallel", "parallel"),
    )(pa, pa, pa, pb, conv_w, par)


def _pair_mm(x, y, bdmask):
    yb = y.astype(bf16)
    return _mm(x.astype(bf16), jnp.where(bdmask, jnp.concatenate([yb, yb], axis=0), jnp.zeros((), bf16)))


def _unit_tri_inverses(lws, eye, bd16, bdmask):
    mm = lambda a, b: _pair_mm(a, b, bdmask)
    lds = [jnp.where(bd16, lw, 0.0) for lw in lws]
    los = [lw - ld for lw, ld in zip(lws, lds)]
    ps = [eye - ld for ld in lds]
    pw = lds
    for _ in range(3):
        pw = [mm(x, x) for x in pw]
        ps = [p + mm(p, x) for p, x in zip(ps, pw)]
    ms = [mm(p, lo) for p, lo in zip(ps, los)]
    m2s = [mm(m, m) for m in ms]
    qs = [eye - m for m in ms]
    qs = [q + mm(q, m2) for q, m2 in zip(qs, m2s)]
    return [mm(q, p) for q, p in zip(qs, ps)]


def _gdn_par_body(q_ref, k_ref, v_ref, gb_ref, *out_refs):
    nch = q_ref.shape[0] // CHUNK
    r64 = lax.broadcasted_iota(i32, (CHUNK, CHUNK), 0)
    c64 = lax.broadcasted_iota(i32, (CHUNK, CHUNK), 1)
    tris = (_ind(r64 >= c64), _ind(r64 <= c64))
    rp = lax.broadcasted_iota(i32, (CHUNK, 2 * CHUNK), 0)
    lp = lax.broadcasted_iota(i32, (CHUNK, 2 * CHUNK), 1)
    fwd_half = lp < CHUNK
    cp = jnp.where(fwd_half, lp, lp - CHUNK)
    incl = (fwd_half & (rp >= cp)) | (~fwd_half & (rp <= cp))
    strict = (fwd_half & (rp > cp)) | (~fwd_half & (rp < cp))
    eye = _ind(rp == cp)
    bd16 = (rp // 16) == (cp // 16)
    bdmask = (lax.broadcasted_iota(i32, (2 * CHUNK, 2 * CHUNK), 0) // CHUNK
              == lax.broadcasted_iota(i32, (2 * CHUNK, 2 * CHUNK), 1) // CHUNK)
    zeros_rhs = jnp.zeros((CHUNK, 2 * GDN_DV), bf16)

    def chunk_group(ci, carry):
        chains = []
        for j in range(GDN_GROUP):
            c = ci * GDN_GROUP + j
            rows = pl.ds(pl.multiple_of(c * CHUNK, CHUNK), CHUNK)
            gcols = gb_ref[rows, :]
            gcols_b = pltpu.roll(gcols, LANES - GDN_H, 1)
            gam_f = _mm_hi(tris[0], gcols)
            gam_b = pltpu.roll(gam_f[CHUNK - 1:CHUNK, :] - gam_f + gcols, LANES - GDN_H, 1)
            grows = jnp.concatenate([gam_f.T[0:SUBLANES, :], gam_b.T[0:SUBLANES, :]], axis=1)
            out_refs[5][c] = jnp.exp(gam_f[CHUNK - 1:CHUNK, :])
            out_refs[11][c] = pltpu.roll(jnp.exp(gam_b[0:1, :]), GDN_H, 1)
            for h in range(GDN_H):
                sl = slice(h * LANES, (h + 1) * LANES)
                qb, kb16 = q_ref[rows, sl], k_ref[rows, sl]
                raw = _nt(jnp.concatenate([qb, kb16], axis=0), jnp.concatenate([kb16, kb16], axis=0))
                chains.append(dict(
                    rows=rows, h=h, raw=raw, qh=qb.astype(f32), kh=kb16.astype(f32), vh=v_ref[rows, sl].astype(f32),
                    gcols=(gam_f[:, h:h + 1], gam_b[:, h:h + 1]), grow=grows[h:h + 1, :],
                    betas=(gcols[:, 2 * GDN_H + h:2 * GDN_H + h + 1], gcols_b[:, 2 * GDN_H + h:2 * GDN_H + h + 1])))
        lws = []
        for ch in chains:
            rows, h, raw = ch['rows'], ch['h'], ch['raw']
            gcol = jnp.where(fwd_half, ch['gcols'][0], ch['gcols'][1])
            beta = jnp.where(fwd_half, ch['betas'][0], ch['betas'][1])
            dec = jnp.where(incl, jnp.exp(jnp.where(incl, gcol - ch['grow'], 0.0)), 0.0)
            attn = (raw[:CHUNK] * dec).astype(bf16)
            for d in range(2):
                out_refs[6 * d + 4][rows, h * CHUNK:(h + 1) * CHUNK] = attn[:, d * CHUNK:(d + 1) * CHUNK]
            lws.append(jnp.where(strict, raw[CHUNK:] * dec * beta, 0.0))
        tinvs = _unit_tri_inverses(lws, eye, bd16, bdmask)
        sols = []
        for ch, tinv in zip(chains, tinvs):
            ch['egs'] = tuple(jnp.exp(g) for g in ch['gcols'])
            rhs = []
            for d in range(2):
                kb = ch['kh'] * ch['betas'][d]
                rhs.append(jnp.concatenate([ch['vh'] * ch['betas'][d], kb * ch['egs'][d]], axis=1).astype(bf16))
            rhs2 = jnp.concatenate([jnp.concatenate([rhs[0], zeros_rhs], axis=1),
                                    jnp.concatenate([zeros_rhs, rhs[1]], axis=1)], axis=0)
            sols.append(_mm(tinv.astype(bf16), rhs2))
        for ch, sol in zip(chains, sols):
            rows = ch['rows']
            sl = slice(ch['h'] * LANES, (ch['h'] + 1) * LANES)
            for d in range(2):
                u_ref, w_ref, qd_ref, kt_ref = out_refs[6 * d:6 * d + 4]
                gcol = ch['gcols'][d]
                gtot = gcol[CHUNK - 1:CHUNK, :] if d == 0 else gcol[0:1, :]
                u_ref[rows, sl] = sol[:, 2 * d * GDN_DV:(2 * d + 1) * GDN_DV]
                w_ref[rows, sl] = sol[:, (2 * d + 1) * GDN_DV:(2 * d + 2) * GDN_DV].astype(bf16)
                qd_ref[rows, sl] = (ch['qh'] * ch['egs'][d]).astype(bf16)
                kt_ref[rows, sl] = (ch['kh'] * jnp.exp(gtot - gcol)).astype(bf16)
        return carry

    lax.fori_loop(0, nch // GDN_GROUP, chunk_group, 0)


def _gdn_par(qkv, gb, blk=512):
    b, l, _ = qkv.shape
    blk = min(blk, l)
    nch = blk // CHUNK
    assert l % blk == 0 and nch % GDN_GROUP == 0
    col = lambda j: pl.BlockSpec((None, blk, BW), lambda bi, i: (bi, i, j))
    wide = pl.BlockSpec((None, blk, BW), lambda bi, i: (bi, i, 0))
    one_dir_shapes = (jax.ShapeDtypeStruct((b, l, BW), f32), jax.ShapeDtypeStruct((b, l, BW), bf16),
                      jax.ShapeDtypeStruct((b, l, BW), bf16), jax.ShapeDtypeStruct((b, l, BW), bf16),
                      jax.ShapeDtypeStruct((b, l, GDN_H * CHUNK), bf16),
                      jax.ShapeDtypeStruct((b, l // CHUNK, 1, LANES), f32))
    one_dir_specs = (wide, wide, wide, wide,
                     pl.BlockSpec((None, blk, GDN_H * CHUNK), lambda bi, i: (bi, i, 0)),
                     pl.BlockSpec((None, nch, 1, LANES), lambda bi, i: (bi, i, 0, 0)))
    outs = pl.pallas_call(
        _gdn_par_body, name="gdn_par",
        out_shape=one_dir_shapes * 2,
        grid=(b, l // blk),
        in_specs=[col(0), col(1), col(2), pl.BlockSpec((None, blk, LANES), lambda bi, i: (bi, i, 0))],
        out_specs=one_dir_specs * 2,
        compiler_params=_cparams("parallel", "parallel"),
    )(qkv, qkv, qkv, gb)
    return outs[:6], outs[6:]


def _gdn_seq_body(*refs):
    ins = (refs[0:6], refs[6:12])
    o_refs = refs[12:14]
    s_ref = refs[14]
    nseq = o_refs[0].shape[0]
    nch = o_refs[0].shape[1] // CHUNK

    @pl.when(pl.program_id(0) == 0)
    def _():
        s_ref[...] = jnp.zeros_like(s_ref)

    def chunk(ci, carry):
        rows, cidx = [], []
        for d in range(2):
            c = ci if d == 0 else nch - 1 - ci
            cidx.append(c)
            rows.append(pl.ds(pl.multiple_of(c * CHUNK, CHUNK), CHUNK))
        ch = [(bi, d, h) for bi in range(nseq) for d in range(2) for h in range(GDN_H)]
        sl = lambda h: slice(h * LANES, (h + 1) * LANES)
        ss = [s_ref[bi, d, sl(h), :] for bi, d, h in ch]
        sbs = [s.astype(bf16) for s in ss]
        wss = [_mm(ins[d][1][bi, rows[d], sl(h)], sb) for (bi, d, h), sb in zip(ch, sbs)]
        qss = [_mm(ins[d][2][bi, rows[d], sl(h)], sb) for (bi, d, h), sb in zip(ch, sbs)]
        vns = [(ins[d][0][bi, rows[d], sl(h)] - ws).astype(bf16) for (bi, d, h), ws in zip(ch, wss)]
        avs = [_mm(ins[d][4][bi, rows[d], h * CHUNK:(h + 1) * CHUNK], vn) for (bi, d, h), vn in zip(ch, vns)]
        kvs = [_tn(ins[d][3][bi, rows[d], sl(h)], vn) for (bi, d, h), vn in zip(ch, vns)]
        for (bi, d, h), s, kv in zip(ch, ss, kvs):
            lg = d * GDN_H + h
            cd = ins[d][5][bi, cidx[d]]
            s_ref[bi, d, sl(h), :] = s * cd[:, lg:lg + 1] + kv
        for bi in range(nseq):
            for d in range(2):
                o_refs[d][bi, rows[d], :] = jnp.concatenate(
                    [qs + av for (b2, d2, _), qs, av in zip(ch, qss, avs) if (b2, d2) == (bi, d)],
                    axis=1).astype(bf16)
        return carry

    lax.fori_loop(0, nch, chunk, 0)


def _gdn_seq(fwd_in, bwd_in, blk=512):
    b, l, _ = fwd_in[0].shape
    blk = min(blk, l)
    nb = l // blk
    nch = blk // CHUNK

    def specs(bidx):
        wide = pl.BlockSpec((b, blk, BW), lambda i: (0, bidx(i), 0))
        return [wide, wide, wide, wide,
                pl.BlockSpec((b, blk, GDN_H * CHUNK), lambda i: (0, bidx(i), 0)),
                pl.BlockSpec((b, nch, 1, LANES), lambda i: (0, bidx(i), 0, 0))]

    up, down = (lambda i: i), (lambda i: nb - 1 - i)
    out = jax.ShapeDtypeStruct((b, l, BW), bf16)
    return pl.pallas_call(
        _gdn_seq_body, name="gdn_seq",
        out_shape=(out, out),
        grid=(nb,),
        in_specs=specs(up) + specs(down),
        out_specs=(pl.BlockSpec((b, blk, BW), lambda i: (0, up(i), 0)),
                   pl.BlockSpec((b, blk, BW), lambda i: (0, down(i), 0))),
        scratch_shapes=[pltpu.VMEM((b, 2, GDN_H * GDN_DK, GDN_DV), f32)],
        compiler_params=_cparams("arbitrary"),
    )(*fwd_in, *bwd_in)


def _head_norm_gate(o, gain, z):
    outs = []
    for h in range(BW // LANES):
        oh = o[:, h * LANES:(h + 1) * LANES]
        outs.append(oh * lax.rsqrt(jnp.mean(oh * oh, axis=-1, keepdims=True) + EPS) * gain)
    return jnp.concatenate(outs, axis=1) * _silu(z)


def _merge_body(h_ref, gate_ref, ch_ref, u_ref, glaf_ref, glab_ref, r_ref, gdnf_ref, gdnb_ref, z_ref,
                dsk_ref, wglu_ref, gng_ref, dng_ref, wbr_ref, wout_ref, o_ref):
    up = lambda ref: ref[...].astype(f32)
    y0 = _gelu_tanh(ch_ref[...] + dsk_ref[...] * u_ref[...])
    y_s5 = y0 * _sigmoid(_mm(y0.astype(bf16), wglu_ref[...]))
    y_gla = _head_norm_gate(up(glaf_ref) + up(glab_ref), gng_ref[...], up(r_ref))
    y_gdn = _head_norm_gate(up(gdnf_ref) + up(gdnb_ref), dng_ref[...], up(z_ref))
    merged = None
    for r, y in enumerate((y_s5, y_gla, y_gdn)):
        th = jnp.tanh(gate_ref[:, r * D_MODEL:(r + 1) * D_MODEL].astype(f32))
        t_half = _mm(y.astype(bf16), wbr_ref[r])
        term = t_half + t_half * th
        merged = term if merged is None else merged + term
    o_ref[...] = h_ref[...] + _mm(merged.astype(bf16), wout_ref[...])


def _merge(h2d, pa2d, pb2d, ch, gla_f, gla_b, gdn_f, gdn_b, dsk, wglu, gng, dng, wbr, wout, tm=512):
    m = h2d.shape[0]
    row = lambda width, off=0: pl.BlockSpec((tm, width), lambda i: (i, off // width))
    full = lambda shape: pl.BlockSpec(shape, lambda i: (0,) * len(shape))
    return pl.pallas_call(
        _merge_body, name="merge",
        out_shape=jax.ShapeDtypeStruct((m, D_MODEL), f32),
        grid=(m // tm,),
        in_specs=[row(D_MODEL), row(3 * D_MODEL, A_GATE), row(BW), row(BW, B_S5), row(BW), row(BW),
                  row(BW, A_GLA_R), row(BW), row(BW), row(BW, A_GDN_Z),
                  full((1, BW)), full((BW, BW)), full((1, LANES)), full((1, LANES)),
                  full((3, BW, D_MODEL)), full((D_MODEL, D_MODEL))],
        out_specs=row(D_MODEL),
        compiler_params=_cparams("parallel"),
    )(h2d, pa2d, ch, pb2d, gla_f, gla_b, pa2d, gdn_f, gdn_b, pa2d, dsk, wglu, gng, dng, wbr, wout)


def _router_body(h_ref, g_ref, wr_ref, hn_ref, aff_ref):
    hn = _rms(h_ref[...], g_ref[...])
    hn_hi = hn.astype(bf16)
    hn_ref[...] = hn_hi
    hn_lo = (hn - hn_hi.astype(f32)).astype(bf16)
    wr = wr_ref[...]
    wr_hi = wr.astype(bf16)
    wr_lo = (wr - wr_hi.astype(f32)).astype(bf16)
    logits = _nt(wr_hi, hn_hi) + (_nt(wr_hi, hn_lo) + _nt(wr_lo, hn_hi))
    e = jnp.exp(logits - jnp.max(logits, axis=0, keepdims=True))
    aff_ref[...] = e / jnp.sum(e, axis=0, keepdims=True)


def _router(h2d, gain, wr_t, tm=1024):
    m = h2d.shape[0]
    return pl.pallas_call(
        _router_body, name="router",
        out_shape=(jax.ShapeDtypeStruct((m, D_MODEL), bf16), jax.ShapeDtypeStruct((N_EXPERTS, m), f32)),
        grid=(m // tm,),
        in_specs=[pl.BlockSpec((tm, D_MODEL), lambda i: (i, 0)),
                  pl.BlockSpec((1, D_MODEL), lambda i: (0, 0)),
                  pl.BlockSpec((N_EXPERTS, D_MODEL), lambda i: (0, 0))],
        out_specs=(pl.BlockSpec((tm, D_MODEL), lambda i: (i, 0)), pl.BlockSpec((N_EXPERTS, tm), lambda i: (0, i))),
        compiler_params=_cparams("parallel"),
    )(h2d, gain, wr_t)


def _threshold_body(aff_ref, thr_ref, *, cap):
    keys = pltpu.bitcast(aff_ref[...], i32)

    def count(mask):
        return jnp.sum(jnp.where(mask, 1.0, 0.0), axis=1, keepdims=True).astype(i32)

    def bit(bi, t):
        cand = t | (1 << (30 - bi))
        return jnp.where(count(keys >= cand) >= cap, cand, t)

    t = lax.fori_loop(0, 31, bit, jnp.zeros((N_EXPERTS, 1), i32))
    budget = cap - count(keys > t)
    lane = lax.broadcasted_iota(i32, (N_EXPERTS, LANES), 1)
    thr_ref[...] = jnp.where(lane == 0, t, jnp.where(lane == 1, budget, 0))


def _threshold(aff_t, cap):
    n = aff_t.shape[1]
    return pl.pallas_call(
        functools.partial(_threshold_body, cap=cap), name="topc_threshold",
        out_shape=jax.ShapeDtypeStruct((N_EXPERTS, LANES), i32),
        in_specs=[pl.BlockSpec((N_EXPERTS, n), lambda: (0, 0))],
        out_specs=pl.BlockSpec((N_EXPERTS, LANES), lambda: (0, 0)),
        compiler_params=pltpu.CompilerParams(vmem_limit_bytes=VMEM_LIMIT_BYTES),
    )(aff_t)


def _slots_body(aff_ref, thr_ref, slot_ref, wts_ref, cnt_ref, run_ref):
    @pl.when(pl.program_id(0) == 0)
    def _():
        run_ref[...] = jnp.zeros_like(run_ref)

    tt = aff_ref.shape[1]
    aff = aff_ref[...]
    keys = pltpu.bitcast(aff, i32)
    t = thr_ref[:, 0:1]
    budget = thr_ref[:, 1:2]
    upper = _ind(lax.broadcasted_iota(i32, (tt, tt), 0) <= lax.broadcasted_iota(i32, (tt, tt), 1), bf16)
    eq = keys == t
    sel_run = run_ref[:, 0:1]
    tie_run = run_ref[:, 1:2]
    cs_eq = _mm(_ind(eq, bf16), upper).astype(i32)
    tie_rank = tie_run + cs_eq - 1
    sel = (keys > t) | (eq & (tie_rank < budget))
    cs_sel = _mm(_ind(sel, bf16), upper).astype(i32)
    slot_ref[...] = jnp.where(sel, sel_run + cs_sel - 1, -1)
    wts_ref[...] = jnp.where(sel, aff, 0.0)
    n_sel = cs_sel[:, tt - 1:tt]
    n_eq = cs_eq[:, tt - 1:tt]
    n_rows = ((n_sel + (SUBLANES - 1)) // SUBLANES) * SUBLANES
    cnt_ref[...] = jnp.broadcast_to(n_rows, cnt_ref.shape)
    lane = lax.broadcasted_iota(i32, run_ref.shape, 1)
    run_ref[...] = run_ref[...] + jnp.where(lane == 0, n_rows, jnp.where(lane == 1, n_eq, 0))


def _slots(aff_t, thr):
    n = aff_t.shape[1]
    nt = n // MOE_TT
    return pl.pallas_call(
        _slots_body, name="topc_slots",
        out_shape=(jax.ShapeDtypeStruct((N_EXPERTS, n), i32), jax.ShapeDtypeStruct((N_EXPERTS, n), f32),
                   jax.ShapeDtypeStruct((nt, N_EXPERTS, LANES), i32)),
        grid=(nt,),
        in_specs=[pl.BlockSpec((N_EXPERTS, MOE_TT), lambda i: (0, i)),
                  pl.BlockSpec((N_EXPERTS, LANES), lambda i: (0, 0))],
        out_specs=(pl.BlockSpec((N_EXPERTS, MOE_TT), lambda i: (0, i)),
                   pl.BlockSpec((N_EXPERTS, MOE_TT), lambda i: (0, i)),
                   pl.BlockSpec((None, N_EXPERTS, LANES), lambda i: (i, 0, 0))),
        scratch_shapes=[pltpu.VMEM((N_EXPERTS, LANES), i32)],
        compiler_params=_cparams("arbitrary"),
    )(aff_t, thr)


def _window_hits(slot_rows, starts, width):
    tt = slot_rows.shape[1]
    r = lax.broadcasted_iota(i32, (width, tt), 0)
    return [r == (slot_rows[e:e + 1, :] - starts[e]) for e in range(len(starts))]


def _dispatch_body(off_ref, x_ref, slot_ref, xe_hbm, buf_ref, xbuf_ref, sem, *, cap):
    i = pl.program_id(0)
    w = MOE_W
    x = x_ref[...]
    slot_rows = slot_ref[...]
    starts = [off_ref[e, i] for e in range(N_EXPERTS)]

    def window_copy(e, start, src):
        return pltpu.make_async_copy(src, xe_hbm.at[e, pl.ds(pl.multiple_of(start, SUBLANES), w)], sem.at[e])

    def window_wait(e):
        pltpu.make_async_copy(buf_ref.at[e], xe_hbm.at[e, pl.ds(0, w)], sem.at[e]).wait()

    @pl.when(i > 0)
    def _():
        for e in range(N_EXPERTS):
            window_wait(e)

    @pl.when(i == 0)
    def _():
        xbuf_ref[...] = jnp.zeros_like(xbuf_ref)
        tails = [pltpu.make_async_copy(xbuf_ref, xe_hbm.at[e, pl.ds(r0, MOE_PAD)], sem.at[N_EXPERTS])
                 for e in range(N_EXPERTS) for r0 in range(cap, xe_hbm.shape[1], MOE_PAD)]
        for cp in tails:
            cp.start()
        for cp in tails:
            cp.wait()

    hits = _window_hits(slot_rows, starts, w)
    lhs = jnp.concatenate([_ind(h, bf16) for h in hits], axis=0)
    rows = _mm(lhs, x).astype(bf16)
    for e in range(N_EXPERTS):
        buf_ref[e] = rows[e * w:(e + 1) * w, :]
        window_copy(e, starts[e], buf_ref.at[e]).start()

    for e in range(N_EXPERTS):
        n_rows = off_ref[e, i + 1] - starts[e]

        def extra(k, carry, e=e):
            start = starts[e] + k * w
            hit = _window_hits(slot_rows[e:e + 1, :], [start], w)[0]
            xbuf_ref[0:w, :] = _mm(_ind(hit, bf16), x).astype(bf16)
            cp = pltpu.make_async_copy(xbuf_ref.at[pl.ds(0, w)],
                                       xe_hbm.at[e, pl.ds(pl.multiple_of(start, SUBLANES), w)], sem.at[N_EXPERTS])
            cp.start()
            cp.wait()
            return carry

        lax.fori_loop(1, (n_rows + (w - 1)) // w, extra, 0)

    @pl.when(i == pl.num_programs(0) - 1)
    def _():
        for e in range(N_EXPERTS):
            window_wait(e)


def _dispatch(hn, slot, off, rows_alloc, cap):
    n = hn.shape[0]
    grid_spec = pltpu.PrefetchScalarGridSpec(
        num_scalar_prefetch=1,
        grid=(n // MOE_TT,),
        in_specs=[pl.BlockSpec((MOE_TT, D_MODEL), lambda i, o: (i, 0)),
                  pl.BlockSpec((N_EXPERTS, MOE_TT), lambda i, o: (0, i))],
        out_specs=pl.BlockSpec(memory_space=pl.ANY),
        scratch_shapes=[pltpu.VMEM((N_EXPERTS, MOE_W, D_MODEL), bf16), pltpu.VMEM((MOE_PAD, D_MODEL), bf16),
                        pltpu.SemaphoreType.DMA((N_EXPERTS + 1,))])
    return pl.pallas_call(
        functools.partial(_dispatch_body, cap=cap), name="dispatch",
        out_shape=jax.ShapeDtypeStruct((N_EXPERTS, rows_alloc, D_MODEL), bf16),
        grid_spec=grid_spec,
        compiler_params=_cparams("arbitrary"),
    )(off, hn, slot)


def _experts_body(tot_ref, x_ref, wg_ref, wu_ref, wd_ref, y_ref):
    e, j = pl.program_id(0), pl.program_id(1)
    fb = x_ref.shape[0]
    n_valid = tot_ref[e] - j * fb

    half = fb // 2

    def ffn(x, rows):
        hid = (_silu(_mm(x, wg_ref[...])) * _mm(x, wu_ref[...])).astype(bf16)
        y_ref[0:rows, :] = _mm(hid, wd_ref[...]).astype(y_ref.dtype)

    def masked(rows):
        row = lax.broadcasted_iota(i32, (rows, 1), 0)
        return jnp.where(row < n_valid, x_ref[0:rows, :], jnp.zeros((), bf16))

    @pl.when(n_valid >= fb)
    def _():
        ffn(x_ref[...], fb)

    @pl.when((n_valid > half) & (n_valid < fb))
    def _():
        ffn(masked(fb), fb)

    @pl.when((n_valid > 0) & (n_valid <= half))
    def _():
        ffn(masked(half), half)
        y_ref[half:fb, :] = jnp.zeros((fb - half, y_ref.shape[1]), y_ref.dtype)

    @pl.when(n_valid <= 0)
    def _():
        y_ref[...] = jnp.zeros_like(y_ref)


def _experts(xe, total, wg, wu, wd, rows):
    grid_spec = pltpu.PrefetchScalarGridSpec(
        num_scalar_prefetch=1,
        grid=(N_EXPERTS, rows // MOE_FB),
        in_specs=[pl.BlockSpec((None, MOE_FB, D_MODEL), lambda e, j, t: (e, j, 0)),
                  pl.BlockSpec((None, D_MODEL, EXPERT_FF), lambda e, j, t: (e, 0, 0)),
                  pl.BlockSpec((None, D_MODEL, EXPERT_FF), lambda e, j, t: (e, 0, 0)),
                  pl.BlockSpec((None, EXPERT_FF, D_MODEL), lambda e, j, t: (e, 0, 0))],
        out_specs=pl.BlockSpec((None, MOE_FB, D_MODEL), lambda e, j, t: (e, j, 0)))
    return pl.pallas_call(
        _experts_body, name="experts",
        out_shape=jax.ShapeDtypeStruct((N_EXPERTS, rows, D_MODEL), bf16),
        grid_spec=grid_spec,
        compiler_params=_cparams("parallel", "arbitrary"),
    )(total, xe, wg, wu, wd)


def _combine_body(off_ref, h_ref, slot_ref, wts_ref, p_ref, g_ref, wpg_ref, wpp_ref, ye_hbm, o_ref,
                  win_ref, xwin_ref, acc_ref, sem, *, rows):
    i, n = pl.program_id(0), pl.num_programs(0)
    w = MOE_W
    tt = h_ref.shape[0]

    def wstart(e, tile, k=0):
        return pl.multiple_of(jnp.minimum(off_ref[e, tile] + k * w, rows - w), SUBLANES)

    def window_copy(e, tile, par):
        return pltpu.make_async_copy(ye_hbm.at[e, pl.ds(wstart(e, tile), w)], win_ref.at[par, e], sem.at[par, e])

    @pl.when(i == 0)
    def _():
        for e in range(N_EXPERTS):
            window_copy(e, 0, 0).start()

    @pl.when(i + 1 < n)
    def _():
        for e in range(N_EXPERTS):
            window_copy(e, i + 1, (i + 1) % 2).start()

    par = i % 2
    slot_rows = slot_ref[...]
    wts_rows = wts_ref[...]
    r = lax.broadcasted_iota(i32, (w, tt), 0)
    hits, wins = [], []
    for e in range(N_EXPERTS):
        window_copy(e, i, par).wait()
        s_row = slot_rows[e:e + 1, :]
        lo = off_ref[e, i]
        hit = (r == s_row - wstart(e, i)) & (s_row < lo + w)
        gsel = jnp.where(hit, wts_rows[e:e + 1, :], 0.0)
        part = gsel[:, 0:LANES]
        for c in range(1, tt // LANES):
            part = part + gsel[:, c * LANES:(c + 1) * LANES]
        slot_gate = jnp.sum(part, axis=1, keepdims=True)
        hits.append(_ind(hit, bf16))
        wins.append((win_ref[par, e].astype(f32) * slot_gate).astype(bf16))
    acc_ref[...] = h_ref[...] + _tn(jnp.concatenate(hits, axis=0), jnp.concatenate(wins, axis=0))
    gates = jnp.transpose(wts_rows)

    for e in range(N_EXPERTS):
        lo = off_ref[e, i]
        n_rows = off_ref[e, i + 1] - lo

        def extra(k, carry, e=e, lo=lo):
            cp = pltpu.make_async_copy(ye_hbm.at[e, pl.ds(wstart(e, i, k), w)], xwin_ref, sem.at[2, 0])
            cp.start()
            cp.wait()
            s_row = slot_rows[e:e + 1, :]
            hit = (r == s_row - wstart(e, i, k)) & (s_row >= lo + k * w) & (s_row < lo + (k + 1) * w)
            acc_ref[...] += gates[:, e:e + 1] * _tn(_ind(hit, bf16), xwin_ref[...])
            return carry

        lax.fori_loop(1, (n_rows + (w - 1)) // w, extra, 0)

    h2 = acc_ref[...]
    gate = _sigmoid(_mm(_rms(h2, g_ref[...]).astype(bf16), wpg_ref[...]))
    o_ref[...] = h2 + gate * _mm(p_ref[...].astype(bf16), wpp_ref[...])


def _combine(h2d, slot, wts, ye, p2d, off, g_ple, wpg, wpp):
    m = h2d.shape[0]
    rows = ye.shape[1]
    full = lambda shape: pl.BlockSpec(shape, lambda i, o: (0,) * len(shape))
    grid_spec = pltpu.PrefetchScalarGridSpec(
        num_scalar_prefetch=1,
        grid=(m // MOE_TT,),
        in_specs=[pl.BlockSpec((MOE_TT, D_MODEL), lambda i, o: (i, 0)),
                  pl.BlockSpec((N_EXPERTS, MOE_TT), lambda i, o: (0, i)),
                  pl.BlockSpec((N_EXPERTS, MOE_TT), lambda i, o: (0, i)),
                  pl.BlockSpec((MOE_TT, PLE_DIM), lambda i, o: (i, 0)),
                  full((1, D_MODEL)), full((D_MODEL, D_MODEL)), full((PLE_DIM, D_MODEL)),
                  pl.BlockSpec(memory_space=pl.ANY)],
        out_specs=pl.BlockSpec((MOE_TT, D_MODEL), lambda i, o: (i, 0)),
        scratch_shapes=[pltpu.VMEM((2, N_EXPERTS, MOE_W, D_MODEL), bf16), pltpu.VMEM((MOE_W, D_MODEL), bf16),
                        pltpu.VMEM((MOE_TT, D_MODEL), f32), pltpu.SemaphoreType.DMA((3, N_EXPERTS))])
    return pl.pallas_call(
        functools.partial(_combine_body, rows=rows), name="combine_ple",
        out_shape=jax.ShapeDtypeStruct((m, D_MODEL), f32),
        grid_spec=grid_spec,
        compiler_params=_cparams("arbitrary"),
    )(off, h2d, slot, wts, p2d, g_ple, wpg, wpp, ye)


def _moe_ple(h2d, p2d, g_ffn, wr_t, wg, wu, wd, g_ple, wpg, wpp):
    n = h2d.shape[0]
    nt = n // MOE_TT
    cap = max(1, EC_FACTOR * n // N_EXPERTS)
    rows = -(-(cap + SUBLANES * nt) // MOE_FB) * MOE_FB
    assert n % MOE_TT == 0 and cap % MOE_PAD == 0 and MOE_W <= MOE_PAD and MOE_W <= cap
    hn, aff_t = _router(h2d, g_ffn, wr_t)
    thr = _threshold(aff_t, cap)
    slot, wts, cnt = _slots(aff_t, thr)
    off = jnp.concatenate([jnp.zeros((1, N_EXPERTS), i32), jnp.cumsum(cnt[:, :, 0], axis=0, dtype=i32)], axis=0)
    off = jnp.transpose(off)
    xe = _dispatch(hn, slot, off, rows + MOE_PAD, cap)
    ye = _experts(xe, off[:, nt], wg, wu, wd, rows)
    return _combine(h2d, slot, wts, ye, p2d, off, g_ple, wpg, wpp)


def _final_norm_body(x_ref, g_ref, o_ref):
    o_ref[...] = _rms(x_ref[...], g_ref[...])


def _final_norm(h2d, gain, tm=1024):
    m = h2d.shape[0]
    return pl.pallas_call(
        _final_norm_body, name="final_norm",
        out_shape=jax.ShapeDtypeStruct((m, D_MODEL), f32),
        grid=(m // tm,),
        in_specs=[pl.BlockSpec((tm, D_MODEL), lambda i: (i, 0)), pl.BlockSpec((1, D_MODEL), lambda i: (0, 0))],
        out_specs=pl.BlockSpec((tm, D_MODEL), lambda i: (i, 0)),
        compiler_params=_cparams("parallel"),
    )(h2d, gain)


def _layer_weights(w):
    lw = {}
    lw['g_mix'] = w['norm_mix'].reshape(1, D_MODEL)
    lw['w_in'] = _reorder_w_in(w['w_in'])
    lw['s5'] = _s5_weights(w['s5_B_re'], w['s5_B_im'], w['s5_C_re'], w['s5_C_im'],
                           w['s5_lam_re'], w['s5_lam_im'], w['s5_log_dt'])
    wgate = w['gla_w_gate']
    lw['gla_wg'] = tuple(jnp.zeros((LANES, GLA_H * GLA_DK), f32).at[d * GLA_RANK:(d + 1) * GLA_RANK].set(wgate[d])
                         for d in range(2))
    lw['gla_bg'] = tuple(w['gla_b_gate'][d].reshape(1, -1) for d in range(2))
    lw['conv_w'] = jnp.pad(0.5 * jnp.transpose(w['gdn_conv']), ((0, SUBLANES - CONV_W), (0, 0)))
    neg_a = -jnp.exp(w['gdn_A_log']).reshape(-1)
    par = jnp.zeros((SUBLANES, LANES), f32).at[0, :2 * GDN_H].set(neg_a).at[1, :2 * GDN_H].set(
        w['gdn_dt_bias'].reshape(-1))
    lw['gdn_par'] = par
    lw['dsk'] = w['s5_D'].reshape(1, BW)
    lw['wglu'] = w['s5_w_glu'].astype(bf16)
    lw['gng'] = w['gla_norm'].reshape(1, LANES)
    lw['dng'] = w['gdn_norm'].reshape(1, LANES)
    lw['wbr'] = (0.5 * w['w_branch']).astype(bf16)
    lw['wout'] = w['w_out'].astype(bf16)
    lw['g_ffn'] = w['norm_ffn'].reshape(1, D_MODEL)
    lw['wr_t'] = jnp.transpose(w['w_router'])
    lw['wg'] = w['w_exp_gate'].astype(bf16)
    lw['wu'] = w['w_exp_up'].astype(bf16)
    lw['wd'] = w['w_exp_down'].astype(bf16)
    lw['g_ple'] = w['norm_ple'].reshape(1, D_MODEL)
    lw['wpg'] = w['w_ple_gate'].astype(bf16)
    lw['wpp'] = w['w_ple_proj'].astype(bf16)
    return lw


def _mixers(h, lw):
    b, l, _ = h.shape
    h2d = h.reshape(b * l, D_MODEL)
    pa2d, pb2d = _inproj(h2d, lw['g_mix'], lw['w_in'])
    pa, pb = pa2d.reshape(b, l, D_A), pb2d.reshape(b, l, D_B)
    ch = _s5_mixer(pb, lw['s5'])
    gla_f, gla_b = _gla(pa, pb, lw['gla_wg'], lw['gla_bg'])
    qkv, gb = _gdn_prep(pa, pb, lw['conv_w'], lw['gdn_par'])
    gdn_f, gdn_b = _gdn_seq(*_gdn_par(qkv, gb))
    flat = lambda a: a.reshape(b * l, a.shape[-1])
    return _merge(h2d, pa2d, pb2d, flat(ch), flat(gla_f), flat(gla_b), flat(gdn_f), flat(gdn_b),
                  lw['dsk'], lw['wglu'], lw['gng'], lw['dng'], lw['wbr'], lw['wout'])


def _layer(h, p_i, lw):
    b, l, _ = h.shape
    h1 = _mixers(h, lw)
    h3 = _moe_ple(h1, p_i.reshape(b * l, PLE_DIM), lw['g_ffn'], lw['wr_t'], lw['wg'], lw['wu'], lw['wd'],
                  lw['g_ple'], lw['wpg'], lw['wpp'])
    return h3.reshape(b, l, D_MODEL)


def kernel(x_prompt, x_sample, p_prompt, p_sample, norm_mix, w_in, s5_B_re, s5_B_im, s5_C_re, s5_C_im, s5_D, s5_lam_re, s5_lam_im, s5_log_dt, s5_w_glu, gla_w_gate, gla_b_gate, gla_norm, gdn_conv, gdn_A_log, gdn_dt_bias, gdn_norm, w_branch, w_out, norm_ffn, w_router, w_exp_gate, w_exp_up, w_exp_down, norm_ple, w_ple_gate, w_ple_proj, norm_final):
    weights = dict(norm_mix=norm_mix, w_in=w_in, s5_B_re=s5_B_re, s5_B_im=s5_B_im, s5_C_re=s5_C_re, s5_C_im=s5_C_im,
                   s5_D=s5_D, s5_lam_re=s5_lam_re, s5_lam_im=s5_lam_im, s5_log_dt=s5_log_dt, s5_w_glu=s5_w_glu,
                   gla_w_gate=gla_w_gate, gla_b_gate=gla_b_gate, gla_norm=gla_norm, gdn_conv=gdn_conv,
                   gdn_A_log=gdn_A_log, gdn_dt_bias=gdn_dt_bias, gdn_norm=gdn_norm, w_branch=w_branch, w_out=w_out,
                   norm_ffn=norm_ffn, w_router=w_router, w_exp_gate=w_exp_gate, w_exp_up=w_exp_up,
                   w_exp_down=w_exp_down, norm_ple=norm_ple, w_ple_gate=w_ple_gate, w_ple_proj=w_ple_proj)

    def body(carry, xs):
        hp, hs = carry
        w_i, pp, ps = xs
        lw = _layer_weights(w_i)
        return (_layer(hp, pp, lw), _layer(hs, ps, lw)), None

    (hp, hs), _ = lax.scan(body, (x_prompt.astype(f32), x_sample.astype(f32)), (weights, p_prompt, p_sample))
    g_fin = norm_final.reshape(1, D_MODEL)
    yp = _final_norm(hp.reshape(-1, D_MODEL), g_fin).reshape(x_prompt.shape).astype(x_prompt.dtype)
    ys = _final_norm(hs.reshape(-1, D_MODEL), g_fin).reshape(x_sample.shape).astype(x_sample.dtype)
    return (yp, ys)
```
